```python
import jax
import jax.numpy as jnp
from jax import lax
import numpy as np

D_MODEL = 1024
BATCH = 2
SEQ = 8192
DEPTH = 2
DEC_BATCH = 128
DEC_SEQ = 8
PAST_LEN = 2048
PAGE_SIZE = 128

HEAD_DIM = 64
NSA_HEADS = 6
NSA_KV_HEADS = 2
NSA_GROUP = NSA_HEADS // NSA_KV_HEADS
MOBA_HEADS = 5
DSA_HEADS = 5
D_MIX = (NSA_HEADS + MOBA_HEADS + DSA_HEADS) * HEAD_DIM
CMP_LEN = 32
CMP_STRIDE = 16
CMP_HIDDEN = 256
SEL_BLOCK = 64
SEL_TOPN = 16
WINDOW = 512
MOBA_BLOCK = 256
MOBA_TOPK = 3
IDX_HEADS = 4
DSA_TOPK = 256
ROPE_THETA = 500000.0
ROT_DIM = HEAD_DIM // 4
N_GROUPS = 4
EXPERTS_PER_GROUP = 8
N_EXPERTS = N_GROUPS * EXPERTS_PER_GROUP
EXPERT_TOPK = 2
D_EXPERT = 256
ALPHA = (2 * DEPTH) ** 0.25
BETA = (8 * DEPTH) ** -0.25
LN_EPS = 1e-5
Q_BLOCK = 64
NEG = -1e30
BIG = 1e30
SPLIT_SIZES = (NSA_HEADS * HEAD_DIM, 6 * NSA_KV_HEADS * HEAD_DIM, 3 * NSA_HEADS,
               MOBA_HEADS * HEAD_DIM, MOBA_HEADS * HEAD_DIM, MOBA_HEADS * HEAD_DIM,
               DSA_HEADS * HEAD_DIM, 2 * HEAD_DIM, IDX_HEADS * HEAD_DIM, HEAD_DIM, IDX_HEADS)
N_IN = sum(SPLIT_SIZES)

kernel_name = "hybrid_nsa_moba_dsa_hiermoe_step"


def layer_norm(x, g=None, b=None):
    x32 = x.astype(jnp.float32)
    mu = jnp.mean(x32, axis=-1, keepdims=True)
    var = jnp.mean(jnp.square(x32 - mu), axis=-1, keepdims=True)
    y = (x32 - mu) * lax.rsqrt(var + LN_EPS)
    if g is not None:
        y = y * g + b
    return y.astype(x.dtype)


def rope(x, pos):
    half = ROT_DIM // 2
    inv = ROPE_THETA ** (-jnp.arange(0, ROT_DIM, 2, dtype=jnp.float32) / ROT_DIM)
    ang = pos.astype(jnp.float32)[:, None] * inv[None, :]
    cos = jnp.cos(ang)[:, None, :]
    sin = jnp.sin(ang)[:, None, :]
    xf = x[..., :ROT_DIM].astype(jnp.float32)
    x1, x2 = xf[..., :half], xf[..., half:]
    rot = jnp.concatenate([x1 * cos - x2 * sin, x2 * cos + x1 * sin], axis=-1).astype(x.dtype)
    return jnp.concatenate([rot, x[..., ROT_DIM:]], axis=-1)


def masked_softmax(s, mask):
    s = jnp.where(mask, s.astype(jnp.float32), NEG)
    p = jax.nn.softmax(s, axis=-1)
    return jnp.where(mask, p, 0.0)


def _mix_one(q_a, q_ar, g_a, q_b, q_c, q_i, w_i, nsa_kv, moba_kv, dsa_kv, win_kv,
             win_k0, cmp_pe, cmp_w1, cmp_w2):
    T = q_a.shape[0]
    L = nsa_kv.shape[0]
    q0 = L - T
    blk = Q_BLOCK if T % Q_BLOCK == 0 else T
    scale = HEAD_DIM ** -0.5
    take = jax.vmap(lambda rows, ix: rows[ix])

    n_cmp = (L - CMP_LEN) // CMP_STRIDE + 1
    c_start = jnp.arange(n_cmp) * CMP_STRIDE
    c_end = c_start + CMP_LEN - 1
    c_idx = c_start[:, None] + jnp.arange(CMP_LEN)[None, :]

    def compress(rows, j):
        b = rows[c_idx] + cmp_pe[j][None, :, None, :]
        b = b.transpose(0, 2, 1, 3).reshape(n_cmp, NSA_KV_HEADS, CMP_LEN * HEAD_DIM)
        return jax.nn.gelu(b @ cmp_w1[j]) @ cmp_w2[j]

    k_cmp = compress(nsa_kv[:, 0], 0)
    v_cmp = compress(nsa_kv[:, 1], 1)
    n_sel = -(-L // SEL_BLOCK)
    n_top = min(SEL_TOPN, n_sel)
    s_start = jnp.arange(n_sel) * SEL_BLOCK
    overlap = ((c_start[:, None] < s_start[None, :] + SEL_BLOCK)
               & (c_end[:, None] >= s_start[None, :])).astype(jnp.float32)
    sel_rows = jnp.pad(nsa_kv[:, 2:4], ((0, n_sel * SEL_BLOCK - L), (0, 0), (0, 0), (0, 0)))
    sel_rows = sel_rows.reshape(n_sel, SEL_BLOCK, 2, NSA_KV_HEADS, HEAD_DIM).transpose(2, 3, 0, 1, 4)
    win_rows = jnp.pad(win_kv, ((WINDOW, 0), (0, 0), (0, 0), (0, 0)))
    nb = -(-L // MOBA_BLOCK)
    m_top = min(MOBA_TOPK, nb)
    moba_rows = jnp.pad(moba_kv, ((0, nb * MOBA_BLOCK - L + blk), (0, 0), (0, 0), (0, 0)))
    moba_blocks = moba_rows[:nb * MOBA_BLOCK].reshape(
        nb, MOBA_BLOCK, 2, MOBA_HEADS, HEAD_DIM).transpose(2, 3, 0, 1, 4)
    k_mean = jnp.mean(moba_blocks[0].astype(jnp.float32), axis=2)
    k_d, v_d, k_idx = dsa_kv[:, 0], dsa_kv[:, 1], dsa_kv[:, 2]
    k_top = min(DSA_TOPK, L // 4)
    key_pos = jnp.arange(L)

    def block(i):
        t0 = q0 + i * blk
        t = t0 + jnp.arange(blk)
        cut = lambda a: lax.dynamic_slice_in_dim(a, i * blk, blk, 0)

        qa = cut(q_a).reshape(blk, NSA_KV_HEADS, NSA_GROUP, HEAD_DIM)
        qr = cut(q_ar).reshape(blk, NSA_KV_HEADS, NSA_GROUP, HEAD_DIM)
        p_c = masked_softmax(jnp.einsum('qkgd,ckd->qkgc', qa, k_cmp) * scale,
                             (c_end[None, :] <= t[:, None])[:, None, None, :])
        o_cmp = jnp.einsum('qkgc,ckd->qkgd', p_c.astype(v_cmp.dtype), v_cmp)
        imp = jnp.einsum('qkgc,cn->qkn', p_c, overlap)
        j = jnp.arange(n_sel)[None, :]
        tb = (t // SEL_BLOCK)[:, None]
        forced = (j == 0) | (j == tb) | (j == tb - 1)
        admissible = j * SEL_BLOCK <= t[:, None]
        imp = jnp.where(forced[:, None, :], BIG, jnp.where(admissible[:, None, :], imp, NEG))
        _, sel = lax.top_k(imp, n_top)
        sel_kv = sel.transpose(1, 0, 2)
        k_g = take(sel_rows[0], sel_kv)
        v_g = take(sel_rows[1], sel_kv)
        s_pos = sel[..., None] * SEL_BLOCK + jnp.arange(SEL_BLOCK)
        m_s = (s_pos <= t[:, None, None, None]).reshape(blk, NSA_KV_HEADS, 1, n_top * SEL_BLOCK)
        s_s = jnp.einsum('qkgd,kqjsd->qkgjs', qr, k_g).reshape(
            blk, NSA_KV_HEADS, NSA_GROUP, n_top * SEL_BLOCK) * scale
        p_s = masked_softmax(s_s, m_s).reshape(blk, NSA_KV_HEADS, NSA_GROUP, n_top, SEL_BLOCK)
        o_sel = jnp.einsum('qkgjs,kqjsd->qkgd', p_s.astype(v_g.dtype), v_g)
        w = lax.dynamic_slice_in_dim(win_rows, t0 - win_k0, WINDOW + blk, 0)
        w_pos = t0 - WINDOW + jnp.arange(WINDOW + blk)
        dist = t[:, None] - w_pos[None, :]
        m_w = (dist >= 0) & (dist <= WINDOW) & (w_pos[None, :] >= win_k0)
        p_w = masked_softmax(jnp.einsum('qkgd,skd->qkgs', qr, w[:, 0]) * scale, m_w[:, None, None, :])
        o_win = jnp.einsum('qkgs,skd->qkgd', p_w.astype(w.dtype), w[:, 1])
        g = jax.nn.sigmoid(cut(g_a).astype(jnp.float32)).astype(o_cmp.dtype).reshape(
            blk, NSA_KV_HEADS, NSA_GROUP, 3)
        o_nsa = g[..., 0:1] * o_cmp + g[..., 1:2] * o_sel + g[..., 2:3] * o_win

        qm = cut(q_b)
        own = t // MOBA_BLOCK
        s_blk = jnp.einsum('qhd,hnd->qhn', qm.astype(jnp.float32), k_mean)
        past_ok = jnp.arange(nb)[None, :] < own[:, None]
        s_blk = jnp.where(past_ok[:, None, :], s_blk, NEG)
        _, bsel = lax.top_k(s_blk, m_top)
        bsel_h = bsel.transpose(1, 0, 2)
        kb = take(moba_blocks[0], bsel_h)
        vb = take(moba_blocks[1], bsel_h)
        m1 = jnp.broadcast_to((bsel < own[:, None, None])[..., None],
                              (blk, MOBA_HEADS, m_top, MOBA_BLOCK)).reshape(blk, MOBA_HEADS, m_top * MOBA_BLOCK)
        s1 = (jnp.einsum('qhd,hqjsd->qhjs', qm, kb).reshape(blk, MOBA_HEADS, m_top * MOBA_BLOCK)
              * scale).astype(jnp.float32)
        o_start = (t0 // MOBA_BLOCK) * MOBA_BLOCK
        ob = lax.dynamic_slice_in_dim(moba_rows, o_start, MOBA_BLOCK + blk, 0)
        o_pos = o_start + jnp.arange(MOBA_BLOCK + blk)
        m2 = ((o_pos[None, :] // MOBA_BLOCK) == own[:, None]) & (o_pos[None, :] <= t[:, None])
        m2 = jnp.broadcast_to(m2[:, None, :], (blk, MOBA_HEADS, MOBA_BLOCK + blk))
        s2 = (jnp.einsum('qhd,shd->qhs', qm, ob[:, 0]) * scale).astype(jnp.float32)
        p_m = masked_softmax(jnp.concatenate([s1, s2], axis=-1),
                             jnp.concatenate([m1, m2], axis=-1)).astype(vb.dtype)
        o_moba = (jnp.einsum('qhjs,hqjsd->qhd',
                             p_m[..., :m_top * MOBA_BLOCK].reshape(blk, MOBA_HEADS, m_top, MOBA_BLOCK), vb)
                  + jnp.einsum('qhs,shd->qhd', p_m[..., m_top * MOBA_BLOCK:], ob[:, 1]))

        qi = cut(q_i).astype(jnp.float32)
        relu_s = jax.nn.relu(jnp.einsum('qhd,sd->qhs', qi, k_idx.astype(jnp.float32)))
        score = jnp.einsum('qh,qhs->qs', cut(w_i).astype(jnp.float32), relu_s)
        score = jnp.where(key_pos[None, :] <= t[:, None], score, NEG)
        _, dsel = lax.top_k(score, k_top)
        kd = k_d[dsel]
        vd = v_d[dsel]
        p_d = masked_softmax(jnp.einsum('qhd,qkd->qhk', cut(q_c), kd) * scale,
                             (dsel <= t[:, None])[:, None, :])
        o_dsa = jnp.einsum('qhk,qkd->qhd', p_d.astype(vd.dtype), vd)

        return jnp.concatenate([o_nsa.reshape(blk, -1), o_moba.reshape(blk, -1),
                                o_dsa.reshape(blk, -1)], axis=-1)

    out = lax.map(block, jnp.arange(T // blk))
    return out.reshape(T, D_MIX)


def token_mixers(h, pos0, w_in, w_out, cmp_pe, cmp_w1, cmp_w2, past):
    B, T, _ = h.shape
    pos = pos0 + jnp.arange(T)
    offsets = [int(v) for v in np.cumsum(SPLIT_SIZES)[:-1]]
    parts = jnp.split(h @ w_in, offsets, axis=-1)
    q_a = parts[0].reshape(B, T, NSA_HEADS, HEAD_DIM)
    kv_a = parts[1].reshape(B, T, 6, NSA_KV_HEADS, HEAD_DIM)
    g_a = parts[2].reshape(B, T, NSA_HEADS, 3)
    q_b = rope(parts[3].reshape(B, T, MOBA_HEADS, HEAD_DIM), pos)
    k_b = rope(parts[4].reshape(B, T, MOBA_HEADS, HEAD_DIM), pos)
    v_b = parts[5].reshape(B, T, MOBA_HEADS, HEAD_DIM)
    q_c = rope(parts[6].reshape(B, T, DSA_HEADS, HEAD_DIM), pos)
    kv_c = parts[7].reshape(B, T, 2, HEAD_DIM)
    q_i = rope(parts[8].reshape(B, T, IDX_HEADS, HEAD_DIM), pos)
    k_i = rope(parts[9].reshape(B, T, 1, HEAD_DIM), pos)[:, :, 0]
    w_i = parts[10]
    q_ar = rope(q_a, pos)

    nsa_new = jnp.stack([kv_a[:, :, 0], kv_a[:, :, 1], rope(kv_a[:, :, 2], pos), kv_a[:, :, 3]], axis=2)
    win_new = jnp.stack([rope(kv_a[:, :, 4], pos), kv_a[:, :, 5]], axis=2)
    moba_new = jnp.stack([k_b, v_b], axis=2)
    dsa_new = jnp.stack([rope(kv_c[:, :, 0:1], pos)[:, :, 0], kv_c[:, :, 1], k_i], axis=2)
    qs = (q_a, q_ar, g_a, q_b, q_c, q_i, w_i)

    if past is None:
        o = lax.map(lambda a: _mix_one(*a, 0, cmp_pe, cmp_w1, cmp_w2),
                    qs + (nsa_new, moba_new, dsa_new, win_new))
        win_out = win_new[:, -min(WINDOW, T):]
    else:
        c_nsa, c_moba, c_dsa, w_buf, table = past
        win_k0 = pos0 - w_buf.shape[1]

        def one(a):
            (qa_, qar_, ga_, qb_, qc_, qi_, wi_, nsa_n, moba_n, dsa_n, win_n, buf, row) = a

            def gather_past(pool):
                rows = pool[row]
                return rows.reshape((-1,) + rows.shape[2:])

            return _mix_one(qa_, qar_, ga_, qb_, qc_, qi_, wi_,
                            jnp.concatenate([gather_past(c_nsa), nsa_n], axis=0),
                            jnp.concatenate([gather_past(c_moba), moba_n], axis=0),
                            jnp.concatenate([gather_past(c_dsa), dsa_n], axis=0),
                            jnp.concatenate([buf, win_n], axis=0),
                            win_k0, cmp_pe, cmp_w1, cmp_w2)

        o = lax.map(one, qs + (nsa_new, moba_new, dsa_new, win_new, w_buf, table))
        win_out = jnp.concatenate([w_buf, win_new], axis=1)[:, -w_buf.shape[1]:]
    o = o.reshape(B, T, D_MIX) @ w_out
    return o, nsa_new, moba_new, dsa_new, win_out


def hier_moe(h, w_rg, b_rg, w_re, b_re, w_ei, w_eo):
    n = h.shape[0]
    p_group = jax.nn.softmax((h @ w_rg + b_rg).astype(jnp.float32), axis=-1)
    g_w, g_idx = lax.top_k(p_group, 1)
    e_logit = (h @ w_re + b_re).astype(jnp.float32).reshape(n, N_GROUPS, EXPERTS_PER_GROUP)
    e_logit = jnp.take_along_axis(e_logit, g_idx[:, :, None], axis=1)[:, 0]
    e_top, e_idx = lax.top_k(jax.nn.softmax(e_logit, axis=-1), EXPERT_TOPK)
    weight = g_w * e_top / jnp.sum(e_top, axis=-1, keepdims=True)
    expert = g_idx * EXPERTS_PER_GROUP + e_idx
    gate = jnp.einsum('nk,nke->ne', weight,
                      jax.nn.one_hot(expert, N_EXPERTS, dtype=jnp.float32)).astype(h.dtype)
    y = jnp.zeros_like(h)
    for e in range(N_EXPERTS):
        a, b = jnp.split(h @ w_ei[e], 2, axis=-1)
        y = y + gate[:, e:e + 1] * ((jax.nn.silu(a) * b) @ w_eo[e])
    return y


def trunk_layer(x, c, l, p, pos0, past):
    mod = (jax.nn.silu(c) @ p['w_ada'][l] + p['b_ada'][l])[:, None, :]
    sh1, sc1, g1, sh2, sc2, g2 = jnp.split(mod, 6, axis=-1)
    h = layer_norm(x) * (1 + sc1) + sh1
    o, nsa_r, moba_r, dsa_r, win_r = token_mixers(h, pos0, p['w_in'][l], p['w_out'][l], p['cmp_pe'][l],
                                                  p['cmp_w1'][l], p['cmp_w2'][l], past)
    x = layer_norm(ALPHA * x + g1 * o, p['ln1_g'][l], p['ln1_b'][l])
    h = layer_norm(x) * (1 + sc2) + sh2
    B, T, D = x.shape
    f = hier_moe(h.reshape(B * T, D), p['w_router_group'][l], p['b_router_group'][l],
                 p['w_router_expert'][l], p['b_router_expert'][l],
                 p['w_expert_in'][l], p['w_expert_out'][l]).reshape(B, T, D)
    x = layer_norm(ALPHA * x + g2 * f, p['ln2_g'][l], p['ln2_b'][l])
    return x, nsa_r, moba_r, dsa_r, win_r


def setup_inputs(seed: int = 0) -> dict:
    key = jax.random.key(seed)
    ks = jax.random.split(key, 26)
    nrm = lambda k, shape, s: jax.random.normal(k, shape, jnp.float32) * s
    n_pages = PAST_LEN // PAGE_SIZE
    n_pool = (5 * DEC_BATCH * n_pages + 3) // 4
    w_buf = min(WINDOW, PAST_LEN)
    page_table = jax.random.permutation(ks[8], n_pool)[:DEC_BATCH * n_pages].reshape(
        DEC_BATCH, n_pages).astype(jnp.int32)
    return {
        'x_prompt': nrm(ks[0], (BATCH, SEQ, D_MODEL), 1.0),
        'x_sample': nrm(ks[1], (DEC_BATCH, DEC_SEQ, D_MODEL), 1.0),
        'c_prompt': nrm(ks[2], (BATCH, D_MODEL), 1.0),
        'c_sample': nrm(ks[3], (DEC_BATCH, D_MODEL), 1.0),
        'cache_nsa': nrm(ks[4], (DEPTH, n_pool, PAGE_SIZE, 4, NSA_KV_HEADS, HEAD_DIM), 1.0),
        'cache_moba': nrm(ks[5], (DEPTH, n_pool, PAGE_SIZE, 2, MOBA_HEADS, HEAD_DIM), 1.0),
        'cache_dsa': nrm(ks[6], (DEPTH, n_pool, PAGE_SIZE, 3, HEAD_DIM), 1.0),
        'state_win': nrm(ks[7], (DEPTH, DEC_BATCH, w_buf, 2, NSA_KV_HEADS, HEAD_DIM), 1.0),
        'page_table': page_table,
        'w_ada': nrm(ks[9], (DEPTH, D_MODEL, 6 * D_MODEL), 0.5 * D_MODEL ** -0.5),
        'b_ada': nrm(ks[10], (DEPTH, 6 * D_MODEL), 0.02),
        'w_in': nrm(ks[11], (DEPTH, D_MODEL, N_IN), D_MODEL ** -0.5),
        'cmp_pe': nrm(ks[12], (DEPTH, 2, CMP_LEN, HEAD_DIM), 0.1),
        'cmp_w1': nrm(ks[13], (DEPTH, 2, CMP_LEN * HEAD_DIM, CMP_HIDDEN), (CMP_LEN * HEAD_DIM) ** -0.5),
        'cmp_w2': nrm(ks[14], (DEPTH, 2, CMP_HIDDEN, HEAD_DIM), CMP_HIDDEN ** -0.5),
        'w_out': nrm(ks[15], (DEPTH, D_MIX, D_MODEL), BETA * D_MIX ** -0.5),
        'ln1_g': 1.0 + nrm(ks[16], (DEPTH, D_MODEL), 0.02),
        'ln1_b': nrm(ks[17], (DEPTH, D_MODEL), 0.02),
        'ln2_g': 1.0 + nrm(ks[18], (DEPTH, D_MODEL), 0.02),
        'ln2_b': nrm(ks[19], (DEPTH, D_MODEL), 0.02),
        'w_router_group': nrm(ks[20], (DEPTH, D_MODEL, N_GROUPS), D_MODEL ** -0.5),
        'b_router_group': nrm(ks[21], (DEPTH, N_GROUPS), 0.01),
        'w_router_expert': nrm(ks[22], (DEPTH, D_MODEL, N_EXPERTS), D_MODEL ** -0.5),
        'b_router_expert': nrm(ks[23], (DEPTH, N_EXPERTS), 0.01),
        'w_expert_in': nrm(ks[24], (DEPTH, N_EXPERTS, D_MODEL, 2 * D_EXPERT), D_MODEL ** -0.5),
        'w_expert_out': nrm(ks[25], (DEPTH, N_EXPERTS, D_EXPERT, D_MODEL), BETA * D_EXPERT ** -0.5),
    }


def reference(x_prompt, x_sample, c_prompt, c_sample, cache_nsa, cache_moba, cache_dsa, state_win,
              page_table, w_ada, b_ada, w_in, cmp_pe, cmp_w1, cmp_w2, w_out, ln1_g, ln1_b, ln2_g,
              ln2_b, w_router_group, b_router_group, w_router_expert, b_router_expert,
              w_expert_in, w_expert_out):
    p = {'w_ada': w_ada, 'b_ada': b_ada, 'w_in': w_in, 'cmp_pe': cmp_pe, 'cmp_w1': cmp_w1,
         'cmp_w2': cmp_w2, 'w_out': w_out, 'ln1_g': ln1_g, 'ln1_b': ln1_b, 'ln2_g': ln2_g,
         'ln2_b': ln2_b, 'w_router_group': w_router_group, 'b_router_group': b_router_group,
         'w_router_expert': w_router_expert, 'b_router_expert': b_router_expert,
         'w_expert_in': w_expert_in, 'w_expert_out': w_expert_out}
    past_len = page_table.shape[1] * cache_nsa.shape[2]
    y_prompt, y_sample = x_prompt, x_sample
    st_p, st_s = [], []
    for l in range(DEPTH):
        y_prompt, *sp = trunk_layer(y_prompt, c_prompt, l, p, 0, None)
        y_sample, *ss = trunk_layer(y_sample, c_sample, l, p, past_len,
                                    (cache_nsa[l], cache_moba[l], cache_dsa[l], state_win[l], page_table))
        st_p.append(sp)
        st_s.append(ss)
    nsa_rows_prompt = jnp.stack([s[0] for s in st_p])
    nsa_rows_sample = jnp.stack([s[0] for s in st_s])
    moba_rows_prompt = jnp.stack([s[1] for s in st_p])
    moba_rows_sample = jnp.stack([s[1] for s in st_s])
    dsa_rows_prompt = jnp.stack([s[2] for s in st_p])
    dsa_rows_sample = jnp.stack([s[2] for s in st_s])
    win_prompt = jnp.stack([s[3] for s in st_p])
    win_sample = jnp.stack([s[3] for s in st_s])
    return (y_prompt, y_sample, nsa_rows_prompt, nsa_rows_sample, moba_rows_prompt, moba_rows_sample,
            dsa_rows_prompt, dsa_rows_sample, win_prompt, win_sample)
```

```python
import functools

import numpy as np
import jax
import jax.numpy as jnp
from jax import lax
from jax.experimental import pallas as pl
from jax.experimental.pallas import tpu as pltpu

D_MODEL = 1024
DEPTH = 2
PAGE_SIZE = 128
HEAD_DIM = 64
NSA_HEADS = 6
NSA_KV_HEADS = 2
NSA_GROUP = NSA_HEADS // NSA_KV_HEADS
MOBA_HEADS = 5
DSA_HEADS = 5
D_MIX = (NSA_HEADS + MOBA_HEADS + DSA_HEADS) * HEAD_DIM
CMP_LEN = 32
CMP_STRIDE = 16
CMP_HIDDEN = 256
SEL_BLOCK = 64
SEL_TOPN = 16
WINDOW = 512
MOBA_BLOCK = 256
MOBA_TOPK = 3
IDX_HEADS = 4
DSA_TOPK = 256
ROPE_THETA = 500000.0
ROT_DIM = HEAD_DIM // 4
N_GROUPS = 4
EXPERTS_PER_GROUP = 8
N_EXPERTS = N_GROUPS * EXPERTS_PER_GROUP
EXPERT_TOPK = 2
D_EXPERT = 256
ALPHA = (2 * DEPTH) ** 0.25
LN_EPS = 1e-5
Q_BLOCK = 64
NEG = -1e30
BIG = 1e30

LANES = 128
F32 = jnp.float32
BF16 = jnp.bfloat16

W_QA = NSA_HEADS * HEAD_DIM
W_KVA = 6 * NSA_KV_HEADS * HEAD_DIM
W_NSA = 4 * NSA_KV_HEADS * HEAD_DIM
W_WIN = 2 * NSA_KV_HEADS * HEAD_DIM
W_QBC = (MOBA_HEADS + DSA_HEADS) * HEAD_DIM
W_MOBA = 2 * MOBA_HEADS * HEAD_DIM
W_DSA = 3 * HEAD_DIM
W_DSA_PAD = 256
W_QI = IDX_HEADS * HEAD_DIM
W_GW = LANES
OFF_QA = 0
OFF_KVA = OFF_QA + W_QA
OFF_QBC = OFF_KVA + W_KVA
OFF_MOBA = OFF_QBC + W_QBC
OFF_DSA = OFF_MOBA + W_MOBA
OFF_QI = OFF_DSA + W_DSA_PAD
OFF_GW = OFF_QI + W_QI
W_PROJ = OFF_GW + W_GW

VMEM_LIMIT = 56 * 1024 * 1024


def _params(sem):
    return pltpu.CompilerParams(dimension_semantics=sem, vmem_limit_bytes=VMEM_LIMIT)


def _ln(x):
    mu = jnp.mean(x, axis=-1, keepdims=True)
    xc = x - mu
    var = jnp.mean(xc * xc, axis=-1, keepdims=True)
    return xc * lax.rsqrt(var + LN_EPS)


def _ada_kernel(c_ref, w_ref, b_ref, o_ref):
    c = c_ref[...]
    a = (c * jax.nn.sigmoid(c)).astype(BF16)
    o_ref[0] = jnp.dot(a, w_ref[0].astype(BF16), preferred_element_type=F32) + b_ref[0]


def _ada_call(c_all, w_ada, b_ada):
    n = c_all.shape[0]
    tn = 1536
    return pl.pallas_call(
        _ada_kernel,
        grid=(DEPTH, 6 * D_MODEL // tn),
        in_specs=[
            pl.BlockSpec((n, D_MODEL), lambda l, j: (0, 0)),
            pl.BlockSpec((1, D_MODEL, tn), lambda l, j: (l, 0, j)),
            pl.BlockSpec((1, 1, tn), lambda l, j: (l, 0, j)),
        ],
        out_specs=pl.BlockSpec((1, n, tn), lambda l, j: (l, 0, j)),
        out_shape=jax.ShapeDtypeStruct((DEPTH, n, 6 * D_MODEL), F32),
        compiler_params=_params(("arbitrary", "arbitrary")),
        name="ada_mod",
    )(c_all, w_ada, b_ada.reshape(DEPTH, 1, 6 * D_MODEL))


def _rope_tile(t, cos, s_lo, s_hi, first_half_only):
    if first_half_only:
        lane = lax.broadcasted_iota(jnp.int32, cos.shape, 2)
        keep = lane < HEAD_DIM
        cos = jnp.where(keep, cos, 1.0)
        s_lo = jnp.where(keep, s_lo, 0.0)
        s_hi = jnp.where(keep, s_hi, 0.0)
    half = ROT_DIM // 2
    up = pltpu.roll(t, LANES - half, axis=2)
    dn = pltpu.roll(t, half, axis=2)
    return t * cos + up * s_lo + dn * s_hi


def _rope_slab(p, cos, s_lo, s_hi, modes):
    tiles = []
    for j, m in enumerate(modes):
        t = p[:, :, j * LANES:(j + 1) * LANES]
        if m != 'n':
            t = _rope_tile(t, cos, s_lo, s_hi, m == 'h')
        tiles.append(t)
    return tiles[0] if len(tiles) == 1 else jnp.concatenate(tiles, axis=2)


def _proj_kernel(x_ref, sh_ref, sc_ref, w_ref, cos_ref, slo_ref, shi_ref,
                 qa_ref, qar_ref, nsa_ref, win_ref, qbc_ref, moba_ref, dsa_ref, qi_ref, gw_ref):
    x = x_ref[...]
    s, r, d = x.shape
    h = _ln(x) * (1.0 + sc_ref[...]) + sh_ref[...]
    hb = h.reshape(s * r, d).astype(BF16)
    cos, s_lo, s_hi = cos_ref[...], slo_ref[...], shi_ref[...]

    def seg(off, width):
        p = jnp.dot(hb, w_ref[:, off:off + width], preferred_element_type=F32)
        return p.reshape(s, r, width)

    rope = functools.partial(_rope_slab, cos=cos, s_lo=s_lo, s_hi=s_hi)
    qa = seg(OFF_QA, W_QA)
    qa_ref[...] = qa
    qar_ref[...] = rope(qa, modes='fff')
    kva = seg(OFF_KVA, W_KVA)
    nsa_ref[...] = rope(kva[:, :, :W_NSA], modes='nnfn')
    win_ref[...] = rope(kva[:, :, W_NSA:], modes='fn')
    qbc_ref[...] = rope(seg(OFF_QBC, W_QBC), modes='fffff')
    moba_ref[...] = rope(seg(OFF_MOBA, W_MOBA), modes='ffhnn')
    dsa_ref[...] = rope(seg(OFF_DSA, W_DSA_PAD), modes='hh')[:, :, :W_DSA]
    qi_ref[...] = rope(seg(OFF_QI, W_QI), modes='ff')
    gw_ref[...] = seg(OFF_GW, W_GW)


def _proj_call(x, sh, sc, w_perm, tabs, s_blk, r_blk):
    ns, rt, _ = x.shape
    cos, s_lo, s_hi = tabs
    tok = lambda w: pl.BlockSpec((s_blk, r_blk, w), lambda i, j: (i, j, 0))
    mod = pl.BlockSpec((s_blk, 1, D_MODEL), lambda i, j: (i, 0, 0))
    tab = pl.BlockSpec((1, r_blk, LANES), lambda i, j: (0, j, 0))
    widths = (W_QA, W_QA, W_NSA, W_WIN, W_QBC, W_MOBA, W_DSA, W_QI, W_GW)
    return pl.pallas_call(
        _proj_kernel,
        grid=(ns // s_blk, rt // r_blk),
        in_specs=[tok(D_MODEL), mod, mod,
                  pl.BlockSpec((D_MODEL, W_PROJ), lambda i, j: (0, 0)),
                  tab, tab, tab],
        out_specs=[tok(w) for w in widths],
        out_shape=[jax.ShapeDtypeStruct((ns, rt, w), F32) for w in widths],
        compiler_params=_params(("parallel", "parallel")),
        name="ln_proj_rope",
    )(x, sh, sc, w_perm, cos, s_lo, s_hi)


def _permute_w_in(w_in_l):
    sizes = (NSA_HEADS * HEAD_DIM, 6 * NSA_KV_HEADS * HEAD_DIM, 3 * NSA_HEADS,
             MOBA_HEADS * HEAD_DIM, MOBA_HEADS * HEAD_DIM, MOBA_HEADS * HEAD_DIM,
             DSA_HEADS * HEAD_DIM, 2 * HEAD_DIM, IDX_HEADS * HEAD_DIM, HEAD_DIM, IDX_HEADS)
    offs = np.concatenate([[0], np.cumsum(sizes)])
    part = lambda k: w_in_l[:, int(offs[k]):int(offs[k + 1])]
    zeros = lambda n: jnp.zeros((D_MODEL, n), w_in_l.dtype)
    cols = [part(0), part(1), part(3), part(6), part(4), part(5),
            part(7), part(9), zeros(W_DSA_PAD - W_DSA), part(8),
            part(2), part(10), zeros(W_GW - 3 * NSA_HEADS - IDX_HEADS)]
    return jnp.concatenate(cols, axis=1).astype(BF16)


def _rope_tables(pos):
    half = ROT_DIM // 2
    inv = ROPE_THETA ** (-jnp.arange(0, ROT_DIM, 2, dtype=F32) / ROT_DIM)
    ang = pos.astype(F32)[:, None] * inv[None, :]
    cos, sin = jnp.cos(ang), jnp.sin(ang)
    t = pos.shape[0]
    ones = jnp.ones((t, HEAD_DIM - ROT_DIM), F32)
    zeros = jnp.zeros((t, HEAD_DIM - ROT_DIM), F32)
    zh = jnp.zeros((t, half), F32)
    c = jnp.concatenate([cos, cos, ones], axis=1)
    s_lo = jnp.concatenate([-sin, zh, zeros], axis=1)
    s_hi = jnp.concatenate([zh, sin, zeros], axis=1)
    rep = lambda a: jnp.concatenate([a, a], axis=1)[None]
    return rep(c), rep(s_lo), rep(s_hi)


def _post_attn_kernel(o_ref, x_ref, g1_ref, sh_ref, sc_ref, wo_ref, lng_ref, lnb_ref, wr_ref, br_ref,
                      x1_ref, h2_ref, gate_ref):
    x = x_ref[...]
    s, r, d = x.shape
    ob = o_ref[...].reshape(s * r, o_ref.shape[2]).astype(BF16)
    att = jnp.dot(ob, wo_ref[...], preferred_element_type=F32).reshape(s, r, d)
    x1 = _ln(ALPHA * x + g1_ref[...] * att) * lng_ref[...] + lnb_ref[...]
    x1_ref[...] = x1
    h2 = _ln(x1) * (1.0 + sc_ref[...]) + sh_ref[...]
    h2b = h2.reshape(s * r, d).astype(BF16)
    h2_ref[...] = h2b.reshape(s, r, d)

    logit = jnp.dot(h2b, wr_ref[...], preferred_element_type=F32) + br_ref[...]
    lane = lax.broadcasted_iota(jnp.int32, logit.shape, 1)
    is_g = (lane >= N_EXPERTS) & (lane < N_EXPERTS + N_GROUPS)
    lg = jnp.where(is_g, logit, NEG)
    mg = jnp.max(lg, axis=-1, keepdims=True)
    g_lane = jnp.min(jnp.where(lg == mg, lane, LANES), axis=-1, keepdims=True)
    g_den = jnp.sum(jnp.where(is_g, jnp.exp(lg - mg), 0.0), axis=-1, keepdims=True)
    g_w = 1.0 / g_den
    g_idx = g_lane - N_EXPERTS
    in_grp = (lane >= g_idx * EXPERTS_PER_GROUP) & (lane < (g_idx + 1) * EXPERTS_PER_GROUP)
    le = jnp.where(in_grp, logit, NEG)
    me = jnp.max(le, axis=-1, keepdims=True)
    ee = jnp.where(in_grp, jnp.exp(le - me), 0.0)
    pe = ee / jnp.sum(ee, axis=-1, keepdims=True)
    p1 = jnp.max(pe, axis=-1, keepdims=True)
    i1 = jnp.min(jnp.where((pe == p1) & in_grp, lane, LANES), axis=-1, keepdims=True)
    rest = jnp.where(in_grp & (lane != i1), pe, -1.0)
    p2 = jnp.max(rest, axis=-1, keepdims=True)
    i2 = jnp.min(jnp.where(rest == p2, lane, LANES), axis=-1, keepdims=True)
    tot = p1 + p2
    gate = jnp.where(lane == i1, g_w * p1 / tot, jnp.where(lane == i2, g_w * p2 / tot, 0.0))
    gate_ref[...] = gate.reshape(s, r, LANES)


def _post_attn_call(o, x, g1, sh2, sc2, wo, lng, lnb, wr, br, s_blk, r_blk):
    ns, rt, _ = x.shape
    wo_rows = o.shape[2]
    tok = lambda w: pl.BlockSpec((s_blk, r_blk, w), lambda i, j: (i, j, 0))
    mod = pl.BlockSpec((s_blk, 1, D_MODEL), lambda i, j: (i, 0, 0))
    full = lambda a: pl.BlockSpec(a.shape, lambda i, j: (0,) * a.ndim)
    return pl.pallas_call(
        _post_attn_kernel,
        grid=(ns // s_blk, rt // r_blk),
        in_specs=[tok(wo_rows), tok(D_MODEL), mod, mod, mod,
                  full(wo), full(lng), full(lnb), full(wr), full(br)],
        out_specs=[tok(D_MODEL), tok(D_MODEL), tok(LANES)],
        out_shape=[jax.ShapeDtypeStruct((ns, rt, D_MODEL), F32),
                   jax.ShapeDtypeStruct((ns, rt, D_MODEL), BF16),
                   jax.ShapeDtypeStruct((ns, rt, LANES), F32)],
        compiler_params=_params(("parallel", "parallel")),
        name="wout_ln_router",
    )(o, x, g1, sh2, sc2, wo, lng, lnb, wr, br)


def _moe_kernel(h_ref, gate_ref, x_ref, g2_ref, wi_ref, wo_ref, lng_ref, lnb_ref, out_ref, acc_ref):
    e = pl.program_id(2)
    s, r, d = x_ref.shape

    @pl.when(e == 0)
    def _():
        acc_ref[...] = jnp.zeros_like(acc_ref)

    h = h_ref[...].reshape(s * r, d)
    ab = jnp.dot(h, wi_ref[0], preferred_element_type=F32)
    a, b = ab[:, :D_EXPERT], ab[:, D_EXPERT:]
    gate = gate_ref[...].reshape(s * r, LANES)
    lane = lax.broadcasted_iota(jnp.int32, gate.shape, 1)
    gcol = jnp.sum(jnp.where(lane == e, gate, 0.0), axis=-1, keepdims=True)
    hid = (a * jax.nn.sigmoid(a) * b * gcol).astype(BF16)
    acc_ref[...] += jnp.dot(hid, wo_ref[0], preferred_element_type=F32)

    @pl.when(e == N_EXPERTS - 1)
    def _():
        f = acc_ref[...].reshape(s, r, d)
        out_ref[...] = _ln(ALPHA * x_ref[...] + g2_ref[...] * f) * lng_ref[...] + lnb_ref[...]


def _moe_call(h2, gate, x1, g2, wi, wo, lng, lnb, s_blk, r_blk):
    ns, rt, _ = x1.shape
    tok = lambda w: pl.BlockSpec((s_blk, r_blk, w), lambda i, j, e: (i, j, 0))
    mod = pl.BlockSpec((s_blk, 1, D_MODEL), lambda i, j, e: (i, 0, 0))
    vec = pl.BlockSpec((1, 1, D_MODEL), lambda i, j, e: (0, 0, 0))
    return pl.pallas_call(
        _moe_kernel,
        grid=(ns // s_blk, rt // r_blk, N_EXPERTS),
        in_specs=[tok(D_MODEL), tok(LANES), tok(D_MODEL), mod,
                  pl.BlockSpec((1, D_MODEL, 2 * D_EXPERT), lambda i, j, e: (e, 0, 0)),
                  pl.BlockSpec((1, D_EXPERT, D_MODEL), lambda i, j, e: (e, 0, 0)),
                  vec, vec],
        out_specs=tok(D_MODEL),
        out_shape=jax.ShapeDtypeStruct((ns, rt, D_MODEL), F32),
        scratch_shapes=[pltpu.VMEM((s_blk * r_blk, D_MODEL), F32)],
        compiler_params=_params(("parallel", "parallel", "arbitrary")),
        name="moe_ln",
    )(h2, gate, x1, g2, wi, wo, lng, lnb)


def _masked_softmax(s, mask):
    s = jnp.where(mask, s.astype(F32), NEG)
    p = jax.nn.softmax(s, axis=-1)
    return jnp.where(mask, p, 0.0)


def _mix_one(q_a, q_ar, g_a, q_b, q_c, q_i, w_i, nsa_kv, moba_kv, dsa_kv, win_kv,
             win_k0, cmp_pe, cmp_w1, cmp_w2):
    T = q_a.shape[0]
    L = nsa_kv.shape[0]
    q0 = L - T
    blk = Q_BLOCK if T % Q_BLOCK == 0 else T
    scale = HEAD_DIM ** -0.5
    take = jax.vmap(lambda rows, ix: rows[ix])
    n_cmp = (L - CMP_LEN) // CMP_STRIDE + 1
    c_start = jnp.arange(n_cmp) * CMP_STRIDE
    c_end = c_start + CMP_LEN - 1
    c_idx = c_start[:, None] + jnp.arange(CMP_LEN)[None, :]

    def compress(rows, j):
        b = rows[c_idx] + cmp_pe[j][None, :, None, :]
        b = b.transpose(0, 2, 1, 3).reshape(n_cmp, NSA_KV_HEADS, CMP_LEN * HEAD_DIM)
        return jax.nn.gelu(b @ cmp_w1[j]) @ cmp_w2[j]

    k_cmp = compress(nsa_kv[:, 0], 0)
    v_cmp = compress(nsa_kv[:, 1], 1)
    n_sel = -(-L // SEL_BLOCK)
    n_top = min(SEL_TOPN, n_sel)
    s_start = jnp.arange(n_sel) * SEL_BLOCK
    overlap = ((c_start[:, None] < s_start[None, :] + SEL_BLOCK)
               & (c_end[:, None] >= s_start[None, :])).astype(F32)
    sel_rows = jnp.pad(nsa_kv[:, 2:4], ((0, n_sel * SEL_BLOCK - L), (0, 0), (0, 0), (0, 0)))
    sel_rows = sel_rows.reshape(n_sel, SEL_BLOCK, 2, NSA_KV_HEADS, HEAD_DIM).transpose(2, 3, 0, 1, 4)
    win_rows = jnp.pad(win_kv, ((WINDOW, 0), (0, 0), (0, 0), (0, 0)))
    nb = -(-L // MOBA_BLOCK)
    m_top = min(MOBA_TOPK, nb)
    moba_rows = jnp.pad(moba_kv, ((0, nb * MOBA_BLOCK - L + blk), (0, 0), (0, 0), (0, 0)))
    moba_blocks = moba_rows[:nb * MOBA_BLOCK].reshape(
        nb, MOBA_BLOCK, 2, MOBA_HEADS, HEAD_DIM).transpose(2, 3, 0, 1, 4)
    k_mean = jnp.mean(moba_blocks[0].astype(F32), axis=2)
    k_d, v_d, k_idx = dsa_kv[:, 0], dsa_kv[:, 1], dsa_kv[:, 2]
    k_top = min(DSA_TOPK, L // 4)
    key_pos = jnp.arange(L)

    def block(i):
        t0 = q0 + i * blk
        t = t0 + jnp.arange(blk)
        cut = lambda a: lax.dynamic_slice_in_dim(a, i * blk, blk, 0)
        qa = cut(q_a).reshape(blk, NSA_KV_HEADS, NSA_GROUP, HEAD_DIM)
        qr = cut(q_ar).reshape(blk, NSA_KV_HEADS, NSA_GROUP, HEAD_DIM)
        p_c = _masked_softmax(jnp.einsum('qkgd,ckd->qkgc', qa, k_cmp) * scale,
                              (c_end[None, :] <= t[:, None])[:, None, None, :])
        o_cmp = jnp.einsum('qkgc,ckd->qkgd', p_c.astype(v_cmp.dtype), v_cmp)
        imp = jnp.einsum('qkgc,cn->qkn', p_c, overlap)
        j = jnp.arange(n_sel)[None, :]
        tb = (t // SEL_BLOCK)[:, None]
        forced = (j == 0) | (j == tb) | (j == tb - 1)
        admissible = j * SEL_BLOCK <= t[:, None]
        imp = jnp.where(forced[:, None, :], BIG, jnp.where(admissible[:, None, :], imp, NEG))
        _, sel = lax.top_k(imp, n_top)
        sel_kv = sel.transpose(1, 0, 2)
        k_g = take(sel_rows[0], sel_kv)
        v_g = take(sel_rows[1], sel_kv)
        s_pos = sel[..., None] * SEL_BLOCK + jnp.arange(SEL_BLOCK)
        m_s = (s_pos <= t[:, None, None, None]).reshape(blk, NSA_KV_HEADS, 1, n_top * SEL_BLOCK)
        s_s = jnp.einsum('qkgd,kqjsd->qkgjs', qr, k_g).reshape(
            blk, NSA_KV_HEADS, NSA_GROUP, n_top * SEL_BLOCK) * scale
        p_s = _masked_softmax(s_s, m_s).reshape(blk, NSA_KV_HEADS, NSA_GROUP, n_top, SEL_BLOCK)
        o_sel = jnp.einsum('qkgjs,kqjsd->qkgd', p_s.astype(v_g.dtype), v_g)
        w = lax.dynamic_slice_in_dim(win_rows, t0 - win_k0, WINDOW + blk, 0)
        w_pos = t0 - WINDOW + jnp.arange(WINDOW + blk)
        dist = t[:, None] - w_pos[None, :]
        m_w = (dist >= 0) & (dist <= WINDOW) & (w_pos[None, :] >= win_k0)
        p_w = _masked_softmax(jnp.einsum('qkgd,skd->qkgs', qr, w[:, 0]) * scale, m_w[:, None, None, :])
        o_win = jnp.einsum('qkgs,skd->qkgd', p_w.astype(w.dtype), w[:, 1])
        g = jax.nn.sigmoid(cut(g_a).astype(F32)).astype(o_cmp.dtype).reshape(
            blk, NSA_KV_HEADS, NSA_GROUP, 3)
        o_nsa = g[..., 0:1] * o_cmp + g[..., 1:2] * o_sel + g[..., 2:3] * o_win
        qm = cut(q_b)
        own = t // MOBA_BLOCK
        s_blk = jnp.einsum('qhd,hnd->qhn', qm.astype(F32), k_mean)
        past_ok = jnp.arange(nb)[None, :] < own[:, None]
        s_blk = jnp.where(past_ok[:, None, :], s_blk, NEG)
        _, bsel = lax.top_k(s_blk, m_top)
        bsel_h = bsel.transpose(1, 0, 2)
        kb = take(moba_blocks[0], bsel_h)
        vb = take(moba_blocks[1], bsel_h)
        m1 = jnp.broadcast_to((bsel < own[:, None, None])[..., None],
                              (blk, MOBA_HEADS, m_top, MOBA_BLOCK)).reshape(blk, MOBA_HEADS, m_top * MOBA_BLOCK)
        s1 = (jnp.einsum('qhd,hqjsd->qhjs', qm, kb).reshape(blk, MOBA_HEADS, m_top * MOBA_BLOCK)
              * scale).astype(F32)
        o_start = (t0 // MOBA_BLOCK) * MOBA_BLOCK
        ob = lax.dynamic_slice_in_dim(moba_rows, o_start, MOBA_BLOCK + blk, 0)
        o_pos = o_start + jnp.arange(MOBA_BLOCK + blk)
        m2 = ((o_pos[None, :] // MOBA_BLOCK) == own[:, None]) & (o_pos[None, :] <= t[:, None])
        m2 = jnp.broadcast_to(m2[:, None, :], (blk, MOBA_HEADS, MOBA_BLOCK + blk))
        s2 = (jnp.einsum('qhd,shd->qhs', qm, ob[:, 0]) * scale).astype(F32)
        p_m = _masked_softmax(jnp.concatenate([s1, s2], axis=-1),
                              jnp.concatenate([m1, m2], axis=-1)).astype(vb.dtype)
        o_moba = (jnp.einsum('qhjs,hqjsd->qhd',
                             p_m[..., :m_top * MOBA_BLOCK].reshape(blk, MOBA_HEADS, m_top, MOBA_BLOCK), vb)
                  + jnp.einsum('qhs,shd->qhd', p_m[..., m_top * MOBA_BLOCK:], ob[:, 1]))
        qi = cut(q_i).astype(F32)
        relu_s = jax.nn.relu(jnp.einsum('qhd,sd->qhs', qi, k_idx.astype(F32)))
        score = jnp.einsum('qh,qhs->qs', cut(w_i).astype(F32), relu_s)
        score = jnp.where(key_pos[None, :] <= t[:, None], score, NEG)
        _, dsel = lax.top_k(score, k_top)
        kd = k_d[dsel]
        vd = v_d[dsel]
        p_d = _masked_softmax(jnp.einsum('qhd,qkd->qhk', cut(q_c), kd) * scale,
                              (dsel <= t[:, None])[:, None, :])
        o_dsa = jnp.einsum('qhk,qkd->qhd', p_d.astype(vd.dtype), vd)
        return jnp.concatenate([o_nsa.reshape(blk, -1), o_moba.reshape(blk, -1),
                                o_dsa.reshape(blk, -1)], axis=-1)

    out = lax.map(block, jnp.arange(T // blk))
    return out.reshape(T, D_MIX)


def _mixers_jax(pr, cmp_pe, cmp_w1, cmp_w2, past, pos0):
    qa, qar, nsa_new, win_new, qbc, moba_new, dsa_new, qi, gw = pr
    B, T, _ = qa.shape
    q_a = qa.reshape(B, T, NSA_HEADS, HEAD_DIM)
    q_ar = qar.reshape(B, T, NSA_HEADS, HEAD_DIM)
    g_a = gw[:, :, :3 * NSA_HEADS].reshape(B, T, NSA_HEADS, 3)
    w_i = gw[:, :, 3 * NSA_HEADS:3 * NSA_HEADS + IDX_HEADS]
    q_b = qbc[:, :, :MOBA_HEADS * HEAD_DIM].reshape(B, T, MOBA_HEADS, HEAD_DIM)
    q_c = qbc[:, :, MOBA_HEADS * HEAD_DIM:].reshape(B, T, DSA_HEADS, HEAD_DIM)
    q_i = qi.reshape(B, T, IDX_HEADS, HEAD_DIM)
    nsa_n = nsa_new.reshape(B, T, 4, NSA_KV_HEADS, HEAD_DIM)
    win_n = win_new.reshape(B, T, 2, NSA_KV_HEADS, HEAD_DIM)
    moba_n = moba_new.reshape(B, T, 2, MOBA_HEADS, HEAD_DIM)
    dsa_n = dsa_new.reshape(B, T, 3, HEAD_DIM)
    qs = (q_a, q_ar, g_a, q_b, q_c, q_i, w_i)
    if past is None:
        o = lax.map(lambda a: _mix_one(*a, 0, cmp_pe, cmp_w1, cmp_w2),
                    qs + (nsa_n, moba_n, dsa_n, win_n))
        win_out = win_n[:, -min(WINDOW, T):]
    else:
        c_nsa, c_moba, c_dsa, w_buf, table = past
        win_k0 = pos0 - w_buf.shape[1]

        def one(a):
            (qa_, qar_, ga_, qb_, qc_, qi_, wi_, nsa_1, moba_1, dsa_1, win_1, buf, row) = a

            def gather_past(pool):
                rows = pool[row]
                return rows.reshape((-1,) + rows.shape[2:])

            return _mix_one(qa_, qar_, ga_, qb_, qc_, qi_, wi_,
                            jnp.concatenate([gather_past(c_nsa), nsa_1], axis=0),
                            jnp.concatenate([gather_past(c_moba), moba_1], axis=0),
                            jnp.concatenate([gather_past(c_dsa), dsa_1], axis=0),
                            jnp.concatenate([buf, win_1], axis=0),
                            win_k0, cmp_pe, cmp_w1, cmp_w2)

        o = lax.map(one, qs + (nsa_n, moba_n, dsa_n, win_n, w_buf, table))
        win_out = jnp.concatenate([w_buf, win_n], axis=1)[:, -w_buf.shape[1]:]
    o = jnp.pad(o.reshape(B, T, D_MIX), ((0, 0), (0, 0), (0, D_MODEL - D_MIX)))
    return o, nsa_n, moba_n, dsa_n, win_out


def _layer(x, mod, wts, tabs, cmp_w, past, pos0, blk_a, blk_m):
    sh1, sc1, g1, sh2, sc2, g2 = mod
    w_perm, wo, ln1g, ln1b, ln2g, ln2b, wr, br, wei, weo = wts
    pr = _proj_call(x, sh1, sc1, w_perm, tabs, *blk_a)
    o, nsa_r, moba_r, dsa_r, win_r = _mixers_jax(pr, *cmp_w, past, pos0)
    x1, h2, gate = _post_attn_call(o, x, g1, sh2, sc2, wo, ln1g, ln1b, wr, br, *blk_a)
    y = _moe_call(h2, gate, x1, g2, wei, weo, ln2g, ln2b, *blk_m)
    return y, nsa_r, moba_r, dsa_r, win_r


def _layer_weights(p, l):
    wo_pad = jnp.concatenate([p['w_out'][l], jnp.zeros((D_MODEL - D_MIX, D_MODEL), F32)], axis=0).astype(BF16)
    n_pad = LANES - N_EXPERTS - N_GROUPS
    wr = jnp.concatenate([p['w_router_expert'][l], p['w_router_group'][l],
                          jnp.zeros((D_MODEL, n_pad), F32)], axis=1).astype(BF16)
    br = jnp.concatenate([p['b_router_expert'][l], p['b_router_group'][l], jnp.zeros((n_pad,), F32)])[None, :]
    vec = lambda a: a[l].reshape(1, 1, D_MODEL)
    return (_permute_w_in(p['w_in'][l]), wo_pad, vec(p['ln1_g']), vec(p['ln1_b']), vec(p['ln2_g']),
            vec(p['ln2_b']), wr, br, p['w_expert_in'][l].astype(BF16), p['w_expert_out'][l].astype(BF16))


def kernel(x_prompt, x_sample, c_prompt, c_sample, cache_nsa, cache_moba, cache_dsa, state_win, page_table, w_ada, b_ada, w_in, cmp_pe, cmp_w1, cmp_w2, w_out, ln1_g, ln1_b, ln2_g, ln2_b, w_router_group, b_router_group, w_router_expert, b_router_expert, w_expert_in, w_expert_out):
    p = dict(w_in=w_in, w_out=w_out, ln1_g=ln1_g, ln1_b=ln1_b, ln2_g=ln2_g, ln2_b=ln2_b,
             w_router_group=w_router_group, b_router_group=b_router_group,
             w_router_expert=w_router_expert, b_router_expert=b_router_expert,
             w_expert_in=w_expert_in, w_expert_out=w_expert_out)
    nb_p, t_p, _ = x_prompt.shape
    nb_s, t_s, _ = x_sample.shape
    past_len = page_table.shape[1] * cache_nsa.shape[2]

    n_c = nb_p + nb_s
    n_c_pad = -(-n_c // 8) * 8
    c_all = jnp.concatenate([c_prompt, c_sample, jnp.zeros((n_c_pad - n_c, D_MODEL), F32)], axis=0)
    mod_all = _ada_call(c_all, w_ada, b_ada)

    tabs_p = _rope_tables(jnp.arange(t_p))
    tabs_s = _rope_tables(past_len + jnp.arange(t_s))

    y_p, y_s = x_prompt, x_sample
    st_p, st_s = [], []
    for l in range(DEPTH):
        mod_l = mod_all[l]
        split = lambda m: tuple(m[:, None, k * D_MODEL:(k + 1) * D_MODEL] for k in range(6))
        mod_p = split(mod_l[:nb_p])
        mod_s = split(mod_l[nb_p:n_c])
        wts = _layer_weights(p, l)
        cmp_w = (cmp_pe[l], cmp_w1[l], cmp_w2[l])
        y_p, *sp = _layer(y_p, mod_p, wts, tabs_p, cmp_w, None, 0, (1, 512), (1, 1024))
        past = (cache_nsa[l], cache_moba[l], cache_dsa[l], state_win[l], page_table)
        y_s, *ss = _layer(y_s, mod_s, wts, tabs_s, cmp_w, past, past_len, (32, t_s), (nb_s, t_s))
        st_p.append(sp)
        st_s.append(ss)
    stack = lambda st, k: jnp.stack([s[k] for s in st])
    return (y_p, y_s, stack(st_p, 0), stack(st_s, 0), stack(st_p, 1), stack(st_s, 1),
            stack(st_p, 2), stack(st_s, 2), stack(st_p, 3), stack(st_s, 3))
```

```python
import functools
import math

import numpy as np
import jax
import jax.numpy as jnp
from jax import lax
from jax.experimental import pallas as pl
from jax.experimental.pallas import tpu as pltpu

D_MODEL = 1024
DEPTH = 2
HEAD_DIM = 64
NSA_HEADS = 6
NSA_KV_HEADS = 2
NSA_GROUP = NSA_HEADS // NSA_KV_HEADS
MOBA_HEADS = 5
DSA_HEADS = 5
D_MIX = (NSA_HEADS + MOBA_HEADS + DSA_HEADS) * HEAD_DIM
CMP_LEN = 32
CMP_STRIDE = 16
CMP_HIDDEN = 256
SEL_BLOCK = 64
SEL_TOPN = 16
WINDOW = 512
MOBA_BLOCK = 256
MOBA_TOPK = 3
IDX_HEADS = 4
DSA_TOPK = 256
ROPE_THETA = 500000.0
ROT_DIM = HEAD_DIM // 4
N_GROUPS = 4
EXPERTS_PER_GROUP = 8
N_EXPERTS = N_GROUPS * EXPERTS_PER_GROUP
D_EXPERT = 256
ALPHA = (2 * DEPTH) ** 0.25
LN_EPS = 1e-5
NEG = -1e30
BIG = 1e30
BELOW_NEG = -3e38
INT_MIN = -2 ** 31

LANES = 128
F32 = jnp.float32
BF16 = jnp.bfloat16
MXU = BF16
Q_SCALE = HEAD_DIM ** -0.5

W_QA = NSA_HEADS * LANES
W_KVA = 6 * NSA_KV_HEADS * HEAD_DIM
W_NSA = 4 * NSA_KV_HEADS * HEAD_DIM
W_WIN = 2 * NSA_KV_HEADS * HEAD_DIM
W_QB = MOBA_HEADS * LANES
W_QC = DSA_HEADS * LANES
W_MOBA = 2 * MOBA_HEADS * HEAD_DIM
W_DSA = 3 * HEAD_DIM
W_DSA_PAD = 2 * LANES
W_QI = IDX_HEADS * LANES
W_GW = LANES
OFF_QA = 0
OFF_KVA = OFF_QA + W_QA
OFF_QB = OFF_KVA + W_KVA
OFF_QC = OFF_QB + W_QB
OFF_MOBA = OFF_QC + W_QC
OFF_DSA = OFF_MOBA + W_MOBA
OFF_QI = OFF_DSA + W_DSA_PAD
OFF_GW = OFF_QI + W_QI
W_PROJ = OFF_GW + W_GW
GW_WI = 3 * NSA_HEADS

W_ONSA = NSA_HEADS * LANES
W_OMOBA = MOBA_HEADS * LANES
W_ODSA = DSA_HEADS * LANES
W_OALL = W_ONSA + W_OMOBA + W_ODSA

VMEM_LIMIT = 56 * 1024 * 1024


def _params(sem):
    return pltpu.CompilerParams(dimension_semantics=sem, vmem_limit_bytes=VMEM_LIMIT)


def _ln(x):
    mu = jnp.mean(x, axis=-1, keepdims=True)
    xc = x - mu
    var = jnp.mean(xc * xc, axis=-1, keepdims=True)
    return xc * lax.rsqrt(var + LN_EPS)


def _dot(a, b):
    return jnp.dot(a, b, preferred_element_type=F32)


def _dot_t(a, b):
    return lax.dot_general(a, b, (((1,), (1,)), ((), ())), preferred_element_type=F32)


def _ada_kernel(c_ref, w_ref, b_ref, o_ref):
    c = c_ref[...]
    a = (c * jax.nn.sigmoid(c)).astype(MXU)
    o_ref[0] = _dot(a, w_ref[0].astype(MXU)) + b_ref[0]


def _ada_call(c_all, w_ada, b_ada):
    n = c_all.shape[0]
    tn = 1536
    return pl.pallas_call(
        _ada_kernel,
        grid=(DEPTH, 6 * D_MODEL // tn),
        in_specs=[
            pl.BlockSpec((n, D_MODEL), lambda l, j: (0, 0)),
            pl.BlockSpec((1, D_MODEL, tn), lambda l, j: (l, 0, j)),
            pl.BlockSpec((1, 1, tn), lambda l, j: (l, 0, j)),
        ],
        out_specs=pl.BlockSpec((1, n, tn), lambda l, j: (l, 0, j)),
        out_shape=jax.ShapeDtypeStruct((DEPTH, n, 6 * D_MODEL), F32),
        compiler_params=_params(("arbitrary", "arbitrary")),
        name="ada_mod",
    )(c_all, w_ada, b_ada.reshape(DEPTH, 1, 6 * D_MODEL))


def _rope_tile(t, cos, s_lo, s_hi, first_half_only):
    if first_half_only:
        lane = lax.broadcasted_iota(jnp.int32, cos.shape, 2)
        keep = lane < HEAD_DIM
        cos = jnp.where(keep, cos, 1.0)
        s_lo = jnp.where(keep, s_lo, 0.0)
        s_hi = jnp.where(keep, s_hi, 0.0)
    half = ROT_DIM // 2
    up = pltpu.roll(t, LANES - half, axis=2)
    dn = pltpu.roll(t, half, axis=2)
    return t * cos + up * s_lo + dn * s_hi


def _rope_slab(p, cos, s_lo, s_hi, modes):
    tiles = []
    for j, m in enumerate(modes):
        t = p[:, :, j * LANES:(j + 1) * LANES]
        if m != 'n':
            t = _rope_tile(t, cos, s_lo, s_hi, m == 'h')
        tiles.append(t)
    return tiles[0] if len(tiles) == 1 else jnp.concatenate(tiles, axis=2)


PROJ_F32_WIDTHS = (W_NSA, W_WIN, W_MOBA, W_DSA, W_GW)
PROJ_MXU_WIDTHS = (W_QA, W_QA, W_QB, W_QC, W_QI, LANES, LANES, 2 * LANES, W_WIN, W_MOBA, W_DSA_PAD)


def _proj_kernel(x_ref, sh_ref, sc_ref, w_ref, cos_ref, slo_ref, shi_ref,
                 nsa_ref, win_ref, moba_ref, dsa_ref, gw_ref,
                 qa_ref, qar_ref, qb_ref, qc_ref, qi_ref,
                 kc_ref, vc_ref, kvs_ref, winb_ref, mobab_ref, dsab_ref):
    x = x_ref[...]
    s, r, d = x.shape
    h = _ln(x) * (1.0 + sc_ref[...]) + sh_ref[...]
    hb = h.reshape(s * r, d).astype(MXU)
    rope = functools.partial(_rope_slab, cos=cos_ref[...], s_lo=slo_ref[...], s_hi=shi_ref[...])

    def seg(off, width):
        return _dot(hb, w_ref[:, off:off + width]).reshape(s, r, width)

    qa = seg(OFF_QA, W_QA)
    qa_ref[...] = qa.astype(MXU)
    qar_ref[...] = rope(qa, modes='f' * NSA_HEADS).astype(MXU)
    kva = seg(OFF_KVA, W_KVA)
    nsa = rope(kva[:, :, :W_NSA], modes='nnfn')
    nsa_ref[...] = nsa
    kc_ref[...] = nsa[:, :, 0:LANES].astype(MXU)
    vc_ref[...] = nsa[:, :, LANES:2 * LANES].astype(MXU)
    kvs_ref[...] = nsa[:, :, 2 * LANES:].astype(MXU)
    win = rope(kva[:, :, W_NSA:], modes='fn')
    win_ref[...] = win
    winb_ref[...] = win.astype(MXU)
    qb_ref[...] = rope(seg(OFF_QB, W_QB), modes='f' * MOBA_HEADS).astype(MXU)
    qc_ref[...] = rope(seg(OFF_QC, W_QC), modes='f' * DSA_HEADS).astype(MXU)
    moba = rope(seg(OFF_MOBA, W_MOBA), modes='ffhnn')
    moba_ref[...] = moba
    mobab_ref[...] = moba.astype(MXU)
    dsa = rope(seg(OFF_DSA, W_DSA_PAD), modes='hh')
    dsa_ref[...] = dsa[:, :, :W_DSA]
    dsab_ref[...] = dsa.astype(MXU)
    qi_ref[...] = rope(seg(OFF_QI, W_QI), modes='f' * IDX_HEADS).astype(MXU)
    gw_ref[...] = seg(OFF_GW, W_GW)


def _proj_call(x, sh, sc, w_perm, tabs, s_blk, r_blk):
    ns, rt, _ = x.shape
    cos, s_lo, s_hi = tabs
    tok = lambda w: pl.BlockSpec((s_blk, r_blk, w), lambda i, j: (i, j, 0))
    mod = pl.BlockSpec((s_blk, 1, D_MODEL), lambda i, j: (i, 0, 0))
    tab = pl.BlockSpec((1, r_blk, LANES), lambda i, j: (0, j, 0))
    widths = PROJ_F32_WIDTHS + PROJ_MXU_WIDTHS
    dtypes = (F32,) * len(PROJ_F32_WIDTHS) + (MXU,) * len(PROJ_MXU_WIDTHS)
    return pl.pallas_call(
        _proj_kernel,
        grid=(ns // s_blk, rt // r_blk),
        in_specs=[tok(D_MODEL), mod, mod,
                  pl.BlockSpec((D_MODEL, W_PROJ), lambda i, j: (0, 0)),
                  tab, tab, tab],
        out_specs=[tok(w) for w in widths],
        out_shape=[jax.ShapeDtypeStruct((ns, rt, w), dt) for w, dt in zip(widths, dtypes)],
        compiler_params=_params(("parallel", "parallel")),
        name="ln_proj_rope",
    )(x, sh, sc, w_perm, cos, s_lo, s_hi)


def _pad_heads(w, halves, scale):
    z = jnp.zeros((w.shape[0], HEAD_DIM), w.dtype)
    cols = []
    for h, half in enumerate(halves):
        wh = w[:, h * HEAD_DIM:(h + 1) * HEAD_DIM] * scale
        cols += [wh, z] if half == 0 else [z, wh]
    return jnp.concatenate(cols, axis=1)


def _permute_w_in(w_in_l):
    sizes = (NSA_HEADS * HEAD_DIM, 6 * NSA_KV_HEADS * HEAD_DIM, 3 * NSA_HEADS,
             MOBA_HEADS * HEAD_DIM, MOBA_HEADS * HEAD_DIM, MOBA_HEADS * HEAD_DIM,
             DSA_HEADS * HEAD_DIM, 2 * HEAD_DIM, IDX_HEADS * HEAD_DIM, HEAD_DIM, IDX_HEADS)
    offs = np.concatenate([[0], np.cumsum(sizes)])
    part = lambda k: w_in_l[:, int(offs[k]):int(offs[k + 1])]
    zeros = lambda n: jnp.zeros((D_MODEL, n), w_in_l.dtype)
    cols = [_pad_heads(part(0), [h // NSA_GROUP for h in range(NSA_HEADS)], Q_SCALE),
            part(1),
            _pad_heads(part(3), [h % 2 for h in range(MOBA_HEADS)], Q_SCALE),
            _pad_heads(part(6), [0] * DSA_HEADS, Q_SCALE),
            part(4), part(5),
            part(7), part(9), zeros(W_DSA_PAD - W_DSA),
            _pad_heads(part(8), [0] * IDX_HEADS, 1.0),
            part(2), part(10), zeros(W_GW - 3 * NSA_HEADS - IDX_HEADS)]
    return jnp.concatenate(cols, axis=1).astype(MXU)


def _pad_w_out(w_out_l):
    z = jnp.zeros((HEAD_DIM, D_MODEL), w_out_l.dtype)
    rows = []
    head = lambda h: w_out_l[h * HEAD_DIM:(h + 1) * HEAD_DIM]
    for h in range(NSA_HEADS):
        rows += [head(h), z] if h // NSA_GROUP == 0 else [z, head(h)]
    for h in range(MOBA_HEADS):
        rows += [head(NSA_HEADS + h), z] if (MOBA_HEADS + h) % 2 == 0 else [z, head(NSA_HEADS + h)]
    for h in range(DSA_HEADS):
        rows += [z, head(NSA_HEADS + MOBA_HEADS + h)]
    return jnp.concatenate(rows, axis=0).astype(MXU)


def _rope_tables(pos):
    half = ROT_DIM // 2
    inv = ROPE_THETA ** (-jnp.arange(0, ROT_DIM, 2, dtype=F32) / ROT_DIM)
    ang = pos.astype(F32)[:, None] * inv[None, :]
    cos, sin = jnp.cos(ang), jnp.sin(ang)
    t = pos.shape[0]
    ones = jnp.ones((t, HEAD_DIM - ROT_DIM), F32)
    zeros = jnp.zeros((t, HEAD_DIM - ROT_DIM), F32)
    zh = jnp.zeros((t, half), F32)
    c = jnp.concatenate([cos, cos, ones], axis=1)
    s_lo = jnp.concatenate([-sin, zh, zeros], axis=1)
    s_hi = jnp.concatenate([zh, sin, zeros], axis=1)
    rep = lambda a: jnp.concatenate([a, a], axis=1)[None]
    return rep(c), rep(s_lo), rep(s_hi)


def _assemble_kernel(tbl_ref, nsa_ref, moba_ref, dsa_ref, nsa_new_ref, moba_new_ref, dsa_new_ref,
                     kc_ref, vc_ref, kvs_ref, mob_ref, dsao_ref, *, n_pages):
    p = pl.program_id(1)
    rows = kc_ref.shape[1]

    def emit(nsa, moba, dsa):
        kc_ref[0] = nsa[:, 0:LANES].astype(MXU)
        vc_ref[0] = nsa[:, LANES:2 * LANES].astype(MXU)
        kvs_ref[0] = nsa[:, 2 * LANES:].astype(MXU)
        mob_ref[0] = moba.astype(MXU)
        dsao_ref[0] = jnp.concatenate([dsa, jnp.zeros((rows, W_DSA_PAD - W_DSA), F32)], axis=1).astype(MXU)

    @pl.when(p < n_pages)
    def _():
        emit(nsa_ref[0, 0], moba_ref[0, 0], dsa_ref[0, 0])

    @pl.when(p == n_pages)
    def _():
        t_new = nsa_new_ref.shape[1]
        tail = lambda new: jnp.concatenate([new, jnp.zeros((rows - t_new, new.shape[1]), F32)], axis=0)
        emit(tail(nsa_new_ref[0]), tail(moba_new_ref[0]), tail(dsa_new_ref[0]))

    @pl.when(p > n_pages)
    def _():
        emit(jnp.zeros((rows, W_NSA), F32), jnp.zeros((rows, W_MOBA), F32), jnp.zeros((rows, W_DSA), F32))


def _assemble_call(table, cache_nsa, cache_moba, cache_dsa, l, nsa_new, moba_new, dsa_new, lpad):
    ns, n_pages = table.shape
    page = cache_nsa.shape[2]
    n_blk = lpad // page
    pool = lambda w: pl.BlockSpec(
        (1, 1, page, w), lambda b, p, tbl: (l, tbl[b, jnp.minimum(p, n_pages - 1)], 0, 0))
    new = lambda w: pl.BlockSpec((1, nsa_new.shape[1], w), lambda b, p, tbl: (b, 0, 0))
    out = lambda w: pl.BlockSpec((1, page, w), lambda b, p, tbl: (b, p, 0))
    widths = (LANES, LANES, 2 * LANES, W_MOBA, W_DSA_PAD)
    return pl.pallas_call(
        functools.partial(_assemble_kernel, n_pages=n_pages),
        grid_spec=pltpu.PrefetchScalarGridSpec(
            num_scalar_prefetch=1,
            grid=(ns, n_blk),
            in_specs=[pool(W_NSA), pool(W_MOBA), pool(W_DSA), new(W_NSA), new(W_MOBA), new(W_DSA)],
            out_specs=[out(w) for w in widths],
        ),
        out_shape=[jax.ShapeDtypeStruct((ns, lpad, w), MXU) for w in widths],
        compiler_params=_params(("parallel", "arbitrary")),
        name="assemble_pages",
    )(table,
      cache_nsa.reshape(cache_nsa.shape[:3] + (W_NSA,)),
      cache_moba.reshape(cache_moba.shape[:3] + (W_MOBA,)),
      cache_dsa.reshape(cache_dsa.shape[:3] + (W_DSA,)),
      nsa_new, moba_new, dsa_new)


def _win_kernel(buf_ref, new_ref, winb_ref, wout_ref):
    buf = buf_ref[0, 0]
    new = new_ref[0]
    pad = winb_ref.shape[1] - buf.shape[0] - new.shape[0]
    winb_ref[0] = jnp.concatenate([buf, new, jnp.zeros((pad, buf.shape[1]), F32)], axis=0).astype(MXU)
    wout_ref[0] = jnp.concatenate([buf[new.shape[0]:], new], axis=0)


def _win_call(state_win, l, win_new, rows_pad):
    ns, w_buf = state_win.shape[1], state_win.shape[2]
    t_new = win_new.shape[1]
    return pl.pallas_call(
        _win_kernel,
        grid=(ns,),
        in_specs=[pl.BlockSpec((1, 1, w_buf, W_WIN), lambda b: (l, b, 0, 0)),
                  pl.BlockSpec((1, t_new, W_WIN), lambda b: (b, 0, 0))],
        out_specs=[pl.BlockSpec((1, rows_pad, W_WIN), lambda b: (b, 0, 0)),
                   pl.BlockSpec((1, w_buf, W_WIN), lambda b: (b, 0, 0))],
        out_shape=[jax.ShapeDtypeStruct((ns, rows_pad, W_WIN), MXU),
                   jax.ShapeDtypeStruct((ns, w_buf, W_WIN), F32)],
        compiler_params=_params(("parallel",)),
        name="window_buffer",
    )(state_win.reshape(state_win.shape[:3] + (W_WIN,)), win_new)


def _cmp_kernel(kx_ref, vx_ref, pe_ref, w1_ref, w2_ref, kc_ref, vc_ref):
    nch = kx_ref.shape[1]
    nout = kc_ref.shape[1]
    for j, (x_ref, o_ref) in enumerate(((kx_ref, kc_ref), (vx_ref, vc_ref))):
        x = x_ref[0].astype(F32)
        a = _dot((x + pe_ref[j, 0]).astype(MXU), w1_ref[j, 0])
        b = _dot((x + pe_ref[j, 1]).astype(MXU), w1_ref[j, 1])
        hid = a + pltpu.roll(b, nch - 1, axis=0)
        out = _dot(jax.nn.gelu(hid).astype(MXU), w2_ref[j])
        o_ref[0, 0:nch] = out.astype(o_ref.dtype)
        if nout > nch:
            o_ref[0, nch:nout] = jnp.zeros((nout - nch, LANES), o_ref.dtype)


def _cmp_call(kx, vx, pe_e, w1_e, w2_e, nout):
    ns, nch, wx = kx.shape
    full = lambda a: pl.BlockSpec(a.shape, lambda b: (0,) * a.ndim)
    return pl.pallas_call(
        _cmp_kernel,
        grid=(ns,),
        in_specs=[pl.BlockSpec((1, nch, wx), lambda b: (b, 0, 0))] * 2 + [full(pe_e), full(w1_e), full(w2_e)],
        out_specs=[pl.BlockSpec((1, nout, LANES), lambda b: (b, 0, 0))] * 2,
        out_shape=[jax.ShapeDtypeStruct((ns, nout, LANES), MXU)] * 2,
        compiler_params=_params(("parallel",)),
        name="nsa_compress",
    )(kx, vx, pe_e, w1_e, w2_e)


def _cmp_weights(cmp_pe_l, cmp_w1_l, cmp_w2_l):
    half_rows = CMP_LEN // 2
    kv = NSA_KV_HEADS
    w1 = cmp_w1_l.reshape(2, 2, half_rows, HEAD_DIM, CMP_HIDDEN)
    w1_e = jnp.zeros((2, 2, half_rows, kv, HEAD_DIM, kv, CMP_HIDDEN), F32)
    for k in range(kv):
        w1_e = w1_e.at[:, :, :, k, :, k, :].set(w1)
    w1_e = w1_e.reshape(2, 2, half_rows * kv * HEAD_DIM, kv * CMP_HIDDEN).astype(MXU)
    w2_e = jnp.zeros((2, kv, CMP_HIDDEN, kv, HEAD_DIM), F32)
    for k in range(kv):
        w2_e = w2_e.at[:, k, :, k, :].set(cmp_w2_l)
    w2_e = w2_e.reshape(2, kv * CMP_HIDDEN, kv * HEAD_DIM).astype(MXU)
    pe = cmp_pe_l.reshape(2, 2, half_rows, 1, HEAD_DIM)
    pe_e = jnp.broadcast_to(pe, (2, 2, half_rows, kv, HEAD_DIM)).reshape(2, 2, 1, half_rows * kv * HEAD_DIM)
    return pe_e, w1_e, w2_e


def _online_step(s, mask, vt, m_scr, l_scr, acc_scr, rows):
    g, q, c = s.shape
    sm = jnp.where(mask, s, NEG)
    m_prev = m_scr[rows, :]
    m_new = jnp.maximum(m_prev, jnp.max(sm, axis=-1, keepdims=True).reshape(g * q, 1))
    alpha = jnp.exp(m_prev - m_new)
    p = jnp.where(mask, jnp.exp(sm - m_new.reshape(g, q, 1)), 0.0).reshape(g * q, c)
    l_scr[rows, :] = alpha * l_scr[rows, :] + jnp.sum(p, axis=-1, keepdims=True)
    acc_scr[rows, :] = alpha * acc_scr[rows, :] + _dot(p.astype(MXU), vt)
    m_scr[rows, :] = m_new


def _reset_state(m_scr, l_scr, acc_scr):
    m_scr[...] = jnp.full(m_scr.shape, NEG, F32)
    l_scr[...] = jnp.zeros(l_scr.shape, F32)
    acc_scr[...] = jnp.zeros(acc_scr.shape, F32)


def _masked_softmax(s, mask):
    sm = jnp.where(mask, s, NEG)
    m = jnp.max(sm, axis=-1, keepdims=True)
    e = jnp.where(mask, jnp.exp(sm - m), 0.0)
    den = jnp.sum(e, axis=-1, keepdims=True)
    return e * jnp.where(den > 0.0, 1.0 / den, 0.0)


def _stack_heads(ref, tiles):
    parts = [ref[0, :, t * LANES:(t + 1) * LANES].astype(F32) for t in tiles]
    return jnp.concatenate(parts, axis=0).astype(MXU)


def _nsa_kernel(qa_ref, qar_ref, gw_ref, kc_ref, vc_ref, kvs_ref, win_ref, ovl_ref, exp_ref, o_ref,
                m_scr, l_scr, acc_scr, *, qb, q0, ch, win_pos0, slab):
    t0 = q0 + pl.program_id(1) * qb
    tq = t0 + lax.broadcasted_iota(jnp.int32, (qb, 1), 0)
    ncp = kc_ref.shape[1]
    lpad = kvs_ref.shape[1]
    win_rows = win_ref.shape[1]
    g = NSA_GROUP
    gate = jax.nn.sigmoid(gw_ref[0])
    lane = lax.broadcasted_iota(jnp.int32, (1, LANES), 1)
    c_end = lax.broadcasted_iota(jnp.int32, (1, ncp), 1) * CMP_STRIDE + (CMP_LEN - 1)
    mask_c = (c_end <= tq)[None]
    tb = tq // SEL_BLOCK
    forced = (lane == 0) | (lane == tb) | (lane == tb - 1)
    admissible = lane * SEL_BLOCK <= tq
    n_chunks = jnp.minimum((t0 + qb + ch - 1) // ch, lpad // ch)
    if win_rows > slab:
        w_start = pl.multiple_of(jnp.clip(t0 - WINDOW - win_pos0, 0, win_rows - slab), 8)
    else:
        w_start = 0
    w_pos = win_pos0 + w_start + lax.broadcasted_iota(jnp.int32, (1, slab), 1)
    dist = tq - w_pos
    mask_w = ((dist >= 0) & (dist <= WINDOW))[None]

    for k in range(NSA_KV_HEADS):
        tiles = [k * g + j for j in range(g)]
        qa = _stack_heads(qa_ref, tiles)
        qr = _stack_heads(qar_ref, tiles)

        p_c = _masked_softmax(_dot_t(qa, kc_ref[0]).reshape(g, qb, ncp), mask_c)
        o_cmp = _dot(p_c.reshape(g * qb, ncp).astype(MXU), vc_ref[0])
        p_sum = p_c[0]
        for j in range(1, g):
            p_sum = p_sum + p_c[j]
        p_hi = p_sum.astype(MXU)
        p_lo = (p_sum - p_hi.astype(F32)).astype(MXU)
        imp = _dot(p_hi, ovl_ref[...]) + _dot(p_lo, ovl_ref[...])
        val = jnp.where(forced, BIG, jnp.where(admissible, imp, NEG))
        sel = jnp.zeros((qb, LANES), F32)
        for _ in range(SEL_TOPN):
            top = jnp.max(val, axis=-1, keepdims=True)
            idx = jnp.min(jnp.where(val == top, lane, LANES), axis=-1, keepdims=True)
            pick = lane == idx
            sel = jnp.where(pick, 1.0, sel)
            val = jnp.where(pick, BELOW_NEG, val)
        sel_b = sel.astype(MXU)

        _reset_state(m_scr, l_scr, acc_scr)

        def sel_body(c, carry):
            off = pl.multiple_of(c * ch, ch)
            kt = kvs_ref[0, pl.ds(off, ch), 0:LANES]
            vt = kvs_ref[0, pl.ds(off, ch), LANES:2 * LANES]
            s = _dot_t(qr, kt).reshape(g, qb, ch)
            k_pos = off + lax.broadcasted_iota(jnp.int32, (1, ch), 1)
            picked = _dot(sel_b, exp_ref[:, pl.ds(off, ch)]) > 0.5
            mask = (picked & (k_pos <= tq))[None]
            _online_step(s, mask, vt, m_scr, l_scr, acc_scr, slice(None))
            return carry

        lax.fori_loop(0, n_chunks, sel_body, 0)
        o_sel = acc_scr[...] / l_scr[...]

        kt = win_ref[0, pl.ds(w_start, slab), 0:LANES]
        vt = win_ref[0, pl.ds(w_start, slab), LANES:2 * LANES]
        p_w = _masked_softmax(_dot_t(qr, kt).reshape(g, qb, slab), mask_w)
        o_win = _dot(p_w.reshape(g * qb, slab).astype(MXU), vt)

        for j in range(g):
            h = k * g + j
            r = slice(j * qb, (j + 1) * qb)
            o = (gate[:, 3 * h:3 * h + 1] * o_cmp[r] + gate[:, 3 * h + 1:3 * h + 2] * o_sel[r]
                 + gate[:, 3 * h + 2:3 * h + 3] * o_win[r])
            o_ref[0, :, h * LANES:(h + 1) * LANES] = o.astype(o_ref.dtype)


def _nsa_call(qa, qar, gw, kc, vc, kvs, winb, ovl, expand, *, qb, q0, ch, win_pos0):
    ns, t, _ = qa.shape
    lpad = kvs.shape[1]
    slab = min(-(-(WINDOW + qb) // LANES) * LANES, winb.shape[1])
    tok = lambda w: pl.BlockSpec((1, qb, w), lambda b, i: (b, i, 0))
    seq = lambda a: pl.BlockSpec((1,) + a.shape[1:], lambda b, i: (b, 0, 0))
    full = lambda a: pl.BlockSpec(a.shape, lambda b, i: (0,) * a.ndim)
    rows = NSA_GROUP * qb
    return pl.pallas_call(
        functools.partial(_nsa_kernel, qb=qb, q0=q0, ch=ch, win_pos0=win_pos0, slab=slab),
        grid=(ns, t // qb),
        in_specs=[tok(W_QA), tok(W_QA), tok(W_GW), seq(kc), seq(vc), seq(kvs), seq(winb), full(ovl), full(expand)],
        out_specs=tok(W_ONSA),
        out_shape=jax.ShapeDtypeStruct((ns, t, W_ONSA), MXU),
        scratch_shapes=[pltpu.VMEM((rows, 1), F32), pltpu.VMEM((rows, 1), F32), pltpu.VMEM((rows, LANES), F32)],
        compiler_params=_params(("parallel", "arbitrary")),
        name="nsa_mixer",
    )(qa, qar, gw, kc, vc, kvs, winb, ovl, expand)


def _nsa_tables(l_total, ncp, lpad):
    n_cmp = (l_total - CMP_LEN) // CMP_STRIDE + 1
    c = np.arange(ncp)[:, None]
    n = np.arange(LANES)[None, :]
    ovl = ((c * CMP_STRIDE < n * SEL_BLOCK + SEL_BLOCK) & (c * CMP_STRIDE + CMP_LEN - 1 >= n * SEL_BLOCK)
           & (c < n_cmp))
    s = np.arange(lpad)[None, :]
    expand = (s // SEL_BLOCK) == np.arange(LANES)[:, None]
    return jnp.asarray(ovl, MXU), jnp.asarray(expand, MXU)


def _moba_kernel(q_ref, kv_ref, o_ref, kmean_scr, m_scr, l_scr, acc_scr, *, qb, q0):
    i = pl.program_id(1)
    lpad = kv_ref.shape[1]
    nh = MOBA_HEADS
    k_tiles = -(-nh // 2)

    @pl.when(i == 0)
    def _():
        kmean_scr[...] = jnp.zeros(kmean_scr.shape, F32)
        ones = jnp.ones((8, MOBA_BLOCK), MXU)

        def mean_body(n, carry):
            off = pl.multiple_of(n * MOBA_BLOCK, MOBA_BLOCK)
            tot = _dot(ones, kv_ref[0, pl.ds(off, MOBA_BLOCK), 0:k_tiles * LANES])
            kmean_scr[pl.ds(n, 1), :] = tot[0:1] * (1.0 / MOBA_BLOCK)
            return carry

        lax.fori_loop(0, lpad // MOBA_BLOCK, mean_body, 0)

    t0 = q0 + i * qb
    tq = t0 + lax.broadcasted_iota(jnp.int32, (qb, 1), 0)
    own = tq // MOBA_BLOCK
    lane = lax.broadcasted_iota(jnp.int32, (1, LANES), 1)
    qs = [q_ref[0, :, h * LANES:(h + 1) * LANES] for h in range(nh)]
    kmean = kmean_scr[...].astype(MXU)
    vals = []
    for h in range(nh):
        s_blk = _dot_t(qs[h], kmean[:, (h // 2) * LANES:(h // 2 + 1) * LANES])
        vals.append(jnp.where(lane < own, s_blk, NEG))
    val = jnp.concatenate(vals, axis=0)
    picks = []
    for _ in range(MOBA_TOPK):
        top = jnp.max(val, axis=-1, keepdims=True)
        idx = jnp.min(jnp.where(val == top, lane, LANES), axis=-1, keepdims=True)
        picks.append(idx)
        val = jnp.where(lane == idx, BELOW_NEG, val)

    _reset_state(m_scr, l_scr, acc_scr)

    def body(n, carry):
        off = pl.multiple_of(n * MOBA_BLOCK, MOBA_BLOCK)
        k_pos = off + lax.broadcasted_iota(jnp.int32, (1, MOBA_BLOCK), 1)
        own_mask = (own == n) & (k_pos <= tq)
        for h in range(nh):
            r = slice(h * qb, (h + 1) * qb)
            kt = kv_ref[0, pl.ds(off, MOBA_BLOCK), (h // 2) * LANES:(h // 2 + 1) * LANES]
            v_tile = (nh + h) // 2
            vt = kv_ref[0, pl.ds(off, MOBA_BLOCK), v_tile * LANES:(v_tile + 1) * LANES]
            s = _dot_t(qs[h], kt)
            chosen = (picks[0][r] == n) | (picks[1][r] == n) | (picks[2][r] == n)
            mask = (chosen & (n < own)) | own_mask
            _online_step(s[None], mask[None], vt, m_scr, l_scr, acc_scr, r)
        return carry

    lax.fori_loop(0, (t0 + qb - 1) // MOBA_BLOCK + 1, body, 0)
    o = acc_scr[...] / l_scr[...]
    for h in range(nh):
        o_ref[0, :, h * LANES:(h + 1) * LANES] = o[h * qb:(h + 1) * qb].astype(o_ref.dtype)


def _moba_call(q, kv, *, qb, q0):
    ns, t, _ = q.shape
    rows = MOBA_HEADS * qb
    k_tiles = -(-MOBA_HEADS // 2)
    return pl.pallas_call(
        functools.partial(_moba_kernel, qb=qb, q0=q0),
        grid=(ns, t // qb),
        in_specs=[pl.BlockSpec((1, qb, W_QB), lambda b, i: (b, i, 0)),
                  pl.BlockSpec((1,) + kv.shape[1:], lambda b, i: (b, 0, 0))],
        out_specs=pl.BlockSpec((1, qb, W_OMOBA), lambda b, i: (b, i, 0)),
        out_shape=jax.ShapeDtypeStruct((ns, t, W_OMOBA), MXU),
        scratch_shapes=[pltpu.VMEM((LANES, k_tiles * LANES), F32),
                        pltpu.VMEM((rows, 1), F32), pltpu.VMEM((rows, 1), F32), pltpu.VMEM((rows, LANES), F32)],
        compiler_params=_params(("parallel", "arbitrary")),
        name="moba_mixer",
    )(q, kv)


def _dsa_kernel(qc_ref, qi_ref, gw_ref, kv_ref, o_ref, key_scr, cut_scr, m_scr, l_scr, acc_scr,
                *, qb, q0, ch, k_top):
    t0 = q0 + pl.program_id(1) * qb
    tq = t0 + lax.broadcasted_iota(jnp.int32, (qb, 1), 0)
    lpad = kv_ref.shape[1]
    n_ch = jnp.minimum((t0 + qb + ch - 1) // ch, lpad // ch)
    gw = gw_ref[0]
    w_idx = [gw[:, GW_WI + h:GW_WI + h + 1] for h in range(IDX_HEADS)]
    q_idx = [qi_ref[0, :, h * LANES:(h + 1) * LANES] for h in range(IDX_HEADS)]
    lane_ch = lax.broadcasted_iota(jnp.int32, (1, ch), 1)

    def score_body(c, carry):
        off = pl.multiple_of(c * ch, ch)
        kt = kv_ref[0, pl.ds(off, ch), LANES:2 * LANES]
        sc = w_idx[0] * jnp.maximum(_dot_t(q_idx[0], kt), 0.0)
        for h in range(1, IDX_HEADS):
            sc = sc + w_idx[h] * jnp.maximum(_dot_t(q_idx[h], kt), 0.0)
        sc = jnp.where(sc == 0.0, 0.0, sc)
        bits = lax.bitcast_convert_type(sc, jnp.int32)
        key = bits ^ (lax.shift_right_arithmetic(bits, 31) & 0x7FFFFFFF)
        key_scr[:, pl.ds(off, ch)] = jnp.where(off + lane_ch <= tq, key, INT_MIN)
        return carry

    lax.fori_loop(0, n_ch, score_body, 0)

    def count(pred):
        def body(c, acc):
            off = pl.multiple_of(c * ch, ch)
            hit = jnp.where(pred(key_scr[:, pl.ds(off, ch)], off), 1.0, 0.0)
            part = hit[:, 0:LANES]
            for j in range(1, ch // LANES):
                part = part + hit[:, j * LANES:(j + 1) * LANES]
            return acc + part
        acc = lax.fori_loop(0, n_ch, body, jnp.zeros((qb, LANES), F32))
        return jnp.sum(acc, axis=-1, keepdims=True)

    def bit_body(it, prefix):
        cand = prefix | lax.shift_left(jnp.int32(1), 31 - it)
        cand_key = cand ^ INT_MIN
        cnt = count(lambda k, off: k >= cand_key)
        return jnp.where(cnt >= k_top, cand, prefix)

    prefix = lax.fori_loop(0, 32, bit_body, jnp.zeros((qb, 1), jnp.int32))
    thr = jnp.maximum(prefix ^ INT_MIN, INT_MIN + 1)

    n_gt = count(lambda k, off: k > thr)
    n_eq = count(lambda k, off: k == thr)
    need = k_top - n_gt
    overflow = n_eq > need
    cut_scr[...] = jnp.full((qb, 1), lpad, jnp.int32)

    @pl.when(jnp.max(jnp.where(overflow, 1, 0)) > 0)
    def _():
        def cut_body(it, lo_hi):
            lo, hi = lo_hi
            mid = (lo + hi) // 2
            cnt = count(lambda k, off: (k == thr) & (off + lane_ch <= mid))
            ok = cnt >= need
            return jnp.where(ok, lo, mid + 1), jnp.where(ok, mid, hi)
        lo, _ = lax.fori_loop(0, max(1, math.ceil(math.log2(lpad))), cut_body,
                              (jnp.zeros((qb, 1), jnp.int32), jnp.full((qb, 1), lpad - 1, jnp.int32)))
        cut_scr[...] = jnp.where(overflow, lo, lpad)

    cut = cut_scr[...]

    q = _stack_heads(qc_ref, range(DSA_HEADS))
    _reset_state(m_scr, l_scr, acc_scr)

    def att_body(c, carry):
        off = pl.multiple_of(c * ch, ch)
        kt = kv_ref[0, pl.ds(off, ch), 0:LANES]
        s = _dot_t(q, kt).reshape(DSA_HEADS, qb, ch)
        key = key_scr[:, pl.ds(off, ch)]
        mask = ((key > thr) | ((key == thr) & (off + lane_ch <= cut)))[None]
        _online_step(s, mask, kt, m_scr, l_scr, acc_scr, slice(None))
        return carry

    lax.fori_loop(0, n_ch, att_body, 0)
    o = acc_scr[...] / l_scr[...]
    for h in range(DSA_HEADS):
        o_ref[0, :, h * LANES:(h + 1) * LANES] = o[h * qb:(h + 1) * qb].astype(o_ref.dtype)


def _dsa_call(qc, qi, gw, kv, *, qb, q0, ch, k_top):
    ns, t, _ = qc.shape
    lpad = kv.shape[1]
    rows = DSA_HEADS * qb
    tok = lambda w: pl.BlockSpec((1, qb, w), lambda b, i: (b, i, 0))
    return pl.pallas_call(
        functools.partial(_dsa_kernel, qb=qb, q0=q0, ch=ch, k_top=k_top),
        grid=(ns, t // qb),
        in_specs=[tok(W_QC), tok(W_QI), tok(W_GW), pl.BlockSpec((1,) + kv.shape[1:], lambda b, i: (b, 0, 0))],
        out_specs=tok(W_ODSA),
        out_shape=jax.ShapeDtypeStruct((ns, t, W_ODSA), MXU),
        scratch_shapes=[pltpu.VMEM((qb, lpad), jnp.int32), pltpu.VMEM((qb, 1), jnp.int32),
                        pltpu.VMEM((rows, 1), F32), pltpu.VMEM((rows, 1), F32), pltpu.VMEM((rows, LANES), F32)],
        compiler_params=_params(("parallel", "arbitrary")),
        name="dsa_mixer",
    )(qc, qi, gw, kv)


def _post_attn_kernel(oa_ref, ob_ref, oc_ref, x_ref, g1_ref, sh_ref, sc_ref, wo_ref, lng_ref, lnb_ref,
                      wr_ref, br_ref, x1_ref, h2_ref, gate_ref):
    x = x_ref[...]
    s, r, d = x.shape
    att = None
    off = 0
    for o_ref in (oa_ref, ob_ref, oc_ref):
        w = o_ref.shape[2]
        part = _dot(o_ref[...].reshape(s * r, w), wo_ref[off:off + w, :])
        att = part if att is None else att + part
        off += w
    att = att.reshape(s, r, d)
    x1 = _ln(ALPHA * x + g1_ref[...] * att) * lng_ref[...] + lnb_ref[...]
    x1_ref[...] = x1
    h2 = _ln(x1) * (1.0 + sc_ref[...]) + sh_ref[...]
    h2b = h2.reshape(s * r, d).astype(MXU)
    h2_ref[...] = h2b.reshape(s, r, d)

    logit = _dot(h2b, wr_ref[...]) + br_ref[...]
    lane = lax.broadcasted_iota(jnp.int32, logit.shape, 1)
    is_g = (lane >= N_EXPERTS) & (lane < N_EXPERTS + N_GROUPS)
    lg = jnp.where(is_g, logit, NEG)
    mg = jnp.max(lg, axis=-1, keepdims=True)
    g_lane = jnp.min(jnp.where(lg == mg, lane, LANES), axis=-1, keepdims=True)
    g_den = jnp.sum(jnp.where(is_g, jnp.exp(lg - mg), 0.0), axis=-1, keepdims=True)
    g_w = 1.0 / g_den
    g_idx = g_lane - N_EXPERTS
    in_grp = (lane >= g_idx * EXPERTS_PER_GROUP) & (lane < (g_idx + 1) * EXPERTS_PER_GROUP)
    le = jnp.where(in_grp, logit, NEG)
    me = jnp.max(le, axis=-1, keepdims=True)
    ee = jnp.where(in_grp, jnp.exp(le - me), 0.0)
    pe = ee / jnp.sum(ee, axis=-1, keepdims=True)
    p1 = jnp.max(pe, axis=-1, keepdims=True)
    i1 = jnp.min(jnp.where((pe == p1) & in_grp, lane, LANES), axis=-1, keepdims=True)
    rest = jnp.where(in_grp & (lane != i1), pe, -1.0)
    p2 = jnp.max(rest, axis=-1, keepdims=True)
    i2 = jnp.min(jnp.where(rest == p2, lane, LANES), axis=-1, keepdims=True)
    tot = p1 + p2
    gate = jnp.where(lane == i1, g_w * p1 / tot, jnp.where(lane == i2, g_w * p2 / tot, 0.0))
    gate_ref[...] = gate.reshape(s, r, LANES)


def _post_attn_call(o_parts, x, g1, sh2, sc2, wo, lng, lnb, wr, br, s_blk, r_blk):
    ns, rt, _ = x.shape
    tok = lambda w: pl.BlockSpec((s_blk, r_blk, w), lambda i, j: (i, j, 0))
    mod = pl.BlockSpec((s_blk, 1, D_MODEL), lambda i, j: (i, 0, 0))
    full = lambda a: pl.BlockSpec(a.shape, lambda i, j: (0,) * a.ndim)
    return pl.pallas_call(
        _post_attn_kernel,
        grid=(ns // s_blk, rt // r_blk),
        in_specs=[tok(o.shape[2]) for o in o_parts] + [tok(D_MODEL), mod, mod, mod,
                                                       full(wo), full(lng), full(lnb), full(wr), full(br)],
        out_specs=[tok(D_MODEL), tok(D_MODEL), tok(LANES)],
        out_shape=[jax.ShapeDtypeStruct((ns, rt, D_MODEL), F32),
                   jax.ShapeDtypeStruct((ns, rt, D_MODEL), MXU),
                   jax.ShapeDtypeStruct((ns, rt, LANES), F32)],
        compiler_params=_params(("parallel", "parallel")),
        name="wout_ln_router",
    )(*o_parts, x, g1, sh2, sc2, wo, lng, lnb, wr, br)


def _moe_kernel(h_ref, gate_ref, x_ref, g2_ref, wi_ref, wo_ref, lng_ref, lnb_ref, out_ref, acc_ref):
    e = pl.program_id(2)
    s, r, d = x_ref.shape

    @pl.when(e == 0)
    def _():
        acc_ref[...] = jnp.zeros_like(acc_ref)

    h = h_ref[...].reshape(s * r, d)
    ab = _dot(h, wi_ref[0])
    a, b = ab[:, :D_EXPERT], ab[:, D_EXPERT:]
    gate = gate_ref[...].reshape(s * r, LANES)
    lane = lax.broadcasted_iota(jnp.int32, gate.shape, 1)
    gcol = jnp.sum(jnp.where(lane == e, gate, 0.0), axis=-1, keepdims=True)
    hid = (a * jax.nn.sigmoid(a) * b * gcol).astype(MXU)
    acc_ref[...] += _dot(hid, wo_ref[0])

    @pl.when(e == N_EXPERTS - 1)
    def _():
        f = acc_ref[...].reshape(s, r, d)
        out_ref[...] = _ln(ALPHA * x_ref[...] + g2_ref[...] * f) * lng_ref[...] + lnb_ref[...]


def _moe_call(h2, gate, x1, g2, wi, wo, lng, lnb, s_blk, r_blk):
    ns, rt, _ = x1.shape
    tok = lambda w: pl.BlockSpec((s_blk, r_blk, w), lambda i, j, e: (i, j, 0))
    mod = pl.BlockSpec((s_blk, 1, D_MODEL), lambda i, j, e: (i, 0, 0))
    vec = pl.BlockSpec((1, 1, D_MODEL), lambda i, j, e: (0, 0, 0))
    return pl.pallas_call(
        _moe_kernel,
        grid=(ns // s_blk, rt // r_blk, N_EXPERTS),
        in_specs=[tok(D_MODEL), tok(LANES), tok(D_MODEL), mod,
                  pl.BlockSpec((1, D_MODEL, 2 * D_EXPERT), lambda i, j, e: (e, 0, 0)),
                  pl.BlockSpec((1, D_EXPERT, D_MODEL), lambda i, j, e: (e, 0, 0)),
                  vec, vec],
        out_specs=tok(D_MODEL),
        out_shape=jax.ShapeDtypeStruct((ns, rt, D_MODEL), F32),
        scratch_shapes=[pltpu.VMEM((s_blk * r_blk, D_MODEL), F32)],
        compiler_params=_params(("parallel", "parallel", "arbitrary")),
        name="moe_ln",
    )(h2, gate, x1, g2, wi, wo, lng, lnb)


def _mixers(pr, cmp_w, past, cfg):
    (nsa, win, moba, dsa, gw, qa, qar, qb_, qc, qi, kc, vc, kvs, winb, mobab, dsab) = pr
    ns, t, _ = qa.shape
    if past is None:
        l_total, lpad, win_pos0 = t, t, 0
        win_out = win[:, -min(WINDOW, t):]
    else:
        cache_nsa, cache_moba, cache_dsa, state_win, table, l = past
        past_len = table.shape[1] * cache_nsa.shape[2]
        l_total = past_len + t
        lpad = -(-l_total // MOBA_BLOCK) * MOBA_BLOCK
        kc, vc, kvs, mobab, dsab = _assemble_call(table, cache_nsa, cache_moba, cache_dsa, l,
                                                 nsa, moba, dsa, lpad)
        w_buf = state_win.shape[2]
        rows_pad = -(-(w_buf + t) // LANES) * LANES
        winb, win_out = _win_call(state_win, l, win, rows_pad)
        win_pos0 = past_len - w_buf
    q0 = l_total - t
    chunk_rows = LANES * CMP_STRIDE
    nch = lpad // CMP_STRIDE
    ncp = -(-nch // (2 * LANES)) * 2 * LANES
    kcmp, vcmp = _cmp_call(kc.reshape(ns, nch, chunk_rows), vc.reshape(ns, nch, chunk_rows), *cmp_w, ncp)
    ovl, expand = _nsa_tables(l_total, ncp, lpad)
    qb = cfg['qb']
    o_nsa = _nsa_call(qa, qar, gw, kcmp, vcmp, kvs, winb, ovl, expand,
                      qb=qb, q0=q0, ch=cfg['ch_nsa'], win_pos0=win_pos0)
    o_moba = _moba_call(qb_, mobab, qb=qb, q0=q0)
    o_dsa = _dsa_call(qc, qi, gw, dsab, qb=qb, q0=q0, ch=cfg['ch_dsa'], k_top=min(DSA_TOPK, l_total // 4))
    return (o_nsa, o_moba, o_dsa), win_out


def _layer(x, mod, wts, tabs, cmp_w, past, cfg):
    sh1, sc1, g1, sh2, sc2, g2 = mod
    w_perm, wo, ln1g, ln1b, ln2g, ln2b, wr, br, wei, weo = wts
    pr = _proj_call(x, sh1, sc1, w_perm, tabs, *cfg['blk_a'])
    o_parts, win_out = _mixers(pr, cmp_w, past, cfg)
    x1, h2, gate = _post_attn_call(o_parts, x, g1, sh2, sc2, wo, ln1g, ln1b, wr, br, *cfg['blk_a'])
    y = _moe_call(h2, gate, x1, g2, wei, weo, ln2g, ln2b, *cfg['blk_m'])
    return y, pr[0], pr[2], pr[3], win_out


def _layer_weights(p, l):
    n_pad = LANES - N_EXPERTS - N_GROUPS
    wr = jnp.concatenate([p['w_router_expert'][l], p['w_router_group'][l],
                          jnp.zeros((D_MODEL, n_pad), F32)], axis=1).astype(MXU)
    br = jnp.concatenate([p['b_router_expert'][l], p['b_router_group'][l], jnp.zeros((n_pad,), F32)])[None, :]
    vec = lambda a: a[l].reshape(1, 1, D_MODEL)
    return (_permute_w_in(p['w_in'][l]), _pad_w_out(p['w_out'][l]), vec(p['ln1_g']), vec(p['ln1_b']),
            vec(p['ln2_g']), vec(p['ln2_b']), wr, br,
            p['w_expert_in'][l].astype(MXU), p['w_expert_out'][l].astype(MXU))


def kernel(x_prompt, x_sample, c_prompt, c_sample, cache_nsa, cache_moba, cache_dsa, state_win, page_table, w_ada, b_ada, w_in, cmp_pe, cmp_w1, cmp_w2, w_out, ln1_g, ln1_b, ln2_g, ln2_b, w_router_group, b_router_group, w_router_expert, b_router_expert, w_expert_in, w_expert_out):
    p = dict(w_in=w_in, w_out=w_out, ln1_g=ln1_g, ln1_b=ln1_b, ln2_g=ln2_g, ln2_b=ln2_b,
             w_router_group=w_router_group, b_router_group=b_router_group,
             w_router_expert=w_router_expert, b_router_expert=b_router_expert,
             w_expert_in=w_expert_in, w_expert_out=w_expert_out)
    nb_p, t_p, _ = x_prompt.shape
    nb_s, t_s, _ = x_sample.shape
    past_len = page_table.shape[1] * cache_nsa.shape[2]
    lpad_s = -(-(past_len + t_s) // MOBA_BLOCK) * MOBA_BLOCK

    n_c = nb_p + nb_s
    n_c_pad = -(-n_c // 8) * 8
    c_all = jnp.concatenate([c_prompt, c_sample, jnp.zeros((n_c_pad - n_c, D_MODEL), F32)], axis=0)
    mod_all = _ada_call(c_all, w_ada, b_ada)

    tabs_p = _rope_tables(jnp.arange(t_p))
    tabs_s = _rope_tables(past_len + jnp.arange(t_s))
    cfg_p = dict(blk_a=(1, 512), blk_m=(1, 1024), qb=128, ch_nsa=512, ch_dsa=512)
    cfg_s = dict(blk_a=(32, t_s), blk_m=(nb_s, t_s), qb=t_s, ch_nsa=lpad_s, ch_dsa=lpad_s)

    y_p, y_s = x_prompt, x_sample
    st_p, st_s = [], []
    for l in range(DEPTH):
        mod_l = mod_all[l]
        split = lambda m: tuple(m[:, None, k * D_MODEL:(k + 1) * D_MODEL] for k in range(6))
        mod_p = split(mod_l[:nb_p])
        mod_s = split(mod_l[nb_p:n_c])
        wts = _layer_weights(p, l)
        cmp_w = _cmp_weights(cmp_pe[l], cmp_w1[l], cmp_w2[l])
        y_p, *sp = _layer(y_p, mod_p, wts, tabs_p, cmp_w, None, cfg_p)
        past = (cache_nsa, cache_moba, cache_dsa, state_win, page_table, l)
        y_s, *ss = _layer(y_s, mod_s, wts, tabs_s, cmp_w, past, cfg_s)
        st_p.append(sp)
        st_s.append(ss)

    def stack(st, k, tail):
        a = jnp.stack([s[k] for s in st])
        return a.reshape(a.shape[:3] + tail)

    nsa_t = (4, NSA_KV_HEADS, HEAD_DIM)
    moba_t = (2, MOBA_HEADS, HEAD_DIM)
    dsa_t = (3, HEAD_DIM)
    win_t = (2, NSA_KV_HEADS, HEAD_DIM)
    return (y_p, y_s, stack(st_p, 0, nsa_t), stack(st_s, 0, nsa_t), stack(st_p, 1, moba_t),
            stack(st_s, 1, moba_t), stack(st_p, 2, dsa_t), stack(st_s, 2, dsa_t),
            stack(st_p, 3, win_t), stack(st_s, 3, win_t))
```

```python
import functools
import math

import numpy as np
import jax
import jax.numpy as jnp
from jax import lax
from jax.experimental import pallas as pl
from jax.experimental.pallas import tpu as pltpu

D_MODEL = 1024
DEPTH = 2
HEAD_DIM = 64
NSA_HEADS = 6
NSA_KV_HEADS = 2
NSA_GROUP = NSA_HEADS // NSA_KV_HEADS
MOBA_HEADS = 5
DSA_HEADS = 5
D_MIX = (NSA_HEADS + MOBA_HEADS + DSA_HEADS) * HEAD_DIM
CMP_LEN = 32
CMP_STRIDE = 16
CMP_HIDDEN = 256
SEL_BLOCK = 64
SEL_TOPN = 16
WINDOW = 512
MOBA_BLOCK = 256
MOBA_TOPK = 3
IDX_HEADS = 4
DSA_TOPK = 256
ROPE_THETA = 500000.0
ROT_DIM = HEAD_DIM // 4
N_GROUPS = 4
EXPERTS_PER_GROUP = 8
N_EXPERTS = N_GROUPS * EXPERTS_PER_GROUP
D_EXPERT = 256
ALPHA = (2 * DEPTH) ** 0.25
LN_EPS = 1e-5
NEG = -1e30
BIG = 1e30
BELOW_NEG = -3e38
M_INIT = -1e29
INT_MIN = -2 ** 31

LANES = 128
F32 = jnp.float32
BF16 = jnp.bfloat16
MXU = BF16
Q_SCALE = HEAD_DIM ** -0.5

W_QA = NSA_HEADS * LANES
W_KVA = 6 * NSA_KV_HEADS * HEAD_DIM
W_NSA = 4 * NSA_KV_HEADS * HEAD_DIM
W_WIN = 2 * NSA_KV_HEADS * HEAD_DIM
W_QB = MOBA_HEADS * LANES
W_QC = DSA_HEADS * LANES
W_MOBA = 2 * MOBA_HEADS * HEAD_DIM
W_DSA = 3 * HEAD_DIM
W_DSA_PAD = 2 * LANES
W_QI = IDX_HEADS * LANES
W_GW = LANES
OFF_QA = 0
OFF_KVA = OFF_QA + W_QA
OFF_QB = OFF_KVA + W_KVA
OFF_QC = OFF_QB + W_QB
OFF_MOBA = OFF_QC + W_QC
OFF_DSA = OFF_MOBA + W_MOBA
OFF_QI = OFF_DSA + W_DSA_PAD
OFF_GW = OFF_QI + W_QI
W_PROJ = OFF_GW + W_GW
GW_WI = 3 * NSA_HEADS

W_ONSA = NSA_HEADS * LANES
W_OMOBA = MOBA_HEADS * LANES
W_ODSA = DSA_HEADS * LANES
W_OALL = W_ONSA + W_OMOBA + W_ODSA

VMEM_LIMIT = 56 * 1024 * 1024


def _params(sem):
    return pltpu.CompilerParams(dimension_semantics=sem, vmem_limit_bytes=VMEM_LIMIT)


def _ln(x):
    mu = jnp.mean(x, axis=-1, keepdims=True)
    xc = x - mu
    var = jnp.mean(xc * xc, axis=-1, keepdims=True)
    return xc * lax.rsqrt(var + LN_EPS)


def _dot(a, b):
    return jnp.dot(a, b, preferred_element_type=F32)


def _dot_t(a, b):
    return lax.dot_general(a, b, (((1,), (1,)), ((), ())), preferred_element_type=F32)


def _ada_kernel(c_ref, w_ref, b_ref, o_ref):
    c = c_ref[...]
    a = (c * jax.nn.sigmoid(c)).astype(MXU)
    o_ref[0] = _dot(a, w_ref[0].astype(MXU)) + b_ref[0]


def _ada_call(c_all, w_ada, b_ada):
    n = c_all.shape[0]
    tn = 1536
    return pl.pallas_call(
        _ada_kernel,
        grid=(DEPTH, 6 * D_MODEL // tn),
        in_specs=[
            pl.BlockSpec((n, D_MODEL), lambda l, j: (0, 0)),
            pl.BlockSpec((1, D_MODEL, tn), lambda l, j: (l, 0, j)),
            pl.BlockSpec((1, 1, tn), lambda l, j: (l, 0, j)),
        ],
        out_specs=pl.BlockSpec((1, n, tn), lambda l, j: (l, 0, j)),
        out_shape=jax.ShapeDtypeStruct((DEPTH, n, 6 * D_MODEL), F32),
        compiler_params=_params(("arbitrary", "arbitrary")),
        name="ada_mod",
    )(c_all, w_ada, b_ada.reshape(DEPTH, 1, 6 * D_MODEL))


def _rope_tile(t, cos, s_lo, s_hi, first_half_only):
    if first_half_only:
        lane = lax.broadcasted_iota(jnp.int32, cos.shape, 2)
        keep = lane < HEAD_DIM
        cos = jnp.where(keep, cos, 1.0)
        s_lo = jnp.where(keep, s_lo, 0.0)
        s_hi = jnp.where(keep, s_hi, 0.0)
    half = ROT_DIM // 2
    up = pltpu.roll(t, LANES - half, axis=2)
    dn = pltpu.roll(t, half, axis=2)
    return t * cos + up * s_lo + dn * s_hi


def _rope_slab(p, cos, s_lo, s_hi, modes):
    tiles = []
    for j, m in enumerate(modes):
        t = p[:, :, j * LANES:(j + 1) * LANES]
        if m != 'n':
            t = _rope_tile(t, cos, s_lo, s_hi, m == 'h')
        tiles.append(t)
    return tiles[0] if len(tiles) == 1 else jnp.concatenate(tiles, axis=2)


PROJ_F32_WIDTHS = (W_NSA, W_WIN, W_MOBA, W_DSA, W_GW)
PROJ_MXU_WIDTHS = (W_QA, W_QA, W_QB, W_QC, W_QI, LANES, LANES, 2 * LANES, W_WIN, W_MOBA, W_DSA_PAD)


def _proj_kernel(x_ref, sh_ref, sc_ref, w_ref, cos_ref, slo_ref, shi_ref,
                 nsa_ref, win_ref, moba_ref, dsa_ref, gw_ref,
                 qa_ref, qar_ref, qb_ref, qc_ref, qi_ref,
                 kc_ref, vc_ref, kvs_ref, winb_ref, mobab_ref, dsab_ref):
    x = x_ref[...]
    s, r, d = x.shape
    h = _ln(x) * (1.0 + sc_ref[...]) + sh_ref[...]
    hb = h.reshape(s * r, d).astype(MXU)
    rope = functools.partial(_rope_slab, cos=cos_ref[...], s_lo=slo_ref[...], s_hi=shi_ref[...])

    def seg(off, width):
        return _dot(hb, w_ref[:, off:off + width]).reshape(s, r, width)

    qa = seg(OFF_QA, W_QA)
    qa_ref[...] = qa.astype(MXU)
    qar_ref[...] = rope(qa, modes='f' * NSA_HEADS).astype(MXU)
    kva = seg(OFF_KVA, W_KVA)
    nsa = rope(kva[:, :, :W_NSA], modes='nnfn')
    nsa_ref[...] = nsa
    kc_ref[...] = nsa[:, :, 0:LANES].astype(MXU)
    vc_ref[...] = nsa[:, :, LANES:2 * LANES].astype(MXU)
    kvs_ref[...] = nsa[:, :, 2 * LANES:].astype(MXU)
    win = rope(kva[:, :, W_NSA:], modes='fn')
    win_ref[...] = win
    winb_ref[...] = win.astype(MXU)
    qb_ref[...] = rope(seg(OFF_QB, W_QB), modes='f' * MOBA_HEADS).astype(MXU)
    qc_ref[...] = rope(seg(OFF_QC, W_QC), modes='f' * DSA_HEADS).astype(MXU)
    moba = rope(seg(OFF_MOBA, W_MOBA), modes='ffhnn')
    moba_ref[...] = moba
    mobab_ref[...] = moba.astype(MXU)
    dsa = rope(seg(OFF_DSA, W_DSA_PAD), modes='hh')
    dsa_ref[...] = dsa[:, :, :W_DSA]
    dsab_ref[...] = dsa.astype(MXU)
    qi_ref[...] = rope(seg(OFF_QI, W_QI), modes='f' * IDX_HEADS).astype(MXU)
    gw_ref[...] = seg(OFF_GW, W_GW)


def _proj_call(x, sh, sc, w_perm, tabs, s_blk, r_blk):
    ns, rt, _ = x.shape
    cos, s_lo, s_hi = tabs
    tok = lambda w: pl.BlockSpec((s_blk, r_blk, w), lambda i, j: (i, j, 0))
    mod = pl.BlockSpec((s_blk, 1, D_MODEL), lambda i, j: (i, 0, 0))
    tab = pl.BlockSpec((1, r_blk, LANES), lambda i, j: (0, j, 0))
    widths = PROJ_F32_WIDTHS + PROJ_MXU_WIDTHS
    dtypes = (F32,) * len(PROJ_F32_WIDTHS) + (MXU,) * len(PROJ_MXU_WIDTHS)
    return pl.pallas_call(
        _proj_kernel,
        grid=(ns // s_blk, rt // r_blk),
        in_specs=[tok(D_MODEL), mod, mod,
                  pl.BlockSpec((D_MODEL, W_PROJ), lambda i, j: (0, 0)),
                  tab, tab, tab],
        out_specs=[tok(w) for w in widths],
        out_shape=[jax.ShapeDtypeStruct((ns, rt, w), dt) for w, dt in zip(widths, dtypes)],
        compiler_params=_params(("parallel", "parallel")),
        name="ln_proj_rope",
    )(x, sh, sc, w_perm, cos, s_lo, s_hi)


def _pad_heads(w, halves, scale):
    z = jnp.zeros((w.shape[0], HEAD_DIM), w.dtype)
    cols = []
    for h, half in enumerate(halves):
        wh = w[:, h * HEAD_DIM:(h + 1) * HEAD_DIM] * scale
        cols += [wh, z] if half == 0 else [z, wh]
    return jnp.concatenate(cols, axis=1)


def _permute_w_in(w_in_l):
    sizes = (NSA_HEADS * HEAD_DIM, 6 * NSA_KV_HEADS * HEAD_DIM, 3 * NSA_HEADS,
             MOBA_HEADS * HEAD_DIM, MOBA_HEADS * HEAD_DIM, MOBA_HEADS * HEAD_DIM,
             DSA_HEADS * HEAD_DIM, 2 * HEAD_DIM, IDX_HEADS * HEAD_DIM, HEAD_DIM, IDX_HEADS)
    offs = np.concatenate([[0], np.cumsum(sizes)])
    part = lambda k: w_in_l[:, int(offs[k]):int(offs[k + 1])]
    zeros = lambda n: jnp.zeros((D_MODEL, n), w_in_l.dtype)
    cols = [_pad_heads(part(0), [h // NSA_GROUP for h in range(NSA_HEADS)], Q_SCALE),
            part(1),
            _pad_heads(part(3), [h % 2 for h in range(MOBA_HEADS)], Q_SCALE),
            _pad_heads(part(6), [0] * DSA_HEADS, Q_SCALE),
            part(4), part(5),
            part(7), part(9), zeros(W_DSA_PAD - W_DSA),
            _pad_heads(part(8), [0] * IDX_HEADS, 1.0),
            part(2), part(10), zeros(W_GW - 3 * NSA_HEADS - IDX_HEADS)]
    return jnp.concatenate(cols, axis=1).astype(MXU)


def _pad_w_out(w_out_l):
    z = jnp.zeros((HEAD_DIM, D_MODEL), w_out_l.dtype)
    rows = []
    head = lambda h: w_out_l[h * HEAD_DIM:(h + 1) * HEAD_DIM]
    for h in range(NSA_HEADS):
        rows += [head(h), z] if h // NSA_GROUP == 0 else [z, head(h)]
    for h in range(MOBA_HEADS):
        rows += [head(NSA_HEADS + h), z] if (MOBA_HEADS + h) % 2 == 0 else [z, head(NSA_HEADS + h)]
    for h in range(DSA_HEADS):
        rows += [z, head(NSA_HEADS + MOBA_HEADS + h)]
    return jnp.concatenate(rows, axis=0).astype(MXU)


def _rope_tables(pos):
    half = ROT_DIM // 2
    inv = ROPE_THETA ** (-jnp.arange(0, ROT_DIM, 2, dtype=F32) / ROT_DIM)
    ang = pos.astype(F32)[:, None] * inv[None, :]
    cos, sin = jnp.cos(ang), jnp.sin(ang)
    t = pos.shape[0]
    ones = jnp.ones((t, HEAD_DIM - ROT_DIM), F32)
    zeros = jnp.zeros((t, HEAD_DIM - ROT_DIM), F32)
    zh = jnp.zeros((t, half), F32)
    c = jnp.concatenate([cos, cos, ones], axis=1)
    s_lo = jnp.concatenate([-sin, zh, zeros], axis=1)
    s_hi = jnp.concatenate([zh, sin, zeros], axis=1)
    rep = lambda a: jnp.concatenate([a, a], axis=1)[None]
    return rep(c), rep(s_lo), rep(s_hi)


def _assemble_kernel(tbl_ref, nsa_ref, moba_ref, dsa_ref, nsa_new_ref, moba_new_ref, dsa_new_ref,
                     kc_ref, vc_ref, kvs_ref, mob_ref, dsao_ref, *, n_pages):
    p = pl.program_id(1)
    rows = kc_ref.shape[1]

    def emit(nsa, moba, dsa):
        kc_ref[0] = nsa[:, 0:LANES].astype(MXU)
        vc_ref[0] = nsa[:, LANES:2 * LANES].astype(MXU)
        kvs_ref[0] = nsa[:, 2 * LANES:].astype(MXU)
        mob_ref[0] = moba.astype(MXU)
        dsao_ref[0] = jnp.concatenate([dsa, jnp.zeros((rows, W_DSA_PAD - W_DSA), F32)], axis=1).astype(MXU)

    @pl.when(p < n_pages)
    def _():
        emit(nsa_ref[0, 0], moba_ref[0, 0], dsa_ref[0, 0])

    @pl.when(p == n_pages)
    def _():
        t_new = nsa_new_ref.shape[1]
        tail = lambda new: jnp.concatenate([new, jnp.zeros((rows - t_new, new.shape[1]), F32)], axis=0)
        emit(tail(nsa_new_ref[0]), tail(moba_new_ref[0]), tail(dsa_new_ref[0]))

    @pl.when(p > n_pages)
    def _():
        emit(jnp.zeros((rows, W_NSA), F32), jnp.zeros((rows, W_MOBA), F32), jnp.zeros((rows, W_DSA), F32))


def _assemble_call(table, cache_nsa, cache_moba, cache_dsa, l, nsa_new, moba_new, dsa_new, lpad):
    ns, n_pages = table.shape
    page = cache_nsa.shape[2]
    n_blk = lpad // page
    pool = lambda w: pl.BlockSpec(
        (1, 1, page, w), lambda b, p, tbl: (l, tbl[b, jnp.minimum(p, n_pages - 1)], 0, 0))
    new = lambda w: pl.BlockSpec((1, nsa_new.shape[1], w), lambda b, p, tbl: (b, 0, 0))
    out = lambda w: pl.BlockSpec((1, page, w), lambda b, p, tbl: (b, p, 0))
    widths = (LANES, LANES, 2 * LANES, W_MOBA, W_DSA_PAD)
    return pl.pallas_call(
        functools.partial(_assemble_kernel, n_pages=n_pages),
        grid_spec=pltpu.PrefetchScalarGridSpec(
            num_scalar_prefetch=1,
            grid=(ns, n_blk),
            in_specs=[pool(W_NSA), pool(W_MOBA), pool(W_DSA), new(W_NSA), new(W_MOBA), new(W_DSA)],
            out_specs=[out(w) for w in widths],
        ),
        out_shape=[jax.ShapeDtypeStruct((ns, lpad, w), MXU) for w in widths],
        compiler_params=_params(("parallel", "arbitrary")),
        name="assemble_pages",
    )(table,
      cache_nsa.reshape(cache_nsa.shape[:3] + (W_NSA,)),
      cache_moba.reshape(cache_moba.shape[:3] + (W_MOBA,)),
      cache_dsa.reshape(cache_dsa.shape[:3] + (W_DSA,)),
      nsa_new, moba_new, dsa_new)


def _win_kernel(buf_ref, new_ref, winb_ref, wout_ref):
    buf = buf_ref[0, 0]
    new = new_ref[0]
    pad = winb_ref.shape[1] - buf.shape[0] - new.shape[0]
    winb_ref[0] = jnp.concatenate([buf, new, jnp.zeros((pad, buf.shape[1]), F32)], axis=0).astype(MXU)
    wout_ref[0] = jnp.concatenate([buf[new.shape[0]:], new], axis=0)


def _win_call(state_win, l, win_new, rows_pad):
    ns, w_buf = state_win.shape[1], state_win.shape[2]
    t_new = win_new.shape[1]
    return pl.pallas_call(
        _win_kernel,
        grid=(ns,),
        in_specs=[pl.BlockSpec((1, 1, w_buf, W_WIN), lambda b: (l, b, 0, 0)),
                  pl.BlockSpec((1, t_new, W_WIN), lambda b: (b, 0, 0))],
        out_specs=[pl.BlockSpec((1, rows_pad, W_WIN), lambda b: (b, 0, 0)),
                   pl.BlockSpec((1, w_buf, W_WIN), lambda b: (b, 0, 0))],
        out_shape=[jax.ShapeDtypeStruct((ns, rows_pad, W_WIN), MXU),
                   jax.ShapeDtypeStruct((ns, w_buf, W_WIN), F32)],
        compiler_params=_params(("parallel",)),
        name="window_buffer",
    )(state_win.reshape(state_win.shape[:3] + (W_WIN,)), win_new)


def _cmp_kernel(kx_ref, vx_ref, pe_ref, w1_ref, w2_ref, kc_ref, vc_ref):
    nch = kx_ref.shape[1]
    nout = kc_ref.shape[1]
    for j, (x_ref, o_ref) in enumerate(((kx_ref, kc_ref), (vx_ref, vc_ref))):
        x = x_ref[0].astype(F32)
        a = _dot((x + pe_ref[j, 0]).astype(MXU), w1_ref[j, 0])
        b = _dot((x + pe_ref[j, 1]).astype(MXU), w1_ref[j, 1])
        hid = a + pltpu.roll(b, nch - 1, axis=0)
        out = _dot(jax.nn.gelu(hid).astype(MXU), w2_ref[j])
        o_ref[0, 0:nch] = out.astype(o_ref.dtype)
        if nout > nch:
            o_ref[0, nch:nout] = jnp.zeros((nout - nch, LANES), o_ref.dtype)


def _cmp_call(kx, vx, pe_e, w1_e, w2_e, nout):
    ns, nch, wx = kx.shape
    full = lambda a: pl.BlockSpec(a.shape, lambda b: (0,) * a.ndim)
    return pl.pallas_call(
        _cmp_kernel,
        grid=(ns,),
        in_specs=[pl.BlockSpec((1, nch, wx), lambda b: (b, 0, 0))] * 2 + [full(pe_e), full(w1_e), full(w2_e)],
        out_specs=[pl.BlockSpec((1, nout, LANES), lambda b: (b, 0, 0))] * 2,
        out_shape=[jax.ShapeDtypeStruct((ns, nout, LANES), MXU)] * 2,
        compiler_params=_params(("parallel",)),
        name="nsa_compress",
    )(kx, vx, pe_e, w1_e, w2_e)


def _cmp_weights(cmp_pe_l, cmp_w1_l, cmp_w2_l):
    half_rows = CMP_LEN // 2
    kv = NSA_KV_HEADS
    w1 = cmp_w1_l.reshape(2, 2, half_rows, HEAD_DIM, CMP_HIDDEN)
    w1_e = jnp.zeros((2, 2, half_rows, kv, HEAD_DIM, kv, CMP_HIDDEN), F32)
    for k in range(kv):
        w1_e = w1_e.at[:, :, :, k, :, k, :].set(w1)
    w1_e = w1_e.reshape(2, 2, half_rows * kv * HEAD_DIM, kv * CMP_HIDDEN).astype(MXU)
    w2_e = jnp.zeros((2, kv, CMP_HIDDEN, kv, HEAD_DIM), F32)
    for k in range(kv):
        w2_e = w2_e.at[:, k, :, k, :].set(cmp_w2_l)
    w2_e = w2_e.reshape(2, kv * CMP_HIDDEN, kv * HEAD_DIM).astype(MXU)
    pe = cmp_pe_l.reshape(2, 2, half_rows, 1, HEAD_DIM)
    pe_e = jnp.broadcast_to(pe, (2, 2, half_rows, kv, HEAD_DIM)).reshape(2, 2, 1, half_rows * kv * HEAD_DIM)
    return pe_e, w1_e, w2_e


def _online_step(sm, vt, m_scr, l_scr, acc_scr, rows):
    m_prev = m_scr[rows, :]
    m_new = jnp.maximum(m_prev, jnp.max(sm, axis=-1, keepdims=True))
    alpha = jnp.exp(m_prev - m_new)
    p = jnp.exp(sm - m_new)
    l_scr[rows, :] = alpha * l_scr[rows, :] + jnp.sum(p, axis=-1, keepdims=True)
    acc_scr[rows, :] = alpha * acc_scr[rows, :] + _dot(p.astype(MXU), vt)
    m_scr[rows, :] = m_new


def _reset_state(m_scr, l_scr, acc_scr):
    m_scr[...] = jnp.full(m_scr.shape, M_INIT, F32)
    l_scr[...] = jnp.zeros(l_scr.shape, F32)
    acc_scr[...] = jnp.zeros(acc_scr.shape, F32)


def _biased_softmax(sm):
    m = jnp.maximum(jnp.max(sm, axis=-1, keepdims=True), M_INIT)
    e = jnp.exp(sm - m)
    den = jnp.sum(e, axis=-1, keepdims=True)
    return e * jnp.where(den > 0.0, 1.0 / den, 0.0)


def _add_bias(s, bias, qb):
    c = bias.shape[1]
    outer = bias.shape[0] // qb
    groups = s.shape[0] // bias.shape[0]
    return (s.reshape(outer, groups, qb, c) + bias.reshape(outer, 1, qb, c)).reshape(s.shape)


def _stack_heads(ref, tiles):
    parts = [ref[0, :, t * LANES:(t + 1) * LANES].astype(F32) for t in tiles]
    return jnp.concatenate(parts, axis=0).astype(MXU)


def _top_blocks(imp, t0, qb):
    qp = max(qb, LANES)
    kv = imp.shape[0] // qb
    blk = lax.broadcasted_iota(jnp.int32, (LANES, 1), 0)
    tq = t0 + lax.broadcasted_iota(jnp.int32, (1, qp), 1)
    tb = tq // SEL_BLOCK
    forced = (blk == 0) | (blk == tb) | (blk == tb - 1)
    admissible = blk * SEL_BLOCK <= tq
    vals = []
    for k in range(kv):
        part = imp[k * qb:(k + 1) * qb]
        if qp > qb:
            part = jnp.concatenate([part, jnp.zeros((qp - qb, LANES), F32)], axis=0)
        vals.append(jnp.where(forced, BIG, jnp.where(admissible, part.T, NEG)))
    val = jnp.concatenate(vals, axis=1)
    sel = jnp.zeros(val.shape, F32)
    for _ in range(SEL_TOPN):
        top = jnp.max(val, axis=0, keepdims=True)
        idx = jnp.min(jnp.where(val == top, blk, LANES), axis=0, keepdims=True)
        pick = blk == idx
        sel = jnp.where(pick, 1.0, sel)
        val = jnp.where(pick, BELOW_NEG, val)
    return jnp.concatenate([sel[:, k * qp:(k + 1) * qp].T[:qb] for k in range(kv)], axis=0)


def _nsa_kernel(qa_ref, qar_ref, gw_ref, kc_ref, vc_ref, kvs_ref, win_ref, ovl_ref, exp_ref, o_ref,
                m_scr, l_scr, acc_scr, *, qb, q0, ch, win_pos0, slab):
    t0 = q0 + pl.program_id(1) * qb
    tq = t0 + lax.broadcasted_iota(jnp.int32, (qb, 1), 0)
    ncp = kc_ref.shape[1]
    win_rows = win_ref.shape[1]
    g, kv = NSA_GROUP, NSA_KV_HEADS
    qa = _stack_heads(qa_ref, range(NSA_HEADS))
    qr = _stack_heads(qar_ref, range(NSA_HEADS))

    c_end = lax.broadcasted_iota(jnp.int32, (1, ncp), 1) * CMP_STRIDE + (CMP_LEN - 1)
    bias_c = jnp.where(c_end <= tq, 0.0, NEG)
    p_c = _biased_softmax(_add_bias(_dot_t(qa, kc_ref[0]), bias_c, qb))
    o_cmp = _dot(p_c.astype(MXU), vc_ref[0])
    p4 = p_c.reshape(kv, g, qb, ncp)
    p_sum = p4[:, 0]
    for j in range(1, g):
        p_sum = p_sum + p4[:, j]
    p_sum = p_sum.reshape(kv * qb, ncp)
    p_hi = p_sum.astype(MXU)
    p_lo = (p_sum - p_hi.astype(F32)).astype(MXU)
    imp = _dot(p_hi, ovl_ref[...]) + _dot(p_lo, ovl_ref[...])
    sel_b = _top_blocks(imp, t0, qb).astype(MXU)

    _reset_state(m_scr, l_scr, acc_scr)

    def sel_step(c, causal):
        off = pl.multiple_of(c * ch, ch)
        kt = kvs_ref[0, pl.ds(off, ch), 0:LANES]
        vt = kvs_ref[0, pl.ds(off, ch), LANES:2 * LANES]
        bias = (_dot(sel_b, exp_ref[:, pl.ds(off, ch)]) - 1.0) * BIG
        if causal:
            k_pos = off + lax.broadcasted_iota(jnp.int32, (1, ch), 1)
            bias = _add_bias(bias, jnp.where(k_pos <= tq, 0.0, NEG), qb)
        _online_step(_add_bias(_dot_t(qr, kt), bias, qb), vt, m_scr, l_scr, acc_scr, slice(None))

    c_diag = t0 // ch

    def sel_body(c, carry):
        sel_step(c, False)
        return carry

    lax.fori_loop(0, c_diag, sel_body, 0)
    sel_step(c_diag, True)
    o_sel = acc_scr[...] / l_scr[...]

    if win_rows > slab:
        w_start = pl.multiple_of(jnp.clip(t0 - WINDOW - win_pos0, 0, win_rows - slab), 16)
    else:
        w_start = 0
    w_pos = win_pos0 + w_start + lax.broadcasted_iota(jnp.int32, (1, slab), 1)
    dist = tq - w_pos
    bias_w = jnp.where((dist >= 0) & (dist <= WINDOW), 0.0, NEG)
    kt = win_ref[0, pl.ds(w_start, slab), 0:LANES]
    vt = win_ref[0, pl.ds(w_start, slab), LANES:2 * LANES]
    p_w = _biased_softmax(_add_bias(_dot_t(qr, kt), bias_w, qb))
    o_win = _dot(p_w.astype(MXU), vt)

    gate = jax.nn.sigmoid(gw_ref[0])
    for h in range(NSA_HEADS):
        r = slice(h * qb, (h + 1) * qb)
        o = (gate[:, 3 * h:3 * h + 1] * o_cmp[r] + gate[:, 3 * h + 1:3 * h + 2] * o_sel[r]
             + gate[:, 3 * h + 2:3 * h + 3] * o_win[r])
        o_ref[0, :, h * LANES:(h + 1) * LANES] = o.astype(o_ref.dtype)


def _nsa_call(qa, qar, gw, kc, vc, kvs, winb, ovl, expand, *, qb, q0, ch, win_pos0):
    ns, t, _ = qa.shape
    assert ch % qb == 0 and (q0 % qb == 0 or ch == kvs.shape[1])
    slab = min(-(-(WINDOW + qb) // LANES) * LANES, winb.shape[1])
    tok = lambda w: pl.BlockSpec((1, qb, w), lambda b, i: (b, i, 0))
    seq = lambda a: pl.BlockSpec((1,) + a.shape[1:], lambda b, i: (b, 0, 0))
    full = lambda a: pl.BlockSpec(a.shape, lambda b, i: (0,) * a.ndim)
    rows = NSA_HEADS * qb
    return pl.pallas_call(
        functools.partial(_nsa_kernel, qb=qb, q0=q0, ch=ch, win_pos0=win_pos0, slab=slab),
        grid=(ns, t // qb),
        in_specs=[tok(W_QA), tok(W_QA), tok(W_GW), seq(kc), seq(vc), seq(kvs), seq(winb), full(ovl), full(expand)],
        out_specs=tok(W_ONSA),
        out_shape=jax.ShapeDtypeStruct((ns, t, W_ONSA), MXU),
        scratch_shapes=[pltpu.VMEM((rows, 1), F32), pltpu.VMEM((rows, 1), F32), pltpu.VMEM((rows, LANES), F32)],
        compiler_params=_params(("parallel", "arbitrary")),
        name="nsa_mixer",
    )(qa, qar, gw, kc, vc, kvs, winb, ovl, expand)


def _nsa_tables(l_total, ncp, lpad):
    n_cmp = (l_total - CMP_LEN) // CMP_STRIDE + 1
    c = np.arange(ncp)[:, None]
    n = np.arange(LANES)[None, :]
    ovl = ((c * CMP_STRIDE < n * SEL_BLOCK + SEL_BLOCK) & (c * CMP_STRIDE + CMP_LEN - 1 >= n * SEL_BLOCK)
           & (c < n_cmp))
    s = np.arange(lpad)[None, :]
    expand = (s // SEL_BLOCK) == np.arange(LANES)[:, None]
    return jnp.asarray(ovl, MXU), jnp.asarray(expand, MXU)


def _moba_kernel(q_ref, kv_ref, o_ref, kmean_scr, m_scr, l_scr, acc_scr, *, qb, q0, ch):
    i = pl.program_id(1)
    lpad = kv_ref.shape[1]
    nh = MOBA_HEADS
    k_tiles = -(-nh // 2)
    blocks_per_chunk = ch // MOBA_BLOCK

    @pl.when(i == 0)
    def _():
        kmean_scr[...] = jnp.zeros(kmean_scr.shape, F32)
        ones = jnp.ones((8, MOBA_BLOCK), MXU)

        def mean_body(n, carry):
            off = pl.multiple_of(n * MOBA_BLOCK, MOBA_BLOCK)
            tot = _dot(ones, kv_ref[0, pl.ds(off, MOBA_BLOCK), 0:k_tiles * LANES])
            kmean_scr[pl.ds(n, 1), :] = tot[0:1] * (1.0 / MOBA_BLOCK)
            return carry

        lax.fori_loop(0, lpad // MOBA_BLOCK, mean_body, 0)

    t0 = q0 + i * qb
    tq = t0 + lax.broadcasted_iota(jnp.int32, (qb, 1), 0)
    own = tq // MOBA_BLOCK
    lane = lax.broadcasted_iota(jnp.int32, (1, LANES), 1)
    qs = [q_ref[0, :, h * LANES:(h + 1) * LANES] for h in range(nh)]
    kmean = kmean_scr[...].astype(MXU)
    vals = []
    for h in range(nh):
        s_blk = _dot_t(qs[h], kmean[:, (h // 2) * LANES:(h // 2 + 1) * LANES])
        vals.append(jnp.where(lane < own, s_blk, NEG))
    val = jnp.concatenate(vals, axis=0)
    picks = []
    for _ in range(MOBA_TOPK):
        top = jnp.max(val, axis=-1, keepdims=True)
        idx = jnp.min(jnp.where(val == top, lane, LANES), axis=-1, keepdims=True)
        picks.append(idx)
        val = jnp.where(lane == idx, BELOW_NEG, val)

    _reset_state(m_scr, l_scr, acc_scr)
    pos_in_block = lax.broadcasted_iota(jnp.int32, (1, MOBA_BLOCK), 1)

    def step(c, with_own):
        off = pl.multiple_of(c * ch, ch)
        for h in range(nh):
            r = slice(h * qb, (h + 1) * qb)
            kt = kv_ref[0, pl.ds(off, ch), (h // 2) * LANES:(h // 2 + 1) * LANES]
            v_tile = (nh + h) // 2
            vt = kv_ref[0, pl.ds(off, ch), v_tile * LANES:(v_tile + 1) * LANES]
            s = _dot_t(qs[h], kt)
            parts = []
            for j in range(blocks_per_chunk):
                n = c * blocks_per_chunk + j
                chosen = ((picks[0][r] == n) | (picks[1][r] == n) | (picks[2][r] == n)) & (n < own)
                bias = jnp.where(chosen, 0.0, NEG)
                if with_own:
                    k_pos = n * MOBA_BLOCK + pos_in_block
                    bias = jnp.maximum(bias, jnp.where((own == n) & (k_pos <= tq), 0.0, NEG))
                parts.append(s[:, j * MOBA_BLOCK:(j + 1) * MOBA_BLOCK] + bias)
            sm = parts[0] if len(parts) == 1 else jnp.concatenate(parts, axis=1)
            _online_step(sm, vt, m_scr, l_scr, acc_scr, r)

    c_diag = t0 // ch

    def body(c, carry):
        step(c, False)
        return carry

    lax.fori_loop(0, c_diag, body, 0)
    step(c_diag, True)
    o = acc_scr[...] / l_scr[...]
    for h in range(nh):
        o_ref[0, :, h * LANES:(h + 1) * LANES] = o[h * qb:(h + 1) * qb].astype(o_ref.dtype)


def _moba_call(q, kv, *, qb, q0, ch):
    ns, t, _ = q.shape
    assert ch % MOBA_BLOCK == 0 and ch % qb == 0 and (q0 % qb == 0 or ch == kv.shape[1])
    rows = MOBA_HEADS * qb
    k_tiles = -(-MOBA_HEADS // 2)
    return pl.pallas_call(
        functools.partial(_moba_kernel, qb=qb, q0=q0, ch=ch),
        grid=(ns, t // qb),
        in_specs=[pl.BlockSpec((1, qb, W_QB), lambda b, i: (b, i, 0)),
                  pl.BlockSpec((1,) + kv.shape[1:], lambda b, i: (b, 0, 0))],
        out_specs=pl.BlockSpec((1, qb, W_OMOBA), lambda b, i: (b, i, 0)),
        out_shape=jax.ShapeDtypeStruct((ns, t, W_OMOBA), MXU),
        scratch_shapes=[pltpu.VMEM((LANES, k_tiles * LANES), F32),
                        pltpu.VMEM((rows, 1), F32), pltpu.VMEM((rows, 1), F32), pltpu.VMEM((rows, LANES), F32)],
        compiler_params=_params(("parallel", "arbitrary")),
        name="moba_mixer",
    )(q, kv)


def _dsa_kernel(qc_ref, qi_ref, gw_ref, kv_ref, o_ref, key_scr, cut_scr, m_scr, l_scr, acc_scr,
                *, qb, q0, ch, k_top):
    t0 = q0 + pl.program_id(1) * qb
    tq = t0 + lax.broadcasted_iota(jnp.int32, (qb, 1), 0)
    lpad = kv_ref.shape[1]
    n_ch = t0 // ch + 1
    gw = gw_ref[0]
    w_idx = [gw[:, GW_WI + h:GW_WI + h + 1] for h in range(IDX_HEADS)]
    q_idx = [qi_ref[0, :, h * LANES:(h + 1) * LANES] for h in range(IDX_HEADS)]
    lane_ch = lax.broadcasted_iota(jnp.int32, (1, ch), 1)

    def score_body(c, carry):
        off = pl.multiple_of(c * ch, ch)
        kt = kv_ref[0, pl.ds(off, ch), LANES:2 * LANES]
        sc = w_idx[0] * jnp.maximum(_dot_t(q_idx[0], kt), 0.0)
        for h in range(1, IDX_HEADS):
            sc = sc + w_idx[h] * jnp.maximum(_dot_t(q_idx[h], kt), 0.0)
        sc = jnp.where(sc == 0.0, 0.0, sc)
        bits = lax.bitcast_convert_type(sc, jnp.int32)
        key = bits ^ (lax.shift_right_arithmetic(bits, 31) & 0x7FFFFFFF)
        key_scr[:, pl.ds(off, ch)] = jnp.where(off + lane_ch <= tq, key, INT_MIN)
        return carry

    lax.fori_loop(0, n_ch, score_body, 0)

    def count(pred):
        def body(c, acc):
            off = pl.multiple_of(c * ch, ch)
            hit = jnp.where(pred(key_scr[:, pl.ds(off, ch)], off), 1.0, 0.0)
            part = hit[:, 0:LANES]
            for j in range(1, ch // LANES):
                part = part + hit[:, j * LANES:(j + 1) * LANES]
            return acc + part
        acc = lax.fori_loop(0, n_ch, body, jnp.zeros((qb, LANES), F32))
        return jnp.sum(acc, axis=-1, keepdims=True)

    few = tq + 1 <= k_top

    def bit_cond(st):
        it, _, n_sel = st
        open_rows = jnp.where(few | (n_sel == k_top), 0, 1)
        return (it < 32) & (jnp.max(open_rows) > 0)

    def bit_body(st):
        it, prefix, n_sel = st
        cand = prefix | lax.shift_left(jnp.int32(1), 31 - it)
        cand_key = cand ^ INT_MIN
        cnt = count(lambda k, off: k >= cand_key)
        take = (cnt >= k_top) & (n_sel != k_top)
        return it + 1, jnp.where(take, cand, prefix), jnp.where(take, cnt, n_sel)

    _, prefix, n_sel = lax.while_loop(
        bit_cond, bit_body,
        (jnp.int32(0), jnp.zeros((qb, 1), jnp.int32), jnp.full((qb, 1), float(2 * lpad), F32)))
    thr = jnp.where(few, INT_MIN + 1, jnp.maximum(prefix ^ INT_MIN, INT_MIN + 1))

    overflow = (n_sel > k_top) & jnp.logical_not(few)
    cut_scr[...] = jnp.full((qb, 1), lpad, jnp.int32)

    @pl.when(jnp.max(jnp.where(overflow, 1, 0)) > 0)
    def _():
        need = k_top - count(lambda k, off: k > thr)

        def cut_body(it, lo_hi):
            lo, hi = lo_hi
            mid = (lo + hi) // 2
            cnt = count(lambda k, off: (k == thr) & (off + lane_ch <= mid))
            ok = cnt >= need
            return jnp.where(ok, lo, mid + 1), jnp.where(ok, mid, hi)
        lo, _ = lax.fori_loop(0, max(1, math.ceil(math.log2(lpad))), cut_body,
                              (jnp.zeros((qb, 1), jnp.int32), jnp.full((qb, 1), lpad - 1, jnp.int32)))
        cut_scr[...] = jnp.where(overflow, lo, lpad)

    cut = cut_scr[...]

    q = _stack_heads(qc_ref, range(DSA_HEADS))
    _reset_state(m_scr, l_scr, acc_scr)

    def att_body(c, carry):
        off = pl.multiple_of(c * ch, ch)
        kt = kv_ref[0, pl.ds(off, ch), 0:LANES]
        key = key_scr[:, pl.ds(off, ch)]
        bias = jnp.where((key > thr) | ((key == thr) & (off + lane_ch <= cut)), 0.0, NEG)
        _online_step(_add_bias(_dot_t(q, kt), bias, qb), kt, m_scr, l_scr, acc_scr, slice(None))
        return carry

    lax.fori_loop(0, n_ch, att_body, 0)
    o = acc_scr[...] / l_scr[...]
    for h in range(DSA_HEADS):
        o_ref[0, :, h * LANES:(h + 1) * LANES] = o[h * qb:(h + 1) * qb].astype(o_ref.dtype)


def _dsa_call(qc, qi, gw, kv, *, qb, q0, ch, k_top):
    ns, t, _ = qc.shape
    lpad = kv.shape[1]
    assert ch % qb == 0 and (q0 % qb == 0 or ch == lpad)
    rows = DSA_HEADS * qb
    tok = lambda w: pl.BlockSpec((1, qb, w), lambda b, i: (b, i, 0))
    return pl.pallas_call(
        functools.partial(_dsa_kernel, qb=qb, q0=q0, ch=ch, k_top=k_top),
        grid=(ns, t // qb),
        in_specs=[tok(W_QC), tok(W_QI), tok(W_GW), pl.BlockSpec((1,) + kv.shape[1:], lambda b, i: (b, 0, 0))],
        out_specs=tok(W_ODSA),
        out_shape=jax.ShapeDtypeStruct((ns, t, W_ODSA), MXU),
        scratch_shapes=[pltpu.VMEM((qb, lpad), jnp.int32), pltpu.VMEM((qb, 1), jnp.int32),
                        pltpu.VMEM((rows, 1), F32), pltpu.VMEM((rows, 1), F32), pltpu.VMEM((rows, LANES), F32)],
        compiler_params=_params(("parallel", "arbitrary")),
        name="dsa_mixer",
    )(qc, qi, gw, kv)


def _post_attn_kernel(oa_ref, ob_ref, oc_ref, x_ref, g1_ref, sh_ref, sc_ref, wo_ref, lng_ref, lnb_ref,
                      wr_ref, br_ref, x1_ref, h2_ref, gate_ref):
    x = x_ref[...]
    s, r, d = x.shape
    att = None
    off = 0
    for o_ref in (oa_ref, ob_ref, oc_ref):
        w = o_ref.shape[2]
        part = _dot(o_ref[...].reshape(s * r, w), wo_ref[off:off + w, :])
        att = part if att is None else att + part
        off += w
    att = att.reshape(s, r, d)
    x1 = _ln(ALPHA * x + g1_ref[...] * att) * lng_ref[...] + lnb_ref[...]
    x1_ref[...] = x1
    h2 = _ln(x1) * (1.0 + sc_ref[...]) + sh_ref[...]
    h2b = h2.reshape(s * r, d).astype(MXU)
    h2_ref[...] = h2b.reshape(s, r, d)

    logit = _dot(h2b, wr_ref[...]) + br_ref[...]
    lane = lax.broadcasted_iota(jnp.int32, logit.shape, 1)
    is_g = (lane >= N_EXPERTS) & (lane < N_EXPERTS + N_GROUPS)
    lg = jnp.where(is_g, logit, NEG)
    mg = jnp.max(lg, axis=-1, keepdims=True)
    g_lane = jnp.min(jnp.where(lg == mg, lane, LANES), axis=-1, keepdims=True)
    g_den = jnp.sum(jnp.where(is_g, jnp.exp(lg - mg), 0.0), axis=-1, keepdims=True)
    g_w = 1.0 / g_den
    g_idx = g_lane - N_EXPERTS
    in_grp = (lane >= g_idx * EXPERTS_PER_GROUP) & (lane < (g_idx + 1) * EXPERTS_PER_GROUP)
    le = jnp.where(in_grp, logit, NEG)
    me = jnp.max(le, axis=-1, keepdims=True)
    ee = jnp.where(in_grp, jnp.exp(le - me), 0.0)
    pe = ee / jnp.sum(ee, axis=-1, keepdims=True)
    p1 = jnp.max(pe, axis=-1, keepdims=True)
    i1 = jnp.min(jnp.where((pe == p1) & in_grp, lane, LANES), axis=-1, keepdims=True)
    rest = jnp.where(in_grp & (lane != i1), pe, -1.0)
    p2 = jnp.max(rest, axis=-1, keepdims=True)
    i2 = jnp.min(jnp.where(rest == p2, lane, LANES), axis=-1, keepdims=True)
    tot = p1 + p2
    gate = jnp.where(lane == i1, g_w * p1 / tot, jnp.where(lane == i2, g_w * p2 / tot, 0.0))
    gate_ref[...] = gate.reshape(s, r, LANES)


def _post_attn_call(o_parts, x, g1, sh2, sc2, wo, lng, lnb, wr, br, s_blk, r_blk):
    ns, rt, _ = x.shape
    tok = lambda w: pl.BlockSpec((s_blk, r_blk, w), lambda i, j: (i, j, 0))
    mod = pl.BlockSpec((s_blk, 1, D_MODEL), lambda i, j: (i, 0, 0))
    full = lambda a: pl.BlockSpec(a.shape, lambda i, j: (0,) * a.ndim)
    return pl.pallas_call(
        _post_attn_kernel,
        grid=(ns // s_blk, rt // r_blk),
        in_specs=[tok(o.shape[2]) for o in o_parts] + [tok(D_MODEL), mod, mod, mod,
                                                       full(wo), full(lng), full(lnb), full(wr), full(br)],
        out_specs=[tok(D_MODEL), tok(D_MODEL), tok(LANES)],
        out_shape=[jax.ShapeDtypeStruct((ns, rt, D_MODEL), F32),
                   jax.ShapeDtypeStruct((ns, rt, D_MODEL), MXU),
                   jax.ShapeDtypeStruct((ns, rt, LANES), F32)],
        compiler_params=_params(("parallel", "parallel")),
        name="wout_ln_router",
    )(*o_parts, x, g1, sh2, sc2, wo, lng, lnb, wr, br)


def _moe_kernel(h_ref, gate_ref, x_ref, g2_ref, wi_ref, wo_ref, lng_ref, lnb_ref, out_ref, acc_ref):
    e = pl.program_id(2)
    s, r, d = x_ref.shape

    @pl.when(e == 0)
    def _():
        acc_ref[...] = jnp.zeros_like(acc_ref)

    h = h_ref[...].reshape(s * r, d)
    ab = _dot(h, wi_ref[0])
    a, b = ab[:, :D_EXPERT], ab[:, D_EXPERT:]
    gate = gate_ref[...].reshape(s * r, LANES)
    lane = lax.broadcasted_iota(jnp.int32, gate.shape, 1)
    gcol = jnp.sum(jnp.where(lane == e, gate, 0.0), axis=-1, keepdims=True)
    hid = (a * jax.nn.sigmoid(a) * b * gcol).astype(MXU)
    acc_ref[...] += _dot(hid, wo_ref[0])

    @pl.when(e == N_EXPERTS - 1)
    def _():
        f = acc_ref[...].reshape(s, r, d)
        out_ref[...] = _ln(ALPHA * x_ref[...] + g2_ref[...] * f) * lng_ref[...] + lnb_ref[...]


def _moe_call(h2, gate, x1, g2, wi, wo, lng, lnb, s_blk, r_blk):
    ns, rt, _ = x1.shape
    tok = lambda w: pl.BlockSpec((s_blk, r_blk, w), lambda i, j, e: (i, j, 0))
    mod = pl.BlockSpec((s_blk, 1, D_MODEL), lambda i, j, e: (i, 0, 0))
    vec = pl.BlockSpec((1, 1, D_MODEL), lambda i, j, e: (0, 0, 0))
    return pl.pallas_call(
        _moe_kernel,
        grid=(ns // s_blk, rt // r_blk, N_EXPERTS),
        in_specs=[tok(D_MODEL), tok(LANES), tok(D_MODEL), mod,
                  pl.BlockSpec((1, D_MODEL, 2 * D_EXPERT), lambda i, j, e: (e, 0, 0)),
                  pl.BlockSpec((1, D_EXPERT, D_MODEL), lambda i, j, e: (e, 0, 0)),
                  vec, vec],
        out_specs=tok(D_MODEL),
        out_shape=jax.ShapeDtypeStruct((ns, rt, D_MODEL), F32),
        scratch_shapes=[pltpu.VMEM((s_blk * r_blk, D_MODEL), F32)],
        compiler_params=_params(("parallel", "parallel", "arbitrary")),
        name="moe_ln",
    )(h2, gate, x1, g2, wi, wo, lng, lnb)


def _mixers(pr, cmp_w, past, cfg):
    (nsa, win, moba, dsa, gw, qa, qar, qb_, qc, qi, kc, vc, kvs, winb, mobab, dsab) = pr
    ns, t, _ = qa.shape
    if past is None:
        l_total, lpad, win_pos0 = t, t, 0
        win_out = win[:, -min(WINDOW, t):]
    else:
        cache_nsa, cache_moba, cache_dsa, state_win, table, l = past
        past_len = table.shape[1] * cache_nsa.shape[2]
        l_total = past_len + t
        lpad = -(-l_total // MOBA_BLOCK) * MOBA_BLOCK
        kc, vc, kvs, mobab, dsab = _assemble_call(table, cache_nsa, cache_moba, cache_dsa, l,
                                                 nsa, moba, dsa, lpad)
        w_buf = state_win.shape[2]
        rows_pad = -(-(w_buf + t) // LANES) * LANES
        winb, win_out = _win_call(state_win, l, win, rows_pad)
        win_pos0 = past_len - w_buf
    q0 = l_total - t
    chunk_rows = LANES * CMP_STRIDE
    nch = lpad // CMP_STRIDE
    ncp = -(-nch // (2 * LANES)) * 2 * LANES
    kcmp, vcmp = _cmp_call(kc.reshape(ns, nch, chunk_rows), vc.reshape(ns, nch, chunk_rows), *cmp_w, ncp)
    ovl, expand = _nsa_tables(l_total, ncp, lpad)
    qb = cfg['qb']
    o_nsa = _nsa_call(qa, qar, gw, kcmp, vcmp, kvs, winb, ovl, expand,
                      qb=qb, q0=q0, ch=cfg['ch_nsa'], win_pos0=win_pos0)
    o_moba = _moba_call(qb_, mobab, qb=qb, q0=q0, ch=cfg['ch_moba'])
    o_dsa = _dsa_call(qc, qi, gw, dsab, qb=qb, q0=q0, ch=cfg['ch_dsa'], k_top=min(DSA_TOPK, l_total // 4))
    return (o_nsa, o_moba, o_dsa), win_out


def _layer(x, mod, wts, tabs, cmp_w, past, cfg):
    sh1, sc1, g1, sh2, sc2, g2 = mod
    w_perm, wo, ln1g, ln1b, ln2g, ln2b, wr, br, wei, weo = wts
    pr = _proj_call(x, sh1, sc1, w_perm, tabs, *cfg['blk_a'])
    o_parts, win_out = _mixers(pr, cmp_w, past, cfg)
    x1, h2, gate = _post_attn_call(o_parts, x, g1, sh2, sc2, wo, ln1g, ln1b, wr, br, *cfg['blk_a'])
    y = _moe_call(h2, gate, x1, g2, wei, weo, ln2g, ln2b, *cfg['blk_m'])
    return y, pr[0], pr[2], pr[3], win_out


def _layer_weights(p, l):
    n_pad = LANES - N_EXPERTS - N_GROUPS
    wr = jnp.concatenate([p['w_router_expert'][l], p['w_router_group'][l],
                          jnp.zeros((D_MODEL, n_pad), F32)], axis=1).astype(MXU)
    br = jnp.concatenate([p['b_router_expert'][l], p['b_router_group'][l], jnp.zeros((n_pad,), F32)])[None, :]
    vec = lambda a: a[l].reshape(1, 1, D_MODEL)
    return (_permute_w_in(p['w_in'][l]), _pad_w_out(p['w_out'][l]), vec(p['ln1_g']), vec(p['ln1_b']),
            vec(p['ln2_g']), vec(p['ln2_b']), wr, br,
            p['w_expert_in'][l].astype(MXU), p['w_expert_out'][l].astype(MXU))


def kernel(x_prompt, x_sample, c_prompt, c_sample, cache_nsa, cache_moba, cache_dsa, state_win, page_table, w_ada, b_ada, w_in, cmp_pe, cmp_w1, cmp_w2, w_out, ln1_g, ln1_b, ln2_g, ln2_b, w_router_group, b_router_group, w_router_expert, b_router_expert, w_expert_in, w_expert_out):
    p = dict(w_in=w_in, w_out=w_out, ln1_g=ln1_g, ln1_b=ln1_b, ln2_g=ln2_g, ln2_b=ln2_b,
             w_router_group=w_router_group, b_router_group=b_router_group,
             w_router_expert=w_router_expert, b_router_expert=b_router_expert,
             w_expert_in=w_expert_in, w_expert_out=w_expert_out)
    nb_p, t_p, _ = x_prompt.shape
    nb_s, t_s, _ = x_sample.shape
    past_len = page_table.shape[1] * cache_nsa.shape[2]
    lpad_s = -(-(past_len + t_s) // MOBA_BLOCK) * MOBA_BLOCK

    n_c = nb_p + nb_s
    n_c_pad = -(-n_c // 8) * 8
    c_all = jnp.concatenate([c_prompt, c_sample, jnp.zeros((n_c_pad - n_c, D_MODEL), F32)], axis=0)
    mod_all = _ada_call(c_all, w_ada, b_ada)

    tabs_p = _rope_tables(jnp.arange(t_p))
    tabs_s = _rope_tables(past_len + jnp.arange(t_s))
    cfg_p = dict(blk_a=(1, 512), blk_m=(1, 1024), qb=128, ch_nsa=512, ch_moba=MOBA_BLOCK, ch_dsa=512)
    cfg_s = dict(blk_a=(32, t_s), blk_m=(nb_s, t_s), qb=t_s, ch_nsa=lpad_s, ch_moba=lpad_s, ch_dsa=lpad_s)

    y_p, y_s = x_prompt, x_sample
    st_p, st_s = [], []
    for l in range(DEPTH):
        mod_l = mod_all[l]
        split = lambda m: tuple(m[:, None, k * D_MODEL:(k + 1) * D_MODEL] for k in range(6))
        mod_p = split(mod_l[:nb_p])
        mod_s = split(mod_l[nb_p:n_c])
        wts = _layer_weights(p, l)
        cmp_w = _cmp_weights(cmp_pe[l], cmp_w1[l], cmp_w2[l])
        y_p, *sp = _layer(y_p, mod_p, wts, tabs_p, cmp_w, None, cfg_p)
        past = (cache_nsa, cache_moba, cache_dsa, state_win, page_table, l)
        y_s, *ss = _layer(y_s, mod_s, wts, tabs_s, cmp_w, past, cfg_s)
        st_p.append(sp)
        st_s.append(ss)

    def stack(st, k, tail):
        a = jnp.stack([s[k] for s in st])
        return a.reshape(a.shape[:3] + tail)

    nsa_t = (4, NSA_KV_HEADS, HEAD_DIM)
    moba_t = (2, MOBA_HEADS, HEAD_DIM)
    dsa_t = (3, HEAD_DIM)
    win_t = (2, NSA_KV_HEADS, HEAD_DIM)
    return (y_p, y_s, stack(st_p, 0, nsa_t), stack(st_s, 0, nsa_t), stack(st_p, 1, moba_t),
            stack(st_s, 1, moba_t), stack(st_p, 2, dsa_t), stack(st_s, 2, dsa_t),
            stack(st_p, 3, win_t), stack(st_s, 3, win_t))
```

```python
import functools
import math

import numpy as np
import jax
import jax.numpy as jnp
from jax import lax
from jax.experimental import pallas as pl
from jax.experimental.pallas import tpu as pltpu

D_MODEL = 1024
DEPTH = 2
HEAD_DIM = 64
NSA_HEADS = 6
NSA_KV_HEADS = 2
NSA_GROUP = NSA_HEADS // NSA_KV_HEADS
MOBA_HEADS = 5
DSA_HEADS = 5
D_MIX = (NSA_HEADS + MOBA_HEADS + DSA_HEADS) * HEAD_DIM
CMP_LEN = 32
CMP_STRIDE = 16
CMP_HIDDEN = 256
SEL_BLOCK = 64
SEL_TOPN = 16
WINDOW = 512
MOBA_BLOCK = 256
MOBA_TOPK = 3
IDX_HEADS = 4
DSA_TOPK = 256
ROPE_THETA = 500000.0
ROT_DIM = HEAD_DIM // 4
N_GROUPS = 4
EXPERTS_PER_GROUP = 8
N_EXPERTS = N_GROUPS * EXPERTS_PER_GROUP
D_EXPERT = 256
ALPHA = (2 * DEPTH) ** 0.25
LN_EPS = 1e-5
NEG = -1e30
BIG = 1e30
BELOW_NEG = -3e38
M_INIT = -1e29
INT_MIN = -2 ** 31

LANES = 128
F32 = jnp.float32
BF16 = jnp.bfloat16
MXU = BF16
Q_SCALE = HEAD_DIM ** -0.5

W_QA = NSA_HEADS * LANES
W_KVA = 6 * NSA_KV_HEADS * HEAD_DIM
W_NSA = 4 * NSA_KV_HEADS * HEAD_DIM
W_WIN = 2 * NSA_KV_HEADS * HEAD_DIM
W_QB = MOBA_HEADS * LANES
W_QC = DSA_HEADS * LANES
W_MOBA = 2 * MOBA_HEADS * HEAD_DIM
W_DSA = 3 * HEAD_DIM
W_DSA_PAD = 2 * LANES
W_QI = IDX_HEADS * LANES
W_GW = LANES
OFF_QA = 0
OFF_KVA = OFF_QA + W_QA
OFF_QB = OFF_KVA + W_KVA
OFF_QC = OFF_QB + W_QB
OFF_MOBA = OFF_QC + W_QC
OFF_DSA = OFF_MOBA + W_MOBA
OFF_QI = OFF_DSA + W_DSA_PAD
OFF_GW = OFF_QI + W_QI
W_PROJ = OFF_GW + W_GW
GW_WI = 3 * NSA_HEADS

W_ONSA = NSA_HEADS * LANES
W_OMOBA = MOBA_HEADS * LANES
W_ODSA = DSA_HEADS * LANES
W_OALL = W_ONSA + W_OMOBA + W_ODSA

VMEM_LIMIT = 56 * 1024 * 1024


def _params(sem):
    return pltpu.CompilerParams(dimension_semantics=sem, vmem_limit_bytes=VMEM_LIMIT)


def _ln(x):
    mu = jnp.mean(x, axis=-1, keepdims=True)
    xc = x - mu
    var = jnp.mean(xc * xc, axis=-1, keepdims=True)
    return xc * lax.rsqrt(var + LN_EPS)


def _dot(a, b):
    return jnp.dot(a, b, preferred_element_type=F32)


def _dot_t(a, b):
    return lax.dot_general(a, b, (((1,), (1,)), ((), ())), preferred_element_type=F32)


def _ada_kernel(c_ref, w_ref, b_ref, o_ref):
    c = c_ref[...]
    a = (c * jax.nn.sigmoid(c)).astype(MXU)
    o_ref[0] = _dot(a, w_ref[0].astype(MXU)) + b_ref[0]


def _ada_call(c_all, w_ada, b_ada):
    n = c_all.shape[0]
    tn = 1536
    return pl.pallas_call(
        _ada_kernel,
        grid=(DEPTH, 6 * D_MODEL // tn),
        in_specs=[
            pl.BlockSpec((n, D_MODEL), lambda l, j: (0, 0)),
            pl.BlockSpec((1, D_MODEL, tn), lambda l, j: (l, 0, j)),
            pl.BlockSpec((1, 1, tn), lambda l, j: (l, 0, j)),
        ],
        out_specs=pl.BlockSpec((1, n, tn), lambda l, j: (l, 0, j)),
        out_shape=jax.ShapeDtypeStruct((DEPTH, n, 6 * D_MODEL), F32),
        compiler_params=_params(("arbitrary", "arbitrary")),
        name="ada_mod",
    )(c_all, w_ada, b_ada.reshape(DEPTH, 1, 6 * D_MODEL))


def _rope_tile(t, cos, s_lo, s_hi, first_half_only):
    if first_half_only:
        lane = lax.broadcasted_iota(jnp.int32, cos.shape, 2)
        keep = lane < HEAD_DIM
        cos = jnp.where(keep, cos, 1.0)
        s_lo = jnp.where(keep, s_lo, 0.0)
        s_hi = jnp.where(keep, s_hi, 0.0)
    half = ROT_DIM // 2
    up = pltpu.roll(t, LANES - half, axis=2)
    dn = pltpu.roll(t, half, axis=2)
    return t * cos + up * s_lo + dn * s_hi


def _rope_slab(p, cos, s_lo, s_hi, modes):
    tiles = []
    for j, m in enumerate(modes):
        t = p[:, :, j * LANES:(j + 1) * LANES]
        if m != 'n':
            t = _rope_tile(t, cos, s_lo, s_hi, m == 'h')
        tiles.append(t)
    return tiles[0] if len(tiles) == 1 else jnp.concatenate(tiles, axis=2)


PROJ_F32_WIDTHS = (W_NSA, W_WIN, W_MOBA, W_DSA, W_GW)
PROJ_MXU_WIDTHS = (W_QA, W_QA, W_QB, W_QC, W_QI, LANES, LANES, 2 * LANES, W_WIN, W_MOBA, W_DSA_PAD)


def _proj_kernel(x_ref, sh_ref, sc_ref, w_ref, cos_ref, slo_ref, shi_ref,
                 nsa_ref, win_ref, moba_ref, dsa_ref, gw_ref,
                 qa_ref, qar_ref, qb_ref, qc_ref, qi_ref,
                 kc_ref, vc_ref, kvs_ref, winb_ref, mobab_ref, dsab_ref):
    x = x_ref[...]
    s, r, d = x.shape
    h = _ln(x) * (1.0 + sc_ref[...]) + sh_ref[...]
    hb = h.reshape(s * r, d).astype(MXU)
    rope = functools.partial(_rope_slab, cos=cos_ref[...], s_lo=slo_ref[...], s_hi=shi_ref[...])

    def seg(off, width):
        return _dot(hb, w_ref[:, off:off + width]).reshape(s, r, width)

    qa = seg(OFF_QA, W_QA)
    qa_ref[...] = qa.astype(MXU)
    qar_ref[...] = rope(qa, modes='f' * NSA_HEADS).astype(MXU)
    kva = seg(OFF_KVA, W_KVA)
    nsa = rope(kva[:, :, :W_NSA], modes='nnfn')
    nsa_ref[...] = nsa
    kc_ref[...] = nsa[:, :, 0:LANES].astype(MXU)
    vc_ref[...] = nsa[:, :, LANES:2 * LANES].astype(MXU)
    kvs_ref[...] = nsa[:, :, 2 * LANES:].astype(MXU)
    win = rope(kva[:, :, W_NSA:], modes='fn')
    win_ref[...] = win
    winb_ref[...] = win.astype(MXU)
    qb_ref[...] = rope(seg(OFF_QB, W_QB), modes='f' * MOBA_HEADS).astype(MXU)
    qc_ref[...] = rope(seg(OFF_QC, W_QC), modes='f' * DSA_HEADS).astype(MXU)
    moba = rope(seg(OFF_MOBA, W_MOBA), modes='ffhnn')
    moba_ref[...] = moba
    mobab_ref[...] = moba.astype(MXU)
    dsa = rope(seg(OFF_DSA, W_DSA_PAD), modes='hh')
    dsa_ref[...] = dsa[:, :, :W_DSA]
    dsab_ref[...] = dsa.astype(MXU)
    qi_ref[...] = rope(seg(OFF_QI, W_QI), modes='f' * IDX_HEADS).astype(MXU)
    gw_ref[...] = seg(OFF_GW, W_GW)


def _proj_call(x, sh, sc, w_perm, tabs, s_blk, r_blk):
    ns, rt, _ = x.shape
    cos, s_lo, s_hi = tabs
    tok = lambda w: pl.BlockSpec((s_blk, r_blk, w), lambda i, j: (i, j, 0))
    mod = pl.BlockSpec((s_blk, 1, D_MODEL), lambda i, j: (i, 0, 0))
    tab = pl.BlockSpec((1, r_blk, LANES), lambda i, j: (0, j, 0))
    widths = PROJ_F32_WIDTHS + PROJ_MXU_WIDTHS
    dtypes = (F32,) * len(PROJ_F32_WIDTHS) + (MXU,) * len(PROJ_MXU_WIDTHS)
    return pl.pallas_call(
        _proj_kernel,
        grid=(ns // s_blk, rt // r_blk),
        in_specs=[tok(D_MODEL), mod, mod,
                  pl.BlockSpec((D_MODEL, W_PROJ), lambda i, j: (0, 0)),
                  tab, tab, tab],
        out_specs=[tok(w) for w in widths],
        out_shape=[jax.ShapeDtypeStruct((ns, rt, w), dt) for w, dt in zip(widths, dtypes)],
        compiler_params=_params(("parallel", "parallel")),
        name="ln_proj_rope",
    )(x, sh, sc, w_perm, cos, s_lo, s_hi)


def _pad_heads(w, halves, scale):
    z = jnp.zeros((w.shape[0], HEAD_DIM), w.dtype)
    cols = []
    for h, half in enumerate(halves):
        wh = w[:, h * HEAD_DIM:(h + 1) * HEAD_DIM] * scale
        cols += [wh, z] if half == 0 else [z, wh]
    return jnp.concatenate(cols, axis=1)


def _permute_w_in(w_in_l):
    sizes = (NSA_HEADS * HEAD_DIM, 6 * NSA_KV_HEADS * HEAD_DIM, 3 * NSA_HEADS,
             MOBA_HEADS * HEAD_DIM, MOBA_HEADS * HEAD_DIM, MOBA_HEADS * HEAD_DIM,
             DSA_HEADS * HEAD_DIM, 2 * HEAD_DIM, IDX_HEADS * HEAD_DIM, HEAD_DIM, IDX_HEADS)
    offs = np.concatenate([[0], np.cumsum(sizes)])
    part = lambda k: w_in_l[:, int(offs[k]):int(offs[k + 1])]
    zeros = lambda n: jnp.zeros((D_MODEL, n), w_in_l.dtype)
    cols = [_pad_heads(part(0), [h // NSA_GROUP for h in range(NSA_HEADS)], Q_SCALE),
            part(1),
            _pad_heads(part(3), [h % 2 for h in range(MOBA_HEADS)], Q_SCALE),
            _pad_heads(part(6), [0] * DSA_HEADS, Q_SCALE),
            part(4), part(5),
            part(7), part(9), zeros(W_DSA_PAD - W_DSA),
            _pad_heads(part(8), [0] * IDX_HEADS, 1.0),
            part(2), part(10), zeros(W_GW - 3 * NSA_HEADS - IDX_HEADS)]
    return jnp.concatenate(cols, axis=1).astype(MXU)


def _pad_w_out(w_out_l):
    z = jnp.zeros((HEAD_DIM, D_MODEL), w_out_l.dtype)
    rows = []
    head = lambda h: w_out_l[h * HEAD_DIM:(h + 1) * HEAD_DIM]
    for h in range(NSA_HEADS):
        rows += [head(h), z] if h // NSA_GROUP == 0 else [z, head(h)]
    for h in range(MOBA_HEADS):
        rows += [head(NSA_HEADS + h), z] if (MOBA_HEADS + h) % 2 == 0 else [z, head(NSA_HEADS + h)]
    for h in range(DSA_HEADS):
        rows += [z, head(NSA_HEADS + MOBA_HEADS + h)]
    return jnp.concatenate(rows, axis=0).astype(MXU)


def _rope_tables(pos):
    half = ROT_DIM // 2
    inv = ROPE_THETA ** (-jnp.arange(0, ROT_DIM, 2, dtype=F32) / ROT_DIM)
    ang = pos.astype(F32)[:, None] * inv[None, :]
    cos, sin = jnp.cos(ang), jnp.sin(ang)
    t = pos.shape[0]
    ones = jnp.ones((t, HEAD_DIM - ROT_DIM), F32)
    zeros = jnp.zeros((t, HEAD_DIM - ROT_DIM), F32)
    zh = jnp.zeros((t, half), F32)
    c = jnp.concatenate([cos, cos, ones], axis=1)
    s_lo = jnp.concatenate([-sin, zh, zeros], axis=1)
    s_hi = jnp.concatenate([zh, sin, zeros], axis=1)
    rep = lambda a: jnp.concatenate([a, a], axis=1)[None]
    return rep(c), rep(s_lo), rep(s_hi)


def _assemble_kernel(tbl_ref, nsa_ref, moba_ref, dsa_ref, nsa_new_ref, moba_new_ref, dsa_new_ref,
                     kc_ref, vc_ref, kvs_ref, mob_ref, dsao_ref, *, n_pages):
    p = pl.program_id(1)
    rows = kc_ref.shape[1]

    def emit(nsa, moba, dsa):
        kc_ref[0] = nsa[:, 0:LANES].astype(MXU)
        vc_ref[0] = nsa[:, LANES:2 * LANES].astype(MXU)
        kvs_ref[0] = nsa[:, 2 * LANES:].astype(MXU)
        mob_ref[0] = moba.astype(MXU)
        dsao_ref[0] = jnp.concatenate([dsa, jnp.zeros((rows, W_DSA_PAD - W_DSA), F32)], axis=1).astype(MXU)

    @pl.when(p < n_pages)
    def _():
        emit(nsa_ref[0, 0], moba_ref[0, 0], dsa_ref[0, 0])

    @pl.when(p == n_pages)
    def _():
        t_new = nsa_new_ref.shape[1]
        tail = lambda new: jnp.concatenate([new, jnp.zeros((rows - t_new, new.shape[1]), F32)], axis=0)
        emit(tail(nsa_new_ref[0]), tail(moba_new_ref[0]), tail(dsa_new_ref[0]))

    @pl.when(p > n_pages)
    def _():
        emit(jnp.zeros((rows, W_NSA), F32), jnp.zeros((rows, W_MOBA), F32), jnp.zeros((rows, W_DSA), F32))


def _assemble_call(table, cache_nsa, cache_moba, cache_dsa, l, nsa_new, moba_new, dsa_new, lpad):
    ns, n_pages = table.shape
    page = cache_nsa.shape[2]
    n_blk = lpad // page
    pool = lambda w: pl.BlockSpec(
        (1, 1, page, w), lambda b, p, tbl: (l, tbl[b, jnp.minimum(p, n_pages - 1)], 0, 0))
    new = lambda w: pl.BlockSpec((1, nsa_new.shape[1], w), lambda b, p, tbl: (b, 0, 0))
    out = lambda w: pl.BlockSpec((1, page, w), lambda b, p, tbl: (b, p, 0))
    widths = (LANES, LANES, 2 * LANES, W_MOBA, W_DSA_PAD)
    return pl.pallas_call(
        functools.partial(_assemble_kernel, n_pages=n_pages),
        grid_spec=pltpu.PrefetchScalarGridSpec(
            num_scalar_prefetch=1,
            grid=(ns, n_blk),
            in_specs=[pool(W_NSA), pool(W_MOBA), pool(W_DSA), new(W_NSA), new(W_MOBA), new(W_DSA)],
            out_specs=[out(w) for w in widths],
        ),
        out_shape=[jax.ShapeDtypeStruct((ns, lpad, w), MXU) for w in widths],
        compiler_params=_params(("parallel", "arbitrary")),
        name="assemble_pages",
    )(table,
      cache_nsa.reshape(cache_nsa.shape[:3] + (W_NSA,)),
      cache_moba.reshape(cache_moba.shape[:3] + (W_MOBA,)),
      cache_dsa.reshape(cache_dsa.shape[:3] + (W_DSA,)),
      nsa_new, moba_new, dsa_new)


def _win_kernel(buf_ref, new_ref, winb_ref, wout_ref):
    buf = buf_ref[0, 0]
    new = new_ref[0]
    pad = winb_ref.shape[1] - buf.shape[0] - new.shape[0]
    winb_ref[0] = jnp.concatenate([buf, new, jnp.zeros((pad, buf.shape[1]), F32)], axis=0).astype(MXU)
    wout_ref[0] = jnp.concatenate([buf[new.shape[0]:], new], axis=0)


def _win_call(state_win, l, win_new, rows_pad):
    ns, w_buf = state_win.shape[1], state_win.shape[2]
    t_new = win_new.shape[1]
    return pl.pallas_call(
        _win_kernel,
        grid=(ns,),
        in_specs=[pl.BlockSpec((1, 1, w_buf, W_WIN), lambda b: (l, b, 0, 0)),
                  pl.BlockSpec((1, t_new, W_WIN), lambda b: (b, 0, 0))],
        out_specs=[pl.BlockSpec((1, rows_pad, W_WIN), lambda b: (b, 0, 0)),
                   pl.BlockSpec((1, w_buf, W_WIN), lambda b: (b, 0, 0))],
        out_shape=[jax.ShapeDtypeStruct((ns, rows_pad, W_WIN), MXU),
                   jax.ShapeDtypeStruct((ns, w_buf, W_WIN), F32)],
        compiler_params=_params(("parallel",)),
        name="window_buffer",
    )(state_win.reshape(state_win.shape[:3] + (W_WIN,)), win_new)


def _cmp_kernel(kx_ref, vx_ref, pe_ref, w1_ref, w2_ref, kc_ref, vc_ref):
    nch = kx_ref.shape[1]
    nout = kc_ref.shape[1]
    for j, (x_ref, o_ref) in enumerate(((kx_ref, kc_ref), (vx_ref, vc_ref))):
        x = x_ref[0].astype(F32)
        a = _dot((x + pe_ref[j, 0]).astype(MXU), w1_ref[j, 0])
        b = _dot((x + pe_ref[j, 1]).astype(MXU), w1_ref[j, 1])
        hid = a + pltpu.roll(b, nch - 1, axis=0)
        out = _dot(jax.nn.gelu(hid).astype(MXU), w2_ref[j])
        o_ref[0, 0:nch] = out.astype(o_ref.dtype)
        if nout > nch:
            o_ref[0, nch:nout] = jnp.zeros((nout - nch, LANES), o_ref.dtype)


def _cmp_call(kx, vx, pe_e, w1_e, w2_e, nout):
    ns, nch, wx = kx.shape
    full = lambda a: pl.BlockSpec(a.shape, lambda b: (0,) * a.ndim)
    return pl.pallas_call(
        _cmp_kernel,
        grid=(ns,),
        in_specs=[pl.BlockSpec((1, nch, wx), lambda b: (b, 0, 0))] * 2 + [full(pe_e), full(w1_e), full(w2_e)],
        out_specs=[pl.BlockSpec((1, nout, LANES), lambda b: (b, 0, 0))] * 2,
        out_shape=[jax.ShapeDtypeStruct((ns, nout, LANES), MXU)] * 2,
        compiler_params=_params(("parallel",)),
        name="nsa_compress",
    )(kx, vx, pe_e, w1_e, w2_e)


def _cmp_weights(cmp_pe_l, cmp_w1_l, cmp_w2_l):
    half_rows = CMP_LEN // 2
    kv = NSA_KV_HEADS
    w1 = cmp_w1_l.reshape(2, 2, half_rows, HEAD_DIM, CMP_HIDDEN)
    w1_e = jnp.zeros((2, 2, half_rows, kv, HEAD_DIM, kv, CMP_HIDDEN), F32)
    for k in range(kv):
        w1_e = w1_e.at[:, :, :, k, :, k, :].set(w1)
    w1_e = w1_e.reshape(2, 2, half_rows * kv * HEAD_DIM, kv * CMP_HIDDEN).astype(MXU)
    w2_e = jnp.zeros((2, kv, CMP_HIDDEN, kv, HEAD_DIM), F32)
    for k in range(kv):
        w2_e = w2_e.at[:, k, :, k, :].set(cmp_w2_l)
    w2_e = w2_e.reshape(2, kv * CMP_HIDDEN, kv * HEAD_DIM).astype(MXU)
    pe = cmp_pe_l.reshape(2, 2, half_rows, 1, HEAD_DIM)
    pe_e = jnp.broadcast_to(pe, (2, 2, half_rows, kv, HEAD_DIM)).reshape(2, 2, 1, half_rows * kv * HEAD_DIM)
    return pe_e, w1_e, w2_e


def _online_step(sm, vt, m_scr, l_scr, acc_scr, rows):
    m_prev = m_scr[rows, :]
    m_new = jnp.maximum(m_prev, jnp.max(sm, axis=-1, keepdims=True))
    alpha = jnp.exp(m_prev - m_new)
    p = jnp.exp(sm - m_new)
    l_scr[rows, :] = alpha * l_scr[rows, :] + jnp.sum(p, axis=-1, keepdims=True)
    acc_scr[rows, :] = alpha * acc_scr[rows, :] + _dot(p.astype(MXU), vt)
    m_scr[rows, :] = m_new


def _reset_state(m_scr, l_scr, acc_scr):
    m_scr[...] = jnp.full(m_scr.shape, M_INIT, F32)
    l_scr[...] = jnp.zeros(l_scr.shape, F32)
    acc_scr[...] = jnp.zeros(acc_scr.shape, F32)


def _biased_softmax(sm):
    m = jnp.maximum(jnp.max(sm, axis=-1, keepdims=True), M_INIT)
    e = jnp.exp(sm - m)
    den = jnp.sum(e, axis=-1, keepdims=True)
    return e * jnp.where(den > 0.0, 1.0 / den, 0.0)


def _add_bias(s, bias, qb):
    c = bias.shape[1]
    outer = bias.shape[0] // qb
    groups = s.shape[0] // bias.shape[0]
    return (s.reshape(outer, groups, qb, c) + bias.reshape(outer, 1, qb, c)).reshape(s.shape)


def _stack_heads(ref, tiles):
    parts = [ref[0, :, t * LANES:(t + 1) * LANES].astype(F32) for t in tiles]
    return jnp.concatenate(parts, axis=0).astype(MXU)


def _top_blocks(imp, t0, qb):
    qp = max(qb, LANES)
    kv = imp.shape[0] // qb
    blk = lax.broadcasted_iota(jnp.int32, (LANES, 1), 0)
    tq = t0 + lax.broadcasted_iota(jnp.int32, (1, qp), 1)
    tb = tq // SEL_BLOCK
    forced = (blk == 0) | (blk == tb) | (blk == tb - 1)
    admissible = blk * SEL_BLOCK <= tq
    vals = []
    for k in range(kv):
        part = imp[k * qb:(k + 1) * qb]
        if qp > qb:
            part = jnp.concatenate([part, jnp.zeros((qp - qb, LANES), F32)], axis=0)
        vals.append(jnp.where(forced, BIG, jnp.where(admissible, part.T, NEG)))
    val = jnp.concatenate(vals, axis=1)
    sel = jnp.zeros(val.shape, F32)
    for _ in range(SEL_TOPN):
        top = jnp.max(val, axis=0, keepdims=True)
        idx = jnp.min(jnp.where(val == top, blk, LANES), axis=0, keepdims=True)
        pick = blk == idx
        sel = jnp.where(pick, 1.0, sel)
        val = jnp.where(pick, BELOW_NEG, val)
    return jnp.concatenate([sel[:, k * qp:(k + 1) * qp].T[:qb] for k in range(kv)], axis=0)


def _nsa_kernel(qa_ref, qar_ref, gw_ref, kc_ref, vc_ref, kvs_ref, win_ref, ovl_ref, exp_ref, o_ref,
                m_scr, l_scr, acc_scr, *, qb, q0, ch, win_pos0, slab):
    t0 = q0 + pl.program_id(1) * qb
    tq = t0 + lax.broadcasted_iota(jnp.int32, (qb, 1), 0)
    ncp = kc_ref.shape[1]
    win_rows = win_ref.shape[1]
    g, kv = NSA_GROUP, NSA_KV_HEADS
    qa = _stack_heads(qa_ref, range(NSA_HEADS))
    qr = _stack_heads(qar_ref, range(NSA_HEADS))

    c_end = lax.broadcasted_iota(jnp.int32, (1, ncp), 1) * CMP_STRIDE + (CMP_LEN - 1)
    bias_c = jnp.where(c_end <= tq, 0.0, NEG)
    p_c = _biased_softmax(_add_bias(_dot_t(qa, kc_ref[0]), bias_c, qb))
    o_cmp = _dot(p_c.astype(MXU), vc_ref[0])
    p4 = p_c.reshape(kv, g, qb, ncp)
    p_sum = p4[:, 0]
    for j in range(1, g):
        p_sum = p_sum + p4[:, j]
    p_sum = p_sum.reshape(kv * qb, ncp)
    p_hi = p_sum.astype(MXU)
    p_lo = (p_sum - p_hi.astype(F32)).astype(MXU)
    imp = _dot(p_hi, ovl_ref[...]) + _dot(p_lo, ovl_ref[...])
    sel_b = _top_blocks(imp, t0, qb).astype(MXU)

    _reset_state(m_scr, l_scr, acc_scr)

    def sel_step(c, causal):
        off = pl.multiple_of(c * ch, ch)
        kt = kvs_ref[0, pl.ds(off, ch), 0:LANES]
        vt = kvs_ref[0, pl.ds(off, ch), LANES:2 * LANES]
        bias = (_dot(sel_b, exp_ref[:, pl.ds(off, ch)]) - 1.0) * BIG
        if causal:
            k_pos = off + lax.broadcasted_iota(jnp.int32, (1, ch), 1)
            bias = _add_bias(bias, jnp.where(k_pos <= tq, 0.0, NEG), qb)
        _online_step(_add_bias(_dot_t(qr, kt), bias, qb), vt, m_scr, l_scr, acc_scr, slice(None))

    c_diag = t0 // ch

    def sel_body(c, carry):
        sel_step(c, False)
        return carry

    lax.fori_loop(0, c_diag, sel_body, 0)
    sel_step(c_diag, True)
    o_sel = acc_scr[...] / l_scr[...]

    if win_rows > slab:
        w_start = pl.multiple_of(jnp.clip(t0 - WINDOW - win_pos0, 0, win_rows - slab), 16)
    else:
        w_start = 0
    w_pos = win_pos0 + w_start + lax.broadcasted_iota(jnp.int32, (1, slab), 1)
    dist = tq - w_pos
    bias_w = jnp.where((dist >= 0) & (dist <= WINDOW), 0.0, NEG)
    kt = win_ref[0, pl.ds(w_start, slab), 0:LANES]
    vt = win_ref[0, pl.ds(w_start, slab), LANES:2 * LANES]
    p_w = _biased_softmax(_add_bias(_dot_t(qr, kt), bias_w, qb))
    o_win = _dot(p_w.astype(MXU), vt)

    gate = jax.nn.sigmoid(gw_ref[0])
    for h in range(NSA_HEADS):
        r = slice(h * qb, (h + 1) * qb)
        o = (gate[:, 3 * h:3 * h + 1] * o_cmp[r] + gate[:, 3 * h + 1:3 * h + 2] * o_sel[r]
             + gate[:, 3 * h + 2:3 * h + 3] * o_win[r])
        o_ref[0, :, h * LANES:(h + 1) * LANES] = o.astype(o_ref.dtype)


def _nsa_call(qa, qar, gw, kc, vc, kvs, winb, ovl, expand, *, qb, q0, ch, win_pos0):
    ns, t, _ = qa.shape
    assert ch % qb == 0 and (q0 % qb == 0 or ch == kvs.shape[1])
    slab = min(-(-(WINDOW + qb) // LANES) * LANES, winb.shape[1])
    tok = lambda w: pl.BlockSpec((1, qb, w), lambda b, i: (b, i, 0))
    seq = lambda a: pl.BlockSpec((1,) + a.shape[1:], lambda b, i: (b, 0, 0))
    full = lambda a: pl.BlockSpec(a.shape, lambda b, i: (0,) * a.ndim)
    rows = NSA_HEADS * qb
    return pl.pallas_call(
        functools.partial(_nsa_kernel, qb=qb, q0=q0, ch=ch, win_pos0=win_pos0, slab=slab),
        grid=(ns, t // qb),
        in_specs=[tok(W_QA), tok(W_QA), tok(W_GW), seq(kc), seq(vc), seq(kvs), seq(winb), full(ovl), full(expand)],
        out_specs=tok(W_ONSA),
        out_shape=jax.ShapeDtypeStruct((ns, t, W_ONSA), MXU),
        scratch_shapes=[pltpu.VMEM((rows, 1), F32), pltpu.VMEM((rows, 1), F32), pltpu.VMEM((rows, LANES), F32)],
        compiler_params=_params(("parallel", "arbitrary")),
        name="nsa_mixer",
    )(qa, qar, gw, kc, vc, kvs, winb, ovl, expand)


def _nsa_tables(l_total, ncp, lpad):
    n_cmp = (l_total - CMP_LEN) // CMP_STRIDE + 1
    c = np.arange(ncp)[:, None]
    n = np.arange(LANES)[None, :]
    ovl = ((c * CMP_STRIDE < n * SEL_BLOCK + SEL_BLOCK) & (c * CMP_STRIDE + CMP_LEN - 1 >= n * SEL_BLOCK)
           & (c < n_cmp))
    s = np.arange(lpad)[None, :]
    expand = (s // SEL_BLOCK) == np.arange(LANES)[:, None]
    return jnp.asarray(ovl, MXU), jnp.asarray(expand, MXU)


def _moba_kernel(q_ref, kv_ref, o_ref, kmean_scr, m_scr, l_scr, acc_scr, *, qb, q0, ch):
    i = pl.program_id(1)
    lpad = kv_ref.shape[1]
    nh = MOBA_HEADS
    k_tiles = -(-nh // 2)
    blocks_per_chunk = ch // MOBA_BLOCK

    @pl.when(i == 0)
    def _():
        kmean_scr[...] = jnp.zeros(kmean_scr.shape, F32)
        ones = jnp.ones((8, MOBA_BLOCK), MXU)

        def mean_body(n, carry):
            off = pl.multiple_of(n * MOBA_BLOCK, MOBA_BLOCK)
            tot = _dot(ones, kv_ref[0, pl.ds(off, MOBA_BLOCK), 0:k_tiles * LANES])
            kmean_scr[pl.ds(n, 1), :] = tot[0:1] * (1.0 / MOBA_BLOCK)
            return carry

        lax.fori_loop(0, lpad // MOBA_BLOCK, mean_body, 0)

    t0 = q0 + i * qb
    tq = t0 + lax.broadcasted_iota(jnp.int32, (qb, 1), 0)
    own = tq // MOBA_BLOCK
    lane = lax.broadcasted_iota(jnp.int32, (1, LANES), 1)
    qs = [q_ref[0, :, h * LANES:(h + 1) * LANES] for h in range(nh)]
    kmean = kmean_scr[...].astype(MXU)
    vals = []
    for h in range(nh):
        s_blk = _dot_t(qs[h], kmean[:, (h // 2) * LANES:(h // 2 + 1) * LANES])
        vals.append(jnp.where(lane < own, s_blk, NEG))
    val = jnp.concatenate(vals, axis=0)
    picks = []
    for _ in range(MOBA_TOPK):
        top = jnp.max(val, axis=-1, keepdims=True)
        idx = jnp.min(jnp.where(val == top, lane, LANES), axis=-1, keepdims=True)
        picks.append(idx)
        val = jnp.where(lane == idx, BELOW_NEG, val)

    _reset_state(m_scr, l_scr, acc_scr)
    pos_in_block = lax.broadcasted_iota(jnp.int32, (1, MOBA_BLOCK), 1)

    def step(c, with_own):
        off = pl.multiple_of(c * ch, ch)
        for h in range(nh):
            r = slice(h * qb, (h + 1) * qb)
            kt = kv_ref[0, pl.ds(off, ch), (h // 2) * LANES:(h // 2 + 1) * LANES]
            v_tile = (nh + h) // 2
            vt = kv_ref[0, pl.ds(off, ch), v_tile * LANES:(v_tile + 1) * LANES]
            s = _dot_t(qs[h], kt)
            parts = []
            for j in range(blocks_per_chunk):
                n = c * blocks_per_chunk + j
                chosen = ((picks[0][r] == n) | (picks[1][r] == n) | (picks[2][r] == n)) & (n < own)
                bias = jnp.where(chosen, 0.0, NEG)
                if with_own:
                    k_pos = n * MOBA_BLOCK + pos_in_block
                    bias = jnp.maximum(bias, jnp.where((own == n) & (k_pos <= tq), 0.0, NEG))
                parts.append(s[:, j * MOBA_BLOCK:(j + 1) * MOBA_BLOCK] + bias)
            sm = parts[0] if len(parts) == 1 else jnp.concatenate(parts, axis=1)
            _online_step(sm, vt, m_scr, l_scr, acc_scr, r)

    c_diag = t0 // ch

    def body(c, carry):
        step(c, False)
        return carry

    lax.fori_loop(0, c_diag, body, 0)
    step(c_diag, True)
    o = acc_scr[...] / l_scr[...]
    for h in range(nh):
        o_ref[0, :, h * LANES:(h + 1) * LANES] = o[h * qb:(h + 1) * qb].astype(o_ref.dtype)


def _moba_call(q, kv, *, qb, q0, ch):
    ns, t, _ = q.shape
    assert ch % MOBA_BLOCK == 0 and ch % qb == 0 and (q0 % qb == 0 or ch == kv.shape[1])
    rows = MOBA_HEADS * qb
    k_tiles = -(-MOBA_HEADS // 2)
    return pl.pallas_call(
        functools.partial(_moba_kernel, qb=qb, q0=q0, ch=ch),
        grid=(ns, t // qb),
        in_specs=[pl.BlockSpec((1, qb, W_QB), lambda b, i: (b, i, 0)),
                  pl.BlockSpec((1,) + kv.shape[1:], lambda b, i: (b, 0, 0))],
        out_specs=pl.BlockSpec((1, qb, W_OMOBA), lambda b, i: (b, i, 0)),
        out_shape=jax.ShapeDtypeStruct((ns, t, W_OMOBA), MXU),
        scratch_shapes=[pltpu.VMEM((LANES, k_tiles * LANES), F32),
                        pltpu.VMEM((rows, 1), F32), pltpu.VMEM((rows, 1), F32), pltpu.VMEM((rows, LANES), F32)],
        compiler_params=_params(("parallel", "arbitrary")),
        name="moba_mixer",
    )(q, kv)


def _online_step_t(sm, vt_t, m_scr, l_scr, acc_scr, cols):
    m_prev = m_scr[:, cols]
    m_new = jnp.maximum(m_prev, jnp.max(sm, axis=0, keepdims=True))
    alpha = jnp.exp(m_prev - m_new)
    p = jnp.exp(sm - m_new)
    l_scr[:, cols] = alpha * l_scr[:, cols] + jnp.sum(p, axis=0, keepdims=True)
    acc_scr[:, cols] = alpha * acc_scr[:, cols] + _dot(vt_t, p.astype(MXU))
    m_scr[:, cols] = m_new


def _softmax_t(sm):
    m = jnp.maximum(jnp.max(sm, axis=0, keepdims=True), M_INIT)
    e = jnp.exp(sm - m)
    den = jnp.sum(e, axis=0, keepdims=True)
    return e * jnp.where(den > 0.0, 1.0 / den, 0.0)


def _transpose_tile(ref, t):
    return ref[0, :, t * LANES:(t + 1) * LANES].astype(F32).T.astype(MXU)


def _tile_lanes(x, n):
    return x if n == 1 else jnp.concatenate([x] * n, axis=1)


def _moba_t_kernel(q_ref, k_ref, vt_ref, o_ref, kmean_scr, chosen_scr, m_scr, l_scr, acc_scr, *, qb):
    i = pl.program_id(1)
    lpad = k_ref.shape[1]
    nh = MOBA_HEADS
    k_tiles = -(-nh // 2)

    @pl.when(i == 0)
    def _():
        kmean_scr[...] = jnp.zeros(kmean_scr.shape, F32)
        ones = jnp.ones((8, MOBA_BLOCK), MXU)

        def mean_body(n, carry):
            off = pl.multiple_of(n * MOBA_BLOCK, MOBA_BLOCK)
            tot = _dot(ones, k_ref[0, pl.ds(off, MOBA_BLOCK), 0:k_tiles * LANES])
            kmean_scr[pl.ds(n, 1), :] = tot[0:1] * (1.0 / MOBA_BLOCK)
            return carry

        lax.fori_loop(0, lpad // MOBA_BLOCK, mean_body, 0)

    t0 = i * qb
    tq = t0 + lax.broadcasted_iota(jnp.int32, (1, qb), 1)
    own = tq // MOBA_BLOCK
    blk = lax.broadcasted_iota(jnp.int32, (LANES, 1), 0)
    q_t = [_transpose_tile(q_ref, h) for h in range(nh)]
    kmean = kmean_scr[...].astype(MXU)
    past = blk < own
    val = jnp.concatenate(
        [jnp.where(past, _dot(kmean[:, (h // 2) * LANES:(h // 2 + 1) * LANES], q_t[h]), NEG) for h in range(nh)],
        axis=1)
    past_all = _tile_lanes(past, nh)
    chosen = jnp.zeros(val.shape, F32)
    for _ in range(MOBA_TOPK):
        top = jnp.max(val, axis=0, keepdims=True)
        idx = jnp.min(jnp.where(val == top, blk, LANES), axis=0, keepdims=True)
        pick = blk == idx
        chosen = jnp.where(pick & past_all, 1.0, chosen)
        val = jnp.where(pick, BELOW_NEG, val)
    chosen_scr[...] = (chosen - 1.0) * BIG

    _reset_state(m_scr, l_scr, acc_scr)
    k_in_blk = lax.broadcasted_iota(jnp.int32, (MOBA_BLOCK, 1), 0)

    def step(n, with_own):
        off = pl.multiple_of(n * MOBA_BLOCK, MOBA_BLOCK)
        bias_rows = chosen_scr[pl.ds(n, 1), :]
        if with_own:
            own_bias = jnp.where((own == n) & (off + k_in_blk <= tq), 0.0, NEG)
        for h in range(nh):
            c = slice(h * qb, (h + 1) * qb)
            kt = k_ref[0, pl.ds(off, MOBA_BLOCK), (h // 2) * LANES:(h // 2 + 1) * LANES]
            v_tile = (nh + h) // 2 - (nh // 2)
            vt_t = vt_ref[0, v_tile * LANES:(v_tile + 1) * LANES, pl.ds(off, MOBA_BLOCK)]
            bias = jnp.maximum(bias_rows[:, c], own_bias) if with_own else bias_rows[:, c]
            _online_step_t(_dot(kt, q_t[h]) + bias, vt_t, m_scr, l_scr, acc_scr, c)

    n_own = t0 // MOBA_BLOCK

    def body(n, carry):
        step(n, False)
        return carry

    lax.fori_loop(0, n_own, body, 0)
    step(n_own, True)
    o_t = acc_scr[...] / l_scr[...]
    for h in range(nh):
        o_ref[0, :, h * LANES:(h + 1) * LANES] = o_t[:, h * qb:(h + 1) * qb].T.astype(o_ref.dtype)


def _moba_t_call(q, k, vt, *, qb):
    ns, t, _ = q.shape
    assert MOBA_BLOCK % qb == 0 and qb % LANES == 0
    cols = MOBA_HEADS * qb
    k_tiles = -(-MOBA_HEADS // 2)
    return pl.pallas_call(
        functools.partial(_moba_t_kernel, qb=qb),
        grid=(ns, t // qb),
        in_specs=[pl.BlockSpec((1, qb, W_QB), lambda b, i: (b, i, 0)),
                  pl.BlockSpec((1,) + k.shape[1:], lambda b, i: (b, 0, 0)),
                  pl.BlockSpec((1,) + vt.shape[1:], lambda b, i: (b, 0, 0))],
        out_specs=pl.BlockSpec((1, qb, W_OMOBA), lambda b, i: (b, i, 0)),
        out_shape=jax.ShapeDtypeStruct((ns, t, W_OMOBA), MXU),
        scratch_shapes=[pltpu.VMEM((LANES, k_tiles * LANES), F32), pltpu.VMEM((LANES, cols), F32),
                        pltpu.VMEM((1, cols), F32), pltpu.VMEM((1, cols), F32), pltpu.VMEM((LANES, cols), F32)],
        compiler_params=_params(("parallel", "arbitrary")),
        name="moba_mixer_t",
    )(q, k, vt)


def _nsa_t_kernel(qa_ref, qar_ref, gw_ref, kc_ref, vct_ref, kvs_ref, vst_ref, win_ref, wvt_ref, ovlt_ref, expt_ref,
                  o_ref, m_scr, l_scr, acc_scr, *, qb, ch, slab):
    t0 = pl.program_id(1) * qb
    tq = t0 + lax.broadcasted_iota(jnp.int32, (1, qb), 1)
    ncp = kc_ref.shape[1]
    win_rows = win_ref.shape[1]
    g, kv, nh = NSA_GROUP, NSA_KV_HEADS, NSA_HEADS
    qa_t = jnp.concatenate([_transpose_tile(qa_ref, h) for h in range(nh)], axis=1)
    qr_t = jnp.concatenate([_transpose_tile(qar_ref, h) for h in range(nh)], axis=1)

    c_end = lax.broadcasted_iota(jnp.int32, (ncp, 1), 0) * CMP_STRIDE + (CMP_LEN - 1)
    bias_c = jnp.where(c_end <= tq, 0.0, NEG)
    p_c = _softmax_t(_dot(kc_ref[0], qa_t) + _tile_lanes(bias_c, nh))
    o_cmp = _dot(vct_ref[0], p_c.astype(MXU))
    sums = []
    for k in range(kv):
        acc = p_c[:, (k * g) * qb:(k * g + 1) * qb]
        for j in range(1, g):
            acc = acc + p_c[:, (k * g + j) * qb:(k * g + j + 1) * qb]
        sums.append(acc)
    p_sum = jnp.concatenate(sums, axis=1)
    p_hi = p_sum.astype(MXU)
    p_lo = (p_sum - p_hi.astype(F32)).astype(MXU)
    imp = _dot(ovlt_ref[...], p_hi) + _dot(ovlt_ref[...], p_lo)
    blk = lax.broadcasted_iota(jnp.int32, (LANES, 1), 0)
    tb = tq // SEL_BLOCK
    forced = _tile_lanes((blk == 0) | (blk == tb) | (blk == tb - 1), kv)
    admissible = _tile_lanes(blk * SEL_BLOCK <= tq, kv)
    val = jnp.where(forced, BIG, jnp.where(admissible, imp, NEG))
    sel = jnp.zeros(val.shape, F32)
    for _ in range(SEL_TOPN):
        top = jnp.max(val, axis=0, keepdims=True)
        idx = jnp.min(jnp.where(val == top, blk, LANES), axis=0, keepdims=True)
        pick = blk == idx
        sel = jnp.where(pick, 1.0, sel)
        val = jnp.where(pick, BELOW_NEG, val)
    sel_b = sel.astype(MXU)

    _reset_state(m_scr, l_scr, acc_scr)

    def sel_step(c, causal):
        off = pl.multiple_of(c * ch, ch)
        kt = kvs_ref[0, pl.ds(off, ch), 0:LANES]
        vt_t = vst_ref[0, :, pl.ds(off, ch)]
        bias = (_dot(expt_ref[pl.ds(off, ch), :], sel_b) - 1.0) * BIG
        if causal:
            k_pos = off + lax.broadcasted_iota(jnp.int32, (ch, 1), 0)
            bias = bias + _tile_lanes(jnp.where(k_pos <= tq, 0.0, NEG), kv)
        bias = jnp.concatenate([bias[:, k * qb:(k + 1) * qb] for k in range(kv) for _ in range(g)], axis=1)
        _online_step_t(_dot(kt, qr_t) + bias, vt_t, m_scr, l_scr, acc_scr, slice(None))

    c_diag = t0 // ch

    def sel_body(c, carry):
        sel_step(c, False)
        return carry

    lax.fori_loop(0, c_diag, sel_body, 0)
    sel_step(c_diag, True)
    o_sel = acc_scr[...] / l_scr[...]

    w_start = pl.multiple_of(jnp.clip(t0 - WINDOW, 0, win_rows - slab), LANES)
    w_pos = w_start + lax.broadcasted_iota(jnp.int32, (slab, 1), 0)
    dist = tq - w_pos
    bias_w = jnp.where((dist >= 0) & (dist <= WINDOW), 0.0, NEG)
    kt = win_ref[0, pl.ds(w_start, slab), 0:LANES]
    p_w = _softmax_t(_dot(kt, qr_t) + _tile_lanes(bias_w, nh))
    o_win = _dot(wvt_ref[0, :, pl.ds(w_start, slab)], p_w.astype(MXU))

    gate_t = jax.nn.sigmoid(gw_ref[0]).T
    for h in range(nh):
        c = slice(h * qb, (h + 1) * qb)
        o = (gate_t[3 * h:3 * h + 1] * o_cmp[:, c] + gate_t[3 * h + 1:3 * h + 2] * o_sel[:, c]
             + gate_t[3 * h + 2:3 * h + 3] * o_win[:, c])
        o_ref[0, :, h * LANES:(h + 1) * LANES] = o.T.astype(o_ref.dtype)


def _nsa_t_call(qa, qar, gw, kc, vc, kvs, winb, ovl, *, qb, ch):
    ns, t, _ = qa.shape
    lpad = kvs.shape[1]
    assert ch % qb == 0 and qb % LANES == 0 and lpad == t
    slab = WINDOW + qb
    swap = lambda a: jnp.swapaxes(a, 1, 2)
    vct, vst, wvt = swap(vc), swap(kvs[:, :, LANES:]), swap(winb[:, :, LANES:])
    ovlt = ovl.T
    expt = jnp.asarray((np.arange(lpad)[:, None] // SEL_BLOCK) == np.arange(LANES)[None, :], MXU)
    tok = lambda w: pl.BlockSpec((1, qb, w), lambda b, i: (b, i, 0))
    seq = lambda a: pl.BlockSpec((1,) + a.shape[1:], lambda b, i: (b, 0, 0))
    full = lambda a: pl.BlockSpec(a.shape, lambda b, i: (0,) * a.ndim)
    cols = NSA_HEADS * qb
    return pl.pallas_call(
        functools.partial(_nsa_t_kernel, qb=qb, ch=ch, slab=slab),
        grid=(ns, t // qb),
        in_specs=[tok(W_QA), tok(W_QA), tok(W_GW), seq(kc), seq(vct), seq(kvs), seq(vst), seq(winb), seq(wvt),
                  full(ovlt), full(expt)],
        out_specs=tok(W_ONSA),
        out_shape=jax.ShapeDtypeStruct((ns, t, W_ONSA), MXU),
        scratch_shapes=[pltpu.VMEM((1, cols), F32), pltpu.VMEM((1, cols), F32), pltpu.VMEM((LANES, cols), F32)],
        compiler_params=_params(("parallel", "arbitrary")),
        name="nsa_mixer_t",
    )(qa, qar, gw, kc, vct, kvs, vst, winb, wvt, ovlt, expt)


def _dsa_t_kernel(qc_ref, qi_ref, gw_ref, kv_ref, kvt_ref, o_ref, key_scr, cut_scr, m_scr, l_scr, acc_scr,
                  *, qb, ch, k_top):
    t0 = pl.program_id(1) * qb
    tq = t0 + lax.broadcasted_iota(jnp.int32, (1, qb), 1)
    lpad = kv_ref.shape[1]
    n_ch = t0 // ch + 1
    gw_t = gw_ref[0].T
    w_idx = [gw_t[GW_WI + h:GW_WI + h + 1] for h in range(IDX_HEADS)]
    qi_t = [_transpose_tile(qi_ref, h) for h in range(IDX_HEADS)]
    slab = 64
    k_in_ch = lax.broadcasted_iota(jnp.int32, (ch, 1), 0)

    def score_body(c, carry):
        off = pl.multiple_of(c * ch, ch)
        kt = kv_ref[0, pl.ds(off, ch), LANES:2 * LANES]
        sc = w_idx[0] * jnp.maximum(_dot(kt, qi_t[0]), 0.0)
        for h in range(1, IDX_HEADS):
            sc = sc + w_idx[h] * jnp.maximum(_dot(kt, qi_t[h]), 0.0)
        sc = jnp.where(sc == 0.0, 0.0, sc)
        bits = lax.bitcast_convert_type(sc, jnp.int32)
        key = bits ^ (lax.shift_right_arithmetic(bits, 31) & 0x7FFFFFFF)
        key_scr[pl.ds(off, ch), :] = jnp.where(off + k_in_ch <= tq, key, INT_MIN)
        return carry

    lax.fori_loop(0, n_ch, score_body, 0)

    def count(pred):
        def body(c, acc):
            off = pl.multiple_of(c * ch, ch)
            for j in range(ch // slab):
                acc = acc + jnp.where(pred(key_scr[pl.ds(off + j * slab, slab), :], off + j * slab), 1.0, 0.0)
            return acc
        acc = lax.fori_loop(0, n_ch, body, jnp.zeros((slab, qb), F32))
        return jnp.sum(acc, axis=0, keepdims=True)

    few = tq + 1 <= k_top
    bits_per_check = 4

    def bit_cond(st):
        it, _, n_sel = st
        open_cols = jnp.where(few | (n_sel == k_top), 0.0, 1.0)
        return (it < 32) & (jnp.max(open_cols) > 0.0)

    def bit_body(st):
        it, prefix, n_sel = st
        for j in range(bits_per_check):
            cand = prefix | lax.shift_left(jnp.int32(1), 31 - (it + j))
            cand_key = cand ^ INT_MIN
            cnt = count(lambda k, off: k >= cand_key)
            take = (cnt >= k_top) & (n_sel != k_top)
            prefix, n_sel = jnp.where(take, cand, prefix), jnp.where(take, cnt, n_sel)
        return it + bits_per_check, prefix, n_sel

    _, prefix, n_sel = lax.while_loop(
        bit_cond, bit_body,
        (jnp.int32(0), jnp.zeros((1, qb), jnp.int32), jnp.full((1, qb), float(2 * lpad), F32)))
    thr = jnp.where(few, INT_MIN + 1, jnp.maximum(prefix ^ INT_MIN, INT_MIN + 1))

    overflow = (n_sel > k_top) & jnp.logical_not(few)
    cut_scr[...] = jnp.full((1, qb), lpad, jnp.int32)

    @pl.when(jnp.max(jnp.where(overflow, 1.0, 0.0)) > 0.0)
    def _():
        need = k_top - count(lambda k, off: k > thr)

        def cut_body(it, lo_hi):
            lo, hi = lo_hi
            mid = (lo + hi) // 2
            cnt = count(lambda k, off: (k == thr) & (off + k_in_ch[0:slab] <= mid))
            ok = cnt >= need
            return jnp.where(ok, lo, mid + 1), jnp.where(ok, mid, hi)
        lo, _ = lax.fori_loop(0, max(1, math.ceil(math.log2(lpad))), cut_body,
                              (jnp.zeros((1, qb), jnp.int32), jnp.full((1, qb), lpad - 1, jnp.int32)))
        cut_scr[...] = jnp.where(overflow, lo, lpad)

    cut = cut_scr[...]

    q_t = jnp.concatenate([_transpose_tile(qc_ref, h) for h in range(DSA_HEADS)], axis=1)
    _reset_state(m_scr, l_scr, acc_scr)

    def att_body(c, carry):
        off = pl.multiple_of(c * ch, ch)
        kt = kv_ref[0, pl.ds(off, ch), 0:LANES]
        key = key_scr[pl.ds(off, ch), :]
        bias = jnp.where((key > thr) | ((key == thr) & (off + k_in_ch <= cut)), 0.0, NEG)
        _online_step_t(_dot(kt, q_t) + _tile_lanes(bias, DSA_HEADS), kvt_ref[0, :, pl.ds(off, ch)],
                       m_scr, l_scr, acc_scr, slice(None))
        return carry

    lax.fori_loop(0, n_ch, att_body, 0)
    o_t = acc_scr[...] / l_scr[...]
    for h in range(DSA_HEADS):
        o_ref[0, :, h * LANES:(h + 1) * LANES] = o_t[:, h * qb:(h + 1) * qb].T.astype(o_ref.dtype)


def _dsa_t_call(qc, qi, gw, kv, kvt, *, qb, ch, k_top):
    ns, t, _ = qc.shape
    lpad = kv.shape[1]
    assert ch % qb == 0 and qb % LANES == 0 and lpad == t
    cols = DSA_HEADS * qb
    tok = lambda w: pl.BlockSpec((1, qb, w), lambda b, i: (b, i, 0))
    seq = lambda a: pl.BlockSpec((1,) + a.shape[1:], lambda b, i: (b, 0, 0))
    return pl.pallas_call(
        functools.partial(_dsa_t_kernel, qb=qb, ch=ch, k_top=k_top),
        grid=(ns, t // qb),
        in_specs=[tok(W_QC), tok(W_QI), tok(W_GW), seq(kv), seq(kvt)],
        out_specs=tok(W_ODSA),
        out_shape=jax.ShapeDtypeStruct((ns, t, W_ODSA), MXU),
        scratch_shapes=[pltpu.VMEM((lpad, qb), jnp.int32), pltpu.VMEM((1, qb), jnp.int32),
                        pltpu.VMEM((1, cols), F32), pltpu.VMEM((1, cols), F32), pltpu.VMEM((LANES, cols), F32)],
        compiler_params=_params(("parallel", "arbitrary")),
        name="dsa_mixer_t",
    )(qc, qi, gw, kv, kvt)


def _dsa_kernel(qc_ref, qi_ref, gw_ref, kv_ref, o_ref, key_scr, cut_scr, m_scr, l_scr, acc_scr,
                *, qb, q0, ch, k_top):
    t0 = q0 + pl.program_id(1) * qb
    tq = t0 + lax.broadcasted_iota(jnp.int32, (qb, 1), 0)
    lpad = kv_ref.shape[1]
    n_ch = t0 // ch + 1
    gw = gw_ref[0]
    w_idx = [gw[:, GW_WI + h:GW_WI + h + 1] for h in range(IDX_HEADS)]
    q_idx = [qi_ref[0, :, h * LANES:(h + 1) * LANES] for h in range(IDX_HEADS)]
    lane_ch = lax.broadcasted_iota(jnp.int32, (1, ch), 1)

    def score_body(c, carry):
        off = pl.multiple_of(c * ch, ch)
        kt = kv_ref[0, pl.ds(off, ch), LANES:2 * LANES]
        sc = w_idx[0] * jnp.maximum(_dot_t(q_idx[0], kt), 0.0)
        for h in range(1, IDX_HEADS):
            sc = sc + w_idx[h] * jnp.maximum(_dot_t(q_idx[h], kt), 0.0)
        sc = jnp.where(sc == 0.0, 0.0, sc)
        bits = lax.bitcast_convert_type(sc, jnp.int32)
        key = bits ^ (lax.shift_right_arithmetic(bits, 31) & 0x7FFFFFFF)
        key_scr[:, pl.ds(off, ch)] = jnp.where(off + lane_ch <= tq, key, INT_MIN)
        return carry

    lax.fori_loop(0, n_ch, score_body, 0)

    def count(pred):
        def body(c, acc):
            off = pl.multiple_of(c * ch, ch)
            hit = jnp.where(pred(key_scr[:, pl.ds(off, ch)], off), 1.0, 0.0)
            part = hit[:, 0:LANES]
            for j in range(1, ch // LANES):
                part = part + hit[:, j * LANES:(j + 1) * LANES]
            return acc + part
        acc = lax.fori_loop(0, n_ch, body, jnp.zeros((qb, LANES), F32))
        return jnp.sum(acc, axis=-1, keepdims=True)

    few = tq + 1 <= k_top

    def bit_cond(st):
        it, _, n_sel = st
        open_rows = jnp.where(few | (n_sel == k_top), 0, 1)
        return (it < 32) & (jnp.max(open_rows) > 0)

    def bit_body(st):
        it, prefix, n_sel = st
        cand = prefix | lax.shift_left(jnp.int32(1), 31 - it)
        cand_key = cand ^ INT_MIN
        cnt = count(lambda k, off: k >= cand_key)
        take = (cnt >= k_top) & (n_sel != k_top)
        return it + 1, jnp.where(take, cand, prefix), jnp.where(take, cnt, n_sel)

    _, prefix, n_sel = lax.while_loop(
        bit_cond, bit_body,
        (jnp.int32(0), jnp.zeros((qb, 1), jnp.int32), jnp.full((qb, 1), float(2 * lpad), F32)))
    thr = jnp.where(few, INT_MIN + 1, jnp.maximum(prefix ^ INT_MIN, INT_MIN + 1))

    overflow = (n_sel > k_top) & jnp.logical_not(few)
    cut_scr[...] = jnp.full((qb, 1), lpad, jnp.int32)

    @pl.when(jnp.max(jnp.where(overflow, 1, 0)) > 0)
    def _():
        need = k_top - count(lambda k, off: k > thr)

        def cut_body(it, lo_hi):
            lo, hi = lo_hi
            mid = (lo + hi) // 2
            cnt = count(lambda k, off: (k == thr) & (off + lane_ch <= mid))
            ok = cnt >= need
            return jnp.where(ok, lo, mid + 1), jnp.where(ok, mid, hi)
        lo, _ = lax.fori_loop(0, max(1, math.ceil(math.log2(lpad))), cut_body,
                              (jnp.zeros((qb, 1), jnp.int32), jnp.full((qb, 1), lpad - 1, jnp.int32)))
        cut_scr[...] = jnp.where(overflow, lo, lpad)

    cut = cut_scr[...]

    q = _stack_heads(qc_ref, range(DSA_HEADS))
    _reset_state(m_scr, l_scr, acc_scr)

    def att_body(c, carry):
        off = pl.multiple_of(c * ch, ch)
        kt = kv_ref[0, pl.ds(off, ch), 0:LANES]
        key = key_scr[:, pl.ds(off, ch)]
        bias = jnp.where((key > thr) | ((key == thr) & (off + lane_ch <= cut)), 0.0, NEG)
        _online_step(_add_bias(_dot_t(q, kt), bias, qb), kt, m_scr, l_scr, acc_scr, slice(None))
        return carry

    lax.fori_loop(0, n_ch, att_body, 0)
    o = acc_scr[...] / l_scr[...]
    for h in range(DSA_HEADS):
        o_ref[0, :, h * LANES:(h + 1) * LANES] = o[h * qb:(h + 1) * qb].astype(o_ref.dtype)


def _dsa_call(qc, qi, gw, kv, *, qb, q0, ch, k_top):
    ns, t, _ = qc.shape
    lpad = kv.shape[1]
    assert ch % qb == 0 and (q0 % qb == 0 or ch == lpad)
    rows = DSA_HEADS * qb
    tok = lambda w: pl.BlockSpec((1, qb, w), lambda b, i: (b, i, 0))
    return pl.pallas_call(
        functools.partial(_dsa_kernel, qb=qb, q0=q0, ch=ch, k_top=k_top),
        grid=(ns, t // qb),
        in_specs=[tok(W_QC), tok(W_QI), tok(W_GW), pl.BlockSpec((1,) + kv.shape[1:], lambda b, i: (b, 0, 0))],
        out_specs=tok(W_ODSA),
        out_shape=jax.ShapeDtypeStruct((ns, t, W_ODSA), MXU),
        scratch_shapes=[pltpu.VMEM((qb, lpad), jnp.int32), pltpu.VMEM((qb, 1), jnp.int32),
                        pltpu.VMEM((rows, 1), F32), pltpu.VMEM((rows, 1), F32), pltpu.VMEM((rows, LANES), F32)],
        compiler_params=_params(("parallel", "arbitrary")),
        name="dsa_mixer",
    )(qc, qi, gw, kv)


def _post_attn_kernel(oa_ref, ob_ref, oc_ref, x_ref, g1_ref, sh_ref, sc_ref, wo_ref, lng_ref, lnb_ref,
                      wr_ref, br_ref, x1_ref, h2_ref, gate_ref):
    x = x_ref[...]
    s, r, d = x.shape
    att = None
    off = 0
    for o_ref in (oa_ref, ob_ref, oc_ref):
        w = o_ref.shape[2]
        part = _dot(o_ref[...].reshape(s * r, w), wo_ref[off:off + w, :])
        att = part if att is None else att + part
        off += w
    att = att.reshape(s, r, d)
    x1 = _ln(ALPHA * x + g1_ref[...] * att) * lng_ref[...] + lnb_ref[...]
    x1_ref[...] = x1
    h2 = _ln(x1) * (1.0 + sc_ref[...]) + sh_ref[...]
    h2b = h2.reshape(s * r, d).astype(MXU)
    h2_ref[...] = h2b.reshape(s, r, d)

    logit = _dot(h2b, wr_ref[...]) + br_ref[...]
    lane = lax.broadcasted_iota(jnp.int32, logit.shape, 1)
    is_g = (lane >= N_EXPERTS) & (lane < N_EXPERTS + N_GROUPS)
    lg = jnp.where(is_g, logit, NEG)
    mg = jnp.max(lg, axis=-1, keepdims=True)
    g_lane = jnp.min(jnp.where(lg == mg, lane, LANES), axis=-1, keepdims=True)
    g_den = jnp.sum(jnp.where(is_g, jnp.exp(lg - mg), 0.0), axis=-1, keepdims=True)
    g_w = 1.0 / g_den
    g_idx = g_lane - N_EXPERTS
    in_grp = (lane >= g_idx * EXPERTS_PER_GROUP) & (lane < (g_idx + 1) * EXPERTS_PER_GROUP)
    le = jnp.where(in_grp, logit, NEG)
    me = jnp.max(le, axis=-1, keepdims=True)
    ee = jnp.where(in_grp, jnp.exp(le - me), 0.0)
    pe = ee / jnp.sum(ee, axis=-1, keepdims=True)
    p1 = jnp.max(pe, axis=-1, keepdims=True)
    i1 = jnp.min(jnp.where((pe == p1) & in_grp, lane, LANES), axis=-1, keepdims=True)
    rest = jnp.where(in_grp & (lane != i1), pe, -1.0)
    p2 = jnp.max(rest, axis=-1, keepdims=True)
    i2 = jnp.min(jnp.where(rest == p2, lane, LANES), axis=-1, keepdims=True)
    tot = p1 + p2
    gate = jnp.where(lane == i1, g_w * p1 / tot, jnp.where(lane == i2, g_w * p2 / tot, 0.0))
    gate_ref[...] = gate.reshape(s, r, LANES)


def _post_attn_call(o_parts, x, g1, sh2, sc2, wo, lng, lnb, wr, br, s_blk, r_blk):
    ns, rt, _ = x.shape
    tok = lambda w: pl.BlockSpec((s_blk, r_blk, w), lambda i, j: (i, j, 0))
    mod = pl.BlockSpec((s_blk, 1, D_MODEL), lambda i, j: (i, 0, 0))
    full = lambda a: pl.BlockSpec(a.shape, lambda i, j: (0,) * a.ndim)
    return pl.pallas_call(
        _post_attn_kernel,
        grid=(ns // s_blk, rt // r_blk),
        in_specs=[tok(o.shape[2]) for o in o_parts] + [tok(D_MODEL), mod, mod, mod,
                                                       full(wo), full(lng), full(lnb), full(wr), full(br)],
        out_specs=[tok(D_MODEL), tok(D_MODEL), tok(LANES)],
        out_shape=[jax.ShapeDtypeStruct((ns, rt, D_MODEL), F32),
                   jax.ShapeDtypeStruct((ns, rt, D_MODEL), MXU),
                   jax.ShapeDtypeStruct((ns, rt, LANES), F32)],
        compiler_params=_params(("parallel", "parallel")),
        name="wout_ln_router",
    )(*o_parts, x, g1, sh2, sc2, wo, lng, lnb, wr, br)


def _moe_kernel(h_ref, gate_ref, x_ref, g2_ref, wi_ref, wo_ref, lng_ref, lnb_ref, out_ref, acc_ref):
    e = pl.program_id(2)
    s, r, d = x_ref.shape

    @pl.when(e == 0)
    def _():
        acc_ref[...] = jnp.zeros_like(acc_ref)

    h = h_ref[...].reshape(s * r, d)
    ab = _dot(h, wi_ref[0])
    a, b = ab[:, :D_EXPERT], ab[:, D_EXPERT:]
    gate = gate_ref[...].reshape(s * r, LANES)
    lane = lax.broadcasted_iota(jnp.int32, gate.shape, 1)
    gcol = jnp.sum(jnp.where(lane == e, gate, 0.0), axis=-1, keepdims=True)
    hid = (a * jax.nn.sigmoid(a) * b * gcol).astype(MXU)
    acc_ref[...] += _dot(hid, wo_ref[0])

    @pl.when(e == N_EXPERTS - 1)
    def _():
        f = acc_ref[...].reshape(s, r, d)
        out_ref[...] = _ln(ALPHA * x_ref[...] + g2_ref[...] * f) * lng_ref[...] + lnb_ref[...]


def _moe_call(h2, gate, x1, g2, wi, wo, lng, lnb, s_blk, r_blk):
    ns, rt, _ = x1.shape
    tok = lambda w: pl.BlockSpec((s_blk, r_blk, w), lambda i, j, e: (i, j, 0))
    mod = pl.BlockSpec((s_blk, 1, D_MODEL), lambda i, j, e: (i, 0, 0))
    vec = pl.BlockSpec((1, 1, D_MODEL), lambda i, j, e: (0, 0, 0))
    return pl.pallas_call(
        _moe_kernel,
        grid=(ns // s_blk, rt // r_blk, N_EXPERTS),
        in_specs=[tok(D_MODEL), tok(LANES), tok(D_MODEL), mod,
                  pl.BlockSpec((1, D_MODEL, 2 * D_EXPERT), lambda i, j, e: (e, 0, 0)),
                  pl.BlockSpec((1, D_EXPERT, D_MODEL), lambda i, j, e: (e, 0, 0)),
                  vec, vec],
        out_specs=tok(D_MODEL),
        out_shape=jax.ShapeDtypeStruct((ns, rt, D_MODEL), F32),
        scratch_shapes=[pltpu.VMEM((s_blk * r_blk, D_MODEL), F32)],
        compiler_params=_params(("parallel", "parallel", "arbitrary")),
        name="moe_ln",
    )(h2, gate, x1, g2, wi, wo, lng, lnb)


def _mixers(pr, cmp_w, past, cfg):
    (nsa, win, moba, dsa, gw, qa, qar, qb_, qc, qi, kc, vc, kvs, winb, mobab, dsab) = pr
    ns, t, _ = qa.shape
    if past is None:
        l_total, lpad, win_pos0 = t, t, 0
        win_out = win[:, -min(WINDOW, t):]
    else:
        cache_nsa, cache_moba, cache_dsa, state_win, table, l = past
        past_len = table.shape[1] * cache_nsa.shape[2]
        l_total = past_len + t
        lpad = -(-l_total // MOBA_BLOCK) * MOBA_BLOCK
        kc, vc, kvs, mobab, dsab = _assemble_call(table, cache_nsa, cache_moba, cache_dsa, l,
                                                 nsa, moba, dsa, lpad)
        w_buf = state_win.shape[2]
        rows_pad = -(-(w_buf + t) // LANES) * LANES
        winb, win_out = _win_call(state_win, l, win, rows_pad)
        win_pos0 = past_len - w_buf
    q0 = l_total - t
    chunk_rows = LANES * CMP_STRIDE
    nch = lpad // CMP_STRIDE
    ncp = -(-nch // (2 * LANES)) * 2 * LANES
    kcmp, vcmp = _cmp_call(kc.reshape(ns, nch, chunk_rows), vc.reshape(ns, nch, chunk_rows), *cmp_w, ncp)
    ovl, expand = _nsa_tables(l_total, ncp, lpad)
    qb = cfg['qb']
    k_top = min(DSA_TOPK, l_total // 4)
    if past is None and qb % LANES == 0:
        swap = lambda a: jnp.swapaxes(a, 1, 2)
        o_nsa = _nsa_t_call(qa, qar, gw, kcmp, vcmp, kvs, winb, ovl, qb=qb, ch=cfg['ch_nsa'])
        o_moba = _moba_t_call(qb_, mobab, swap(mobab[:, :, (MOBA_HEADS // 2) * LANES:]), qb=qb)
        o_dsa = _dsa_t_call(qc, qi, gw, dsab, swap(dsab[:, :, :LANES]), qb=qb, ch=cfg['ch_dsa'], k_top=k_top)
    else:
        o_nsa = _nsa_call(qa, qar, gw, kcmp, vcmp, kvs, winb, ovl, expand,
                          qb=qb, q0=q0, ch=cfg['ch_nsa'], win_pos0=win_pos0)
        o_moba = _moba_call(qb_, mobab, qb=qb, q0=q0, ch=cfg['ch_moba'])
        o_dsa = _dsa_call(qc, qi, gw, dsab, qb=qb, q0=q0, ch=cfg['ch_dsa'], k_top=k_top)
    return (o_nsa, o_moba, o_dsa), win_out


def _layer(x, mod, wts, tabs, cmp_w, past, cfg):
    sh1, sc1, g1, sh2, sc2, g2 = mod
    w_perm, wo, ln1g, ln1b, ln2g, ln2b, wr, br, wei, weo = wts
    pr = _proj_call(x, sh1, sc1, w_perm, tabs, *cfg['blk_a'])
    o_parts, win_out = _mixers(pr, cmp_w, past, cfg)
    x1, h2, gate = _post_attn_call(o_parts, x, g1, sh2, sc2, wo, ln1g, ln1b, wr, br, *cfg['blk_a'])
    y = _moe_call(h2, gate, x1, g2, wei, weo, ln2g, ln2b, *cfg['blk_m'])
    return y, pr[0], pr[2], pr[3], win_out


def _layer_weights(p, l):
    n_pad = LANES - N_EXPERTS - N_GROUPS
    wr = jnp.concatenate([p['w_router_expert'][l], p['w_router_group'][l],
                          jnp.zeros((D_MODEL, n_pad), F32)], axis=1).astype(MXU)
    br = jnp.concatenate([p['b_router_expert'][l], p['b_router_group'][l], jnp.zeros((n_pad,), F32)])[None, :]
    vec = lambda a: a[l].reshape(1, 1, D_MODEL)
    return (_permute_w_in(p['w_in'][l]), _pad_w_out(p['w_out'][l]), vec(p['ln1_g']), vec(p['ln1_b']),
            vec(p['ln2_g']), vec(p['ln2_b']), wr, br,
            p['w_expert_in'][l].astype(MXU), p['w_expert_out'][l].astype(MXU))


def kernel(x_prompt, x_sample, c_prompt, c_sample, cache_nsa, cache_moba, cache_dsa, state_win, page_table, w_ada, b_ada, w_in, cmp_pe, cmp_w1, cmp_w2, w_out, ln1_g, ln1_b, ln2_g, ln2_b, w_router_group, b_router_group, w_router_expert, b_router_expert, w_expert_in, w_expert_out):
    p = dict(w_in=w_in, w_out=w_out, ln1_g=ln1_g, ln1_b=ln1_b, ln2_g=ln2_g, ln2_b=ln2_b,
             w_router_group=w_router_group, b_router_group=b_router_group,
             w_router_expert=w_router_expert, b_router_expert=b_router_expert,
             w_expert_in=w_expert_in, w_expert_out=w_expert_out)
    nb_p, t_p, _ = x_prompt.shape
    nb_s, t_s, _ = x_sample.shape
    past_len = page_table.shape[1] * cache_nsa.shape[2]
    lpad_s = -(-(past_len + t_s) // MOBA_BLOCK) * MOBA_BLOCK

    n_c = nb_p + nb_s
    n_c_pad = -(-n_c // 8) * 8
    c_all = jnp.concatenate([c_prompt, c_sample, jnp.zeros((n_c_pad - n_c, D_MODEL), F32)], axis=0)
    mod_all = _ada_call(c_all, w_ada, b_ada)

    tabs_p = _rope_tables(jnp.arange(t_p))
    tabs_s = _rope_tables(past_len + jnp.arange(t_s))
    cfg_p = dict(blk_a=(1, 512), blk_m=(1, 1024), qb=128, ch_nsa=512, ch_moba=MOBA_BLOCK, ch_dsa=512)
    cfg_s = dict(blk_a=(32, t_s), blk_m=(nb_s, t_s), qb=t_s, ch_nsa=lpad_s, ch_moba=lpad_s, ch_dsa=lpad_s)

    y_p, y_s = x_prompt, x_sample
    st_p, st_s = [], []
    for l in range(DEPTH):
        mod_l = mod_all[l]
        split = lambda m: tuple(m[:, None, k * D_MODEL:(k + 1) * D_MODEL] for k in range(6))
        mod_p = split(mod_l[:nb_p])
        mod_s = split(mod_l[nb_p:n_c])
        wts = _layer_weights(p, l)
        cmp_w = _cmp_weights(cmp_pe[l], cmp_w1[l], cmp_w2[l])
        y_p, *sp = _layer(y_p, mod_p, wts, tabs_p, cmp_w, None, cfg_p)
        past = (cache_nsa, cache_moba, cache_dsa, state_win, page_table, l)
        y_s, *ss = _layer(y_s, mod_s, wts, tabs_s, cmp_w, past, cfg_s)
        st_p.append(sp)
        st_s.append(ss)

    def stack(st, k, tail):
        a = jnp.stack([s[k] for s in st])
        return a.reshape(a.shape[:3] + tail)

    nsa_t = (4, NSA_KV_HEADS, HEAD_DIM)
    moba_t = (2, MOBA_HEADS, HEAD_DIM)
    dsa_t = (3, HEAD_DIM)
    win_t = (2, NSA_KV_HEADS, HEAD_DIM)
    return (y_p, y_s, stack(st_p, 0, nsa_t), stack(st_s, 0, nsa_t), stack(st_p, 1, moba_t),
            stack(st_s, 1, moba_t), stack(st_p, 2, dsa_t), stack(st_s, 2, dsa_t),
            stack(st_p, 3, win_t), stack(st_s, 3, win_t))
```

```python
import functools
import math

import numpy as np
import jax
import jax.numpy as jnp
from jax import lax
from jax.experimental import pallas as pl
from jax.experimental.pallas import tpu as pltpu

D_MODEL = 1024
DEPTH = 2
HEAD_DIM = 64
NSA_HEADS = 6
NSA_KV_HEADS = 2
NSA_GROUP = NSA_HEADS // NSA_KV_HEADS
MOBA_HEADS = 5
DSA_HEADS = 5
D_MIX = (NSA_HEADS + MOBA_HEADS + DSA_HEADS) * HEAD_DIM
CMP_LEN = 32
CMP_STRIDE = 16
CMP_HIDDEN = 256
SEL_BLOCK = 64
SEL_TOPN = 16
WINDOW = 512
MOBA_BLOCK = 256
MOBA_TOPK = 3
IDX_HEADS = 4
DSA_TOPK = 256
ROPE_THETA = 500000.0
ROT_DIM = HEAD_DIM // 4
N_GROUPS = 4
EXPERTS_PER_GROUP = 8
N_EXPERTS = N_GROUPS * EXPERTS_PER_GROUP
D_EXPERT = 256
ALPHA = (2 * DEPTH) ** 0.25
LN_EPS = 1e-5
NEG = -1e30
BIG = 1e30
BELOW_NEG = -3e38
M_INIT = -1e29
INT_MIN = -2 ** 31

LANES = 128
F32 = jnp.float32
BF16 = jnp.bfloat16
MXU = BF16
Q_SCALE = HEAD_DIM ** -0.5

W_QA = NSA_HEADS * LANES
W_KVA = 6 * NSA_KV_HEADS * HEAD_DIM
W_NSA = 4 * NSA_KV_HEADS * HEAD_DIM
W_WIN = 2 * NSA_KV_HEADS * HEAD_DIM
W_QB = MOBA_HEADS * LANES
W_QC = DSA_HEADS * LANES
W_MOBA = 2 * MOBA_HEADS * HEAD_DIM
W_DSA = 3 * HEAD_DIM
W_DSA_PAD = 2 * LANES
W_QI = IDX_HEADS * LANES
W_GW = LANES
OFF_QA = 0
OFF_KVA = OFF_QA + W_QA
OFF_QB = OFF_KVA + W_KVA
OFF_QC = OFF_QB + W_QB
OFF_MOBA = OFF_QC + W_QC
OFF_DSA = OFF_MOBA + W_MOBA
OFF_QI = OFF_DSA + W_DSA_PAD
OFF_GW = OFF_QI + W_QI
W_PROJ = OFF_GW + W_GW
GW_WI = 3 * NSA_HEADS

W_ONSA = NSA_HEADS * LANES
W_OMOBA = MOBA_HEADS * LANES
W_ODSA = DSA_HEADS * LANES
W_OALL = W_ONSA + W_OMOBA + W_ODSA

VMEM_LIMIT = 56 * 1024 * 1024


def _params(sem):
    return pltpu.CompilerParams(dimension_semantics=sem, vmem_limit_bytes=VMEM_LIMIT)


def _ln(x):
    mu = jnp.mean(x, axis=-1, keepdims=True)
    xc = x - mu
    var = jnp.mean(xc * xc, axis=-1, keepdims=True)
    return xc * lax.rsqrt(var + LN_EPS)


def _dot(a, b):
    return jnp.dot(a, b, preferred_element_type=F32)


def _dot_t(a, b):
    return lax.dot_general(a, b, (((1,), (1,)), ((), ())), preferred_element_type=F32)


def _ada_kernel(c_ref, w_ref, b_ref, o_ref):
    c = c_ref[...]
    a = (c * jax.nn.sigmoid(c)).astype(MXU)
    o_ref[0] = _dot(a, w_ref[0].astype(MXU)) + b_ref[0]


def _ada_call(c_all, w_ada, b_ada):
    n = c_all.shape[0]
    tn = 1536
    return pl.pallas_call(
        _ada_kernel,
        grid=(DEPTH, 6 * D_MODEL // tn),
        in_specs=[
            pl.BlockSpec((n, D_MODEL), lambda l, j: (0, 0)),
            pl.BlockSpec((1, D_MODEL, tn), lambda l, j: (l, 0, j)),
            pl.BlockSpec((1, 1, tn), lambda l, j: (l, 0, j)),
        ],
        out_specs=pl.BlockSpec((1, n, tn), lambda l, j: (l, 0, j)),
        out_shape=jax.ShapeDtypeStruct((DEPTH, n, 6 * D_MODEL), F32),
        compiler_params=_params(("arbitrary", "arbitrary")),
        name="ada_mod",
    )(c_all, w_ada, b_ada.reshape(DEPTH, 1, 6 * D_MODEL))


def _rope_tile(t, cos, s_lo, s_hi, first_half_only):
    if first_half_only:
        lane = lax.broadcasted_iota(jnp.int32, cos.shape, 2)
        keep = lane < HEAD_DIM
        cos = jnp.where(keep, cos, 1.0)
        s_lo = jnp.where(keep, s_lo, 0.0)
        s_hi = jnp.where(keep, s_hi, 0.0)
    half = ROT_DIM // 2
    up = pltpu.roll(t, LANES - half, axis=2)
    dn = pltpu.roll(t, half, axis=2)
    return t * cos + up * s_lo + dn * s_hi


def _rope_slab(p, cos, s_lo, s_hi, modes):
    tiles = []
    for j, m in enumerate(modes):
        t = p[:, :, j * LANES:(j + 1) * LANES]
        if m != 'n':
            t = _rope_tile(t, cos, s_lo, s_hi, m == 'h')
        tiles.append(t)
    return tiles[0] if len(tiles) == 1 else jnp.concatenate(tiles, axis=2)


PROJ_F32_WIDTHS = (W_NSA, W_WIN, W_MOBA, W_DSA, W_GW)
PROJ_MXU_WIDTHS = (W_QA, W_QA, W_QB, W_QC, W_QI, LANES, LANES, 2 * LANES, W_WIN, W_MOBA, W_DSA_PAD)


def _proj_kernel(x_ref, sh_ref, sc_ref, w_ref, cos_ref, slo_ref, shi_ref,
                 nsa_ref, win_ref, moba_ref, dsa_ref, gw_ref,
                 qa_ref, qar_ref, qb_ref, qc_ref, qi_ref,
                 kc_ref, vc_ref, kvs_ref, winb_ref, mobab_ref, dsab_ref):
    x = x_ref[...]
    s, r, d = x.shape
    h = _ln(x) * (1.0 + sc_ref[...]) + sh_ref[...]
    hb = h.reshape(s * r, d).astype(MXU)
    rope = functools.partial(_rope_slab, cos=cos_ref[...], s_lo=slo_ref[...], s_hi=shi_ref[...])

    def seg(off, width):
        return _dot(hb, w_ref[:, off:off + width]).reshape(s, r, width)

    qa = seg(OFF_QA, W_QA)
    qa_ref[...] = qa.astype(MXU)
    qar_ref[...] = rope(qa, modes='f' * NSA_HEADS).astype(MXU)
    kva = seg(OFF_KVA, W_KVA)
    nsa = rope(kva[:, :, :W_NSA], modes='nnfn')
    nsa_ref[...] = nsa
    kc_ref[...] = nsa[:, :, 0:LANES].astype(MXU)
    vc_ref[...] = nsa[:, :, LANES:2 * LANES].astype(MXU)
    kvs_ref[...] = nsa[:, :, 2 * LANES:].astype(MXU)
    win = rope(kva[:, :, W_NSA:], modes='fn')
    win_ref[...] = win
    winb_ref[...] = win.astype(MXU)
    qb_ref[...] = rope(seg(OFF_QB, W_QB), modes='f' * MOBA_HEADS).astype(MXU)
    qc_ref[...] = rope(seg(OFF_QC, W_QC), modes='f' * DSA_HEADS).astype(MXU)
    moba = rope(seg(OFF_MOBA, W_MOBA), modes='ffhnn')
    moba_ref[...] = moba
    mobab_ref[...] = moba.astype(MXU)
    dsa = rope(seg(OFF_DSA, W_DSA_PAD), modes='hh')
    dsa_ref[...] = dsa[:, :, :W_DSA]
    dsab_ref[...] = dsa.astype(MXU)
    qi_ref[...] = rope(seg(OFF_QI, W_QI), modes='f' * IDX_HEADS).astype(MXU)
    gw_ref[...] = seg(OFF_GW, W_GW)


def _proj_call(x, sh, sc, w_perm, tabs, s_blk, r_blk):
    ns, rt, _ = x.shape
    cos, s_lo, s_hi = tabs
    tok = lambda w: pl.BlockSpec((s_blk, r_blk, w), lambda i, j: (i, j, 0))
    mod = pl.BlockSpec((s_blk, 1, D_MODEL), lambda i, j: (i, 0, 0))
    tab = pl.BlockSpec((1, r_blk, LANES), lambda i, j: (0, j, 0))
    widths = PROJ_F32_WIDTHS + PROJ_MXU_WIDTHS
    dtypes = (F32,) * len(PROJ_F32_WIDTHS) + (MXU,) * len(PROJ_MXU_WIDTHS)
    return pl.pallas_call(
        _proj_kernel,
        grid=(ns // s_blk, rt // r_blk),
        in_specs=[tok(D_MODEL), mod, mod,
                  pl.BlockSpec((D_MODEL, W_PROJ), lambda i, j: (0, 0)),
                  tab, tab, tab],
        out_specs=[tok(w) for w in widths],
        out_shape=[jax.ShapeDtypeStruct((ns, rt, w), dt) for w, dt in zip(widths, dtypes)],
        compiler_params=_params(("parallel", "parallel")),
        name="ln_proj_rope",
    )(x, sh, sc, w_perm, cos, s_lo, s_hi)


def _pad_heads(w, halves, scale):
    z = jnp.zeros((w.shape[0], HEAD_DIM), w.dtype)
    cols = []
    for h, half in enumerate(halves):
        wh = w[:, h * HEAD_DIM:(h + 1) * HEAD_DIM] * scale
        cols += [wh, z] if half == 0 else [z, wh]
    return jnp.concatenate(cols, axis=1)


def _permute_w_in(w_in_l):
    sizes = (NSA_HEADS * HEAD_DIM, 6 * NSA_KV_HEADS * HEAD_DIM, 3 * NSA_HEADS,
             MOBA_HEADS * HEAD_DIM, MOBA_HEADS * HEAD_DIM, MOBA_HEADS * HEAD_DIM,
             DSA_HEADS * HEAD_DIM, 2 * HEAD_DIM, IDX_HEADS * HEAD_DIM, HEAD_DIM, IDX_HEADS)
    offs = np.concatenate([[0], np.cumsum(sizes)])
    part = lambda k: w_in_l[:, int(offs[k]):int(offs[k + 1])]
    zeros = lambda n: jnp.zeros((D_MODEL, n), w_in_l.dtype)
    cols = [_pad_heads(part(0), [h // NSA_GROUP for h in range(NSA_HEADS)], Q_SCALE),
            part(1),
            _pad_heads(part(3), [h % 2 for h in range(MOBA_HEADS)], Q_SCALE),
            _pad_heads(part(6), [0] * DSA_HEADS, Q_SCALE),
            part(4), part(5),
            part(7), part(9), zeros(W_DSA_PAD - W_DSA),
            _pad_heads(part(8), [0] * IDX_HEADS, 1.0),
            part(2), part(10), zeros(W_GW - 3 * NSA_HEADS - IDX_HEADS)]
    return jnp.concatenate(cols, axis=1).astype(MXU)


def _pad_w_out(w_out_l):
    z = jnp.zeros((HEAD_DIM, D_MODEL), w_out_l.dtype)
    rows = []
    head = lambda h: w_out_l[h * HEAD_DIM:(h + 1) * HEAD_DIM]
    for h in range(NSA_HEADS):
        rows += [head(h), z] if h // NSA_GROUP == 0 else [z, head(h)]
    for h in range(MOBA_HEADS):
        rows += [head(NSA_HEADS + h), z] if (MOBA_HEADS + h) % 2 == 0 else [z, head(NSA_HEADS + h)]
    for h in range(DSA_HEADS):
        rows += [z, head(NSA_HEADS + MOBA_HEADS + h)]
    return jnp.concatenate(rows, axis=0).astype(MXU)


def _rope_tables(pos):
    half = ROT_DIM // 2
    inv = ROPE_THETA ** (-jnp.arange(0, ROT_DIM, 2, dtype=F32) / ROT_DIM)
    ang = pos.astype(F32)[:, None] * inv[None, :]
    cos, sin = jnp.cos(ang), jnp.sin(ang)
    t = pos.shape[0]
    ones = jnp.ones((t, HEAD_DIM - ROT_DIM), F32)
    zeros = jnp.zeros((t, HEAD_DIM - ROT_DIM), F32)
    zh = jnp.zeros((t, half), F32)
    c = jnp.concatenate([cos, cos, ones], axis=1)
    s_lo = jnp.concatenate([-sin, zh, zeros], axis=1)
    s_hi = jnp.concatenate([zh, sin, zeros], axis=1)
    rep = lambda a: jnp.concatenate([a, a], axis=1)[None]
    return rep(c), rep(s_lo), rep(s_hi)


def _assemble_kernel(tbl_ref, *refs, n_pages, ppb):
    pools = [refs[3 * j:3 * j + 3] for j in range(ppb)]
    nsa_new_ref, moba_new_ref, dsa_new_ref = refs[3 * ppb:3 * ppb + 3]
    kc_ref, vc_ref, kvs_ref, mob_ref, dsao_ref = refs[3 * ppb + 3:]
    p = pl.program_id(1)
    rows = kc_ref.shape[1]

    def emit(nsa, moba, dsa):
        kc_ref[0] = nsa[:, 0:LANES].astype(MXU)
        vc_ref[0] = nsa[:, LANES:2 * LANES].astype(MXU)
        kvs_ref[0] = nsa[:, 2 * LANES:].astype(MXU)
        mob_ref[0] = moba.astype(MXU)
        dsao_ref[0] = jnp.concatenate([dsa, jnp.zeros((rows, W_DSA_PAD - W_DSA), F32)], axis=1).astype(MXU)

    cat = lambda parts: parts[0] if len(parts) == 1 else jnp.concatenate(parts, axis=0)

    @pl.when(p * ppb < n_pages)
    def _():
        emit(*(cat([pools[j][k][0, 0] for j in range(ppb)]) for k in range(3)))

    @pl.when(p * ppb == n_pages)
    def _():
        t_new = nsa_new_ref.shape[1]
        tail = lambda new: jnp.concatenate([new, jnp.zeros((rows - t_new, new.shape[1]), F32)], axis=0)
        emit(tail(nsa_new_ref[0]), tail(moba_new_ref[0]), tail(dsa_new_ref[0]))

    @pl.when(p * ppb > n_pages)
    def _():
        emit(jnp.zeros((rows, W_NSA), F32), jnp.zeros((rows, W_MOBA), F32), jnp.zeros((rows, W_DSA), F32))


def _assemble_call(table, cache_nsa, cache_moba, cache_dsa, l, nsa_new, moba_new, dsa_new, lpad):
    ns, n_pages = table.shape
    page = cache_nsa.shape[2]
    ppb = MOBA_BLOCK // page
    assert n_pages % ppb == 0 and lpad % (ppb * page) == 0 and nsa_new.shape[1] <= ppb * page
    n_blk = lpad // (ppb * page)

    def pool(w, j):
        return pl.BlockSpec((1, 1, page, w),
                            lambda b, p, tbl: (l, tbl[b, jnp.minimum(p * ppb + j, n_pages - 1)], 0, 0))

    new = lambda w: pl.BlockSpec((1, nsa_new.shape[1], w), lambda b, p, tbl: (b, 0, 0))
    out = lambda w: pl.BlockSpec((1, ppb * page, w), lambda b, p, tbl: (b, p, 0))
    widths = (LANES, LANES, 2 * LANES, W_MOBA, W_DSA_PAD)
    pools = (cache_nsa.reshape(cache_nsa.shape[:3] + (W_NSA,)),
             cache_moba.reshape(cache_moba.shape[:3] + (W_MOBA,)),
             cache_dsa.reshape(cache_dsa.shape[:3] + (W_DSA,)))
    return pl.pallas_call(
        functools.partial(_assemble_kernel, n_pages=n_pages, ppb=ppb),
        grid_spec=pltpu.PrefetchScalarGridSpec(
            num_scalar_prefetch=1,
            grid=(ns, n_blk),
            in_specs=[pool(w, j) for j in range(ppb) for w in (W_NSA, W_MOBA, W_DSA)]
            + [new(W_NSA), new(W_MOBA), new(W_DSA)],
            out_specs=[out(w) for w in widths],
        ),
        out_shape=[jax.ShapeDtypeStruct((ns, lpad, w), MXU) for w in widths],
        compiler_params=_params(("parallel", "arbitrary")),
        name="assemble_pages",
    )(table, *(pools * ppb), nsa_new, moba_new, dsa_new)


def _win_kernel(buf_ref, new_ref, winb_ref, wout_ref):
    buf = buf_ref[0, 0]
    new = new_ref[0]
    pad = winb_ref.shape[1] - buf.shape[0] - new.shape[0]
    winb_ref[0] = jnp.concatenate([buf, new, jnp.zeros((pad, buf.shape[1]), F32)], axis=0).astype(MXU)
    wout_ref[0] = jnp.concatenate([buf[new.shape[0]:], new], axis=0)


def _win_call(state_win, l, win_new, rows_pad):
    ns, w_buf = state_win.shape[1], state_win.shape[2]
    t_new = win_new.shape[1]
    return pl.pallas_call(
        _win_kernel,
        grid=(ns,),
        in_specs=[pl.BlockSpec((1, 1, w_buf, W_WIN), lambda b: (l, b, 0, 0)),
                  pl.BlockSpec((1, t_new, W_WIN), lambda b: (b, 0, 0))],
        out_specs=[pl.BlockSpec((1, rows_pad, W_WIN), lambda b: (b, 0, 0)),
                   pl.BlockSpec((1, w_buf, W_WIN), lambda b: (b, 0, 0))],
        out_shape=[jax.ShapeDtypeStruct((ns, rows_pad, W_WIN), MXU),
                   jax.ShapeDtypeStruct((ns, w_buf, W_WIN), F32)],
        compiler_params=_params(("parallel",)),
        name="window_buffer",
    )(state_win.reshape(state_win.shape[:3] + (W_WIN,)), win_new)


def _cmp_kernel(kx_ref, vx_ref, pe_ref, w1_ref, w2_ref, kc_ref, vc_ref):
    nch = kx_ref.shape[1]
    nout = kc_ref.shape[1]
    for j, (x_ref, o_ref) in enumerate(((kx_ref, kc_ref), (vx_ref, vc_ref))):
        x = x_ref[0].astype(F32)
        a = _dot((x + pe_ref[j, 0]).astype(MXU), w1_ref[j, 0])
        b = _dot((x + pe_ref[j, 1]).astype(MXU), w1_ref[j, 1])
        hid = a + pltpu.roll(b, nch - 1, axis=0)
        out = _dot(jax.nn.gelu(hid).astype(MXU), w2_ref[j])
        o_ref[0, 0:nch] = out.astype(o_ref.dtype)
        if nout > nch:
            o_ref[0, nch:nout] = jnp.zeros((nout - nch, LANES), o_ref.dtype)


def _cmp_call(kx, vx, pe_e, w1_e, w2_e, nout):
    ns, nch, wx = kx.shape
    full = lambda a: pl.BlockSpec(a.shape, lambda b: (0,) * a.ndim)
    return pl.pallas_call(
        _cmp_kernel,
        grid=(ns,),
        in_specs=[pl.BlockSpec((1, nch, wx), lambda b: (b, 0, 0))] * 2 + [full(pe_e), full(w1_e), full(w2_e)],
        out_specs=[pl.BlockSpec((1, nout, LANES), lambda b: (b, 0, 0))] * 2,
        out_shape=[jax.ShapeDtypeStruct((ns, nout, LANES), MXU)] * 2,
        compiler_params=_params(("parallel",)),
        name="nsa_compress",
    )(kx, vx, pe_e, w1_e, w2_e)


def _cmp_weights(cmp_pe_l, cmp_w1_l, cmp_w2_l):
    half_rows = CMP_LEN // 2
    kv = NSA_KV_HEADS
    w1 = cmp_w1_l.reshape(2, 2, half_rows, HEAD_DIM, CMP_HIDDEN)
    w1_e = jnp.zeros((2, 2, half_rows, kv, HEAD_DIM, kv, CMP_HIDDEN), F32)
    for k in range(kv):
        w1_e = w1_e.at[:, :, :, k, :, k, :].set(w1)
    w1_e = w1_e.reshape(2, 2, half_rows * kv * HEAD_DIM, kv * CMP_HIDDEN).astype(MXU)
    w2_e = jnp.zeros((2, kv, CMP_HIDDEN, kv, HEAD_DIM), F32)
    for k in range(kv):
        w2_e = w2_e.at[:, k, :, k, :].set(cmp_w2_l)
    w2_e = w2_e.reshape(2, kv * CMP_HIDDEN, kv * HEAD_DIM).astype(MXU)
    pe = cmp_pe_l.reshape(2, 2, half_rows, 1, HEAD_DIM)
    pe_e = jnp.broadcast_to(pe, (2, 2, half_rows, kv, HEAD_DIM)).reshape(2, 2, 1, half_rows * kv * HEAD_DIM)
    return pe_e, w1_e, w2_e


def _online_step(sm, vt, m_scr, l_scr, acc_scr, rows):
    m_prev = m_scr[rows, :]
    m_new = jnp.maximum(m_prev, jnp.max(sm, axis=-1, keepdims=True))
    alpha = jnp.exp(m_prev - m_new)
    p = jnp.exp(sm - m_new)
    l_scr[rows, :] = alpha * l_scr[rows, :] + jnp.sum(p, axis=-1, keepdims=True)
    acc_scr[rows, :] = alpha * acc_scr[rows, :] + _dot(p.astype(MXU), vt)
    m_scr[rows, :] = m_new


def _reset_state(m_scr, l_scr, acc_scr):
    m_scr[...] = jnp.full(m_scr.shape, M_INIT, F32)
    l_scr[...] = jnp.zeros(l_scr.shape, F32)
    acc_scr[...] = jnp.zeros(acc_scr.shape, F32)


def _biased_softmax(sm):
    m = jnp.maximum(jnp.max(sm, axis=-1, keepdims=True), M_INIT)
    e = jnp.exp(sm - m)
    den = jnp.sum(e, axis=-1, keepdims=True)
    return e * jnp.where(den > 0.0, 1.0 / den, 0.0)


def _add_bias(s, bias, qb):
    c = bias.shape[1]
    outer = bias.shape[0] // qb
    groups = s.shape[0] // bias.shape[0]
    return (s.reshape(outer, groups, qb, c) + bias.reshape(outer, 1, qb, c)).reshape(s.shape)


def _stack_heads(ref, tiles):
    parts = [ref[0, :, t * LANES:(t + 1) * LANES].astype(F32) for t in tiles]
    return jnp.concatenate(parts, axis=0).astype(MXU)


def _top_blocks(imp, t0, qb):
    qp = max(qb, LANES)
    kv = imp.shape[0] // qb
    blk = lax.broadcasted_iota(jnp.int32, (LANES, 1), 0)
    tq = t0 + lax.broadcasted_iota(jnp.int32, (1, qp), 1)
    tb = tq // SEL_BLOCK
    forced = (blk == 0) | (blk == tb) | (blk == tb - 1)
    admissible = blk * SEL_BLOCK <= tq
    vals = []
    for k in range(kv):
        part = imp[k * qb:(k + 1) * qb]
        if qp > qb:
            part = jnp.concatenate([part, jnp.zeros((qp - qb, LANES), F32)], axis=0)
        vals.append(jnp.where(forced, BIG, jnp.where(admissible, part.T, NEG)))
    val = jnp.concatenate(vals, axis=1)
    sel = jnp.zeros(val.shape, F32)
    for _ in range(SEL_TOPN):
        top = jnp.max(val, axis=0, keepdims=True)
        idx = jnp.min(jnp.where(val == top, blk, LANES), axis=0, keepdims=True)
        pick = blk == idx
        sel = jnp.where(pick, 1.0, sel)
        val = jnp.where(pick, BELOW_NEG, val)
    return jnp.concatenate([sel[:, k * qp:(k + 1) * qp].T[:qb] for k in range(kv)], axis=0)


def _nsa_kernel(qa_ref, qar_ref, gw_ref, kc_ref, vc_ref, kvs_ref, win_ref, ovl_ref, exp_ref, o_ref,
                m_scr, l_scr, acc_scr, *, qb, q0, ch, win_pos0, slab):
    t0 = q0 + pl.program_id(1) * qb
    tq = t0 + lax.broadcasted_iota(jnp.int32, (qb, 1), 0)
    ncp = kc_ref.shape[1]
    win_rows = win_ref.shape[1]
    g, kv = NSA_GROUP, NSA_KV_HEADS
    qa = _stack_heads(qa_ref, range(NSA_HEADS))
    qr = _stack_heads(qar_ref, range(NSA_HEADS))

    c_end = lax.broadcasted_iota(jnp.int32, (1, ncp), 1) * CMP_STRIDE + (CMP_LEN - 1)
    bias_c = jnp.where(c_end <= tq, 0.0, NEG)
    p_c = _biased_softmax(_add_bias(_dot_t(qa, kc_ref[0]), bias_c, qb))
    o_cmp = _dot(p_c.astype(MXU), vc_ref[0])
    p4 = p_c.reshape(kv, g, qb, ncp)
    p_sum = p4[:, 0]
    for j in range(1, g):
        p_sum = p_sum + p4[:, j]
    p_sum = p_sum.reshape(kv * qb, ncp)
    p_hi = p_sum.astype(MXU)
    p_lo = (p_sum - p_hi.astype(F32)).astype(MXU)
    imp = _dot(p_hi, ovl_ref[...]) + _dot(p_lo, ovl_ref[...])
    sel_b = _top_blocks(imp, t0, qb).astype(MXU)

    _reset_state(m_scr, l_scr, acc_scr)

    def sel_step(c, causal):
        off = pl.multiple_of(c * ch, ch)
        kt = kvs_ref[0, pl.ds(off, ch), 0:LANES]
        vt = kvs_ref[0, pl.ds(off, ch), LANES:2 * LANES]
        bias = (_dot(sel_b, exp_ref[:, pl.ds(off, ch)]) - 1.0) * BIG
        if causal:
            k_pos = off + lax.broadcasted_iota(jnp.int32, (1, ch), 1)
            bias = _add_bias(bias, jnp.where(k_pos <= tq, 0.0, NEG), qb)
        _online_step(_add_bias(_dot_t(qr, kt), bias, qb), vt, m_scr, l_scr, acc_scr, slice(None))

    c_diag = t0 // ch

    def sel_body(c, carry):
        sel_step(c, False)
        return carry

    lax.fori_loop(0, c_diag, sel_body, 0)
    sel_step(c_diag, True)
    o_sel = acc_scr[...] / l_scr[...]

    if win_rows > slab:
        w_start = pl.multiple_of(jnp.clip(t0 - WINDOW - win_pos0, 0, win_rows - slab), 16)
    else:
        w_start = 0
    w_pos = win_pos0 + w_start + lax.broadcasted_iota(jnp.int32, (1, slab), 1)
    dist = tq - w_pos
    bias_w = jnp.where((dist >= 0) & (dist <= WINDOW), 0.0, NEG)
    kt = win_ref[0, pl.ds(w_start, slab), 0:LANES]
    vt = win_ref[0, pl.ds(w_start, slab), LANES:2 * LANES]
    p_w = _biased_softmax(_add_bias(_dot_t(qr, kt), bias_w, qb))
    o_win = _dot(p_w.astype(MXU), vt)

    gate = jax.nn.sigmoid(gw_ref[0])
    for h in range(NSA_HEADS):
        r = slice(h * qb, (h + 1) * qb)
        o = (gate[:, 3 * h:3 * h + 1] * o_cmp[r] + gate[:, 3 * h + 1:3 * h + 2] * o_sel[r]
             + gate[:, 3 * h + 2:3 * h + 3] * o_win[r])
        o_ref[0, :, h * LANES:(h + 1) * LANES] = o.astype(o_ref.dtype)


def _nsa_call(qa, qar, gw, kc, vc, kvs, winb, ovl, expand, *, qb, q0, ch, win_pos0):
    ns, t, _ = qa.shape
    assert ch % qb == 0 and (q0 % qb == 0 or ch == kvs.shape[1])
    slab = min(-(-(WINDOW + qb) // LANES) * LANES, winb.shape[1])
    tok = lambda w: pl.BlockSpec((1, qb, w), lambda b, i: (b, i, 0))
    seq = lambda a: pl.BlockSpec((1,) + a.shape[1:], lambda b, i: (b, 0, 0))
    full = lambda a: pl.BlockSpec(a.shape, lambda b, i: (0,) * a.ndim)
    rows = NSA_HEADS * qb
    return pl.pallas_call(
        functools.partial(_nsa_kernel, qb=qb, q0=q0, ch=ch, win_pos0=win_pos0, slab=slab),
        grid=(ns, t // qb),
        in_specs=[tok(W_QA), tok(W_QA), tok(W_GW), seq(kc), seq(vc), seq(kvs), seq(winb), full(ovl), full(expand)],
        out_specs=tok(W_ONSA),
        out_shape=jax.ShapeDtypeStruct((ns, t, W_ONSA), MXU),
        scratch_shapes=[pltpu.VMEM((rows, 1), F32), pltpu.VMEM((rows, 1), F32), pltpu.VMEM((rows, LANES), F32)],
        compiler_params=_params(("parallel", "arbitrary")),
        name="nsa_mixer",
    )(qa, qar, gw, kc, vc, kvs, winb, ovl, expand)


def _nsa_tables(l_total, ncp, lpad):
    n_cmp = (l_total - CMP_LEN) // CMP_STRIDE + 1
    c = np.arange(ncp)[:, None]
    n = np.arange(LANES)[None, :]
    ovl = ((c * CMP_STRIDE < n * SEL_BLOCK + SEL_BLOCK) & (c * CMP_STRIDE + CMP_LEN - 1 >= n * SEL_BLOCK)
           & (c < n_cmp))
    s = np.arange(lpad)[None, :]
    expand = (s // SEL_BLOCK) == np.arange(LANES)[:, None]
    return jnp.asarray(ovl, MXU), jnp.asarray(expand, MXU)


def _moba_kernel(q_ref, kv_ref, o_ref, kmean_scr, m_scr, l_scr, acc_scr, *, qb, q0, ch):
    i = pl.program_id(1)
    lpad = kv_ref.shape[1]
    nh = MOBA_HEADS
    k_tiles = -(-nh // 2)
    blocks_per_chunk = ch // MOBA_BLOCK

    @pl.when(i == 0)
    def _():
        kmean_scr[...] = jnp.zeros(kmean_scr.shape, F32)
        ones = jnp.ones((8, MOBA_BLOCK), MXU)

        def mean_body(n, carry):
            off = pl.multiple_of(n * MOBA_BLOCK, MOBA_BLOCK)
            tot = _dot(ones, kv_ref[0, pl.ds(off, MOBA_BLOCK), 0:k_tiles * LANES])
            kmean_scr[pl.ds(n, 1), :] = tot[0:1] * (1.0 / MOBA_BLOCK)
            return carry

        lax.fori_loop(0, lpad // MOBA_BLOCK, mean_body, 0)

    t0 = q0 + i * qb
    tq = t0 + lax.broadcasted_iota(jnp.int32, (qb, 1), 0)
    own = tq // MOBA_BLOCK
    lane = lax.broadcasted_iota(jnp.int32, (1, LANES), 1)
    qs = [q_ref[0, :, h * LANES:(h + 1) * LANES] for h in range(nh)]
    kmean = kmean_scr[...].astype(MXU)
    vals = []
    for h in range(nh):
        s_blk = _dot_t(qs[h], kmean[:, (h // 2) * LANES:(h // 2 + 1) * LANES])
        vals.append(jnp.where(lane < own, s_blk, NEG))
    val = jnp.concatenate(vals, axis=0)
    picks = []
    for _ in range(MOBA_TOPK):
        top = jnp.max(val, axis=-1, keepdims=True)
        idx = jnp.min(jnp.where(val == top, lane, LANES), axis=-1, keepdims=True)
        picks.append(idx)
        val = jnp.where(lane == idx, BELOW_NEG, val)

    _reset_state(m_scr, l_scr, acc_scr)
    pos_in_block = lax.broadcasted_iota(jnp.int32, (1, MOBA_BLOCK), 1)

    def step(c, with_own):
        off = pl.multiple_of(c * ch, ch)
        for h in range(nh):
            r = slice(h * qb, (h + 1) * qb)
            kt = kv_ref[0, pl.ds(off, ch), (h // 2) * LANES:(h // 2 + 1) * LANES]
            v_tile = (nh + h) // 2
            vt = kv_ref[0, pl.ds(off, ch), v_tile * LANES:(v_tile + 1) * LANES]
            s = _dot_t(qs[h], kt)
            parts = []
            for j in range(blocks_per_chunk):
                n = c * blocks_per_chunk + j
                chosen = ((picks[0][r] == n) | (picks[1][r] == n) | (picks[2][r] == n)) & (n < own)
                bias = jnp.where(chosen, 0.0, NEG)
                if with_own:
                    k_pos = n * MOBA_BLOCK + pos_in_block
                    bias = jnp.maximum(bias, jnp.where((own == n) & (k_pos <= tq), 0.0, NEG))
                parts.append(s[:, j * MOBA_BLOCK:(j + 1) * MOBA_BLOCK] + bias)
            sm = parts[0] if len(parts) == 1 else jnp.concatenate(parts, axis=1)
            _online_step(sm, vt, m_scr, l_scr, acc_scr, r)

    c_diag = t0 // ch

    def body(c, carry):
        step(c, False)
        return carry

    lax.fori_loop(0, c_diag, body, 0)
    step(c_diag, True)
    o = acc_scr[...] / l_scr[...]
    for h in range(nh):
        o_ref[0, :, h * LANES:(h + 1) * LANES] = o[h * qb:(h + 1) * qb].astype(o_ref.dtype)


def _moba_call(q, kv, *, qb, q0, ch):
    ns, t, _ = q.shape
    assert ch % MOBA_BLOCK == 0 and ch % qb == 0 and (q0 % qb == 0 or ch == kv.shape[1])
    rows = MOBA_HEADS * qb
    k_tiles = -(-MOBA_HEADS // 2)
    return pl.pallas_call(
        functools.partial(_moba_kernel, qb=qb, q0=q0, ch=ch),
        grid=(ns, t // qb),
        in_specs=[pl.BlockSpec((1, qb, W_QB), lambda b, i: (b, i, 0)),
                  pl.BlockSpec((1,) + kv.shape[1:], lambda b, i: (b, 0, 0))],
        out_specs=pl.BlockSpec((1, qb, W_OMOBA), lambda b, i: (b, i, 0)),
        out_shape=jax.ShapeDtypeStruct((ns, t, W_OMOBA), MXU),
        scratch_shapes=[pltpu.VMEM((LANES, k_tiles * LANES), F32),
                        pltpu.VMEM((rows, 1), F32), pltpu.VMEM((rows, 1), F32), pltpu.VMEM((rows, LANES), F32)],
        compiler_params=_params(("parallel", "arbitrary")),
        name="moba_mixer",
    )(q, kv)


def _online_step_t(sm, vt_t, m_scr, l_scr, acc_scr, cols):
    m_prev = m_scr[:, cols]
    m_new = jnp.maximum(m_prev, jnp.max(sm, axis=0, keepdims=True))
    alpha = jnp.exp(m_prev - m_new)
    p = jnp.exp(sm - m_new)
    l_scr[:, cols] = alpha * l_scr[:, cols] + jnp.sum(p, axis=0, keepdims=True)
    acc_scr[:, cols] = alpha * acc_scr[:, cols] + _dot(vt_t, p.astype(MXU))
    m_scr[:, cols] = m_new


def _softmax_t(sm):
    m = jnp.maximum(jnp.max(sm, axis=0, keepdims=True), M_INIT)
    e = jnp.exp(sm - m)
    den = jnp.sum(e, axis=0, keepdims=True)
    return e * jnp.where(den > 0.0, 1.0 / den, 0.0)


def _transpose_tile(ref, t):
    return ref[0, :, t * LANES:(t + 1) * LANES].astype(F32).T.astype(MXU)


def _tile_lanes(x, n):
    return x if n == 1 else jnp.concatenate([x] * n, axis=1)


def _chunk_pairs(n, scores, update):
    def pair(c2, carry):
        a = scores(2 * c2)
        b = scores(2 * c2 + 1)
        update(*a)
        update(*b)
        return carry

    lax.fori_loop(0, n // 2, pair, 0)

    @pl.when(n % 2 == 1)
    def _():
        update(*scores(n - 1))


def _moba_t_kernel(q_ref, k_ref, vt_ref, o_ref, kmean_scr, chosen_scr, m_scr, l_scr, acc_scr, *, qb):
    i = pl.program_id(1)
    lpad = k_ref.shape[1]
    nh = MOBA_HEADS
    k_tiles = -(-nh // 2)

    @pl.when(i == 0)
    def _():
        kmean_scr[...] = jnp.zeros(kmean_scr.shape, F32)
        ones = jnp.ones((8, MOBA_BLOCK), MXU)

        def mean_body(n, carry):
            off = pl.multiple_of(n * MOBA_BLOCK, MOBA_BLOCK)
            tot = _dot(ones, k_ref[0, pl.ds(off, MOBA_BLOCK), 0:k_tiles * LANES])
            kmean_scr[pl.ds(n, 1), :] = tot[0:1] * (1.0 / MOBA_BLOCK)
            return carry

        lax.fori_loop(0, lpad // MOBA_BLOCK, mean_body, 0)

    t0 = i * qb
    tq = t0 + lax.broadcasted_iota(jnp.int32, (1, qb), 1)
    own = tq // MOBA_BLOCK
    blk = lax.broadcasted_iota(jnp.int32, (LANES, 1), 0)
    q_t = [_transpose_tile(q_ref, h) for h in range(nh)]
    kmean = kmean_scr[...].astype(MXU)
    past = blk < own
    val = jnp.concatenate(
        [jnp.where(past, _dot(kmean[:, (h // 2) * LANES:(h // 2 + 1) * LANES], q_t[h]), NEG) for h in range(nh)],
        axis=1)
    past_all = _tile_lanes(past, nh)
    chosen = jnp.zeros(val.shape, F32)
    for _ in range(MOBA_TOPK):
        top = jnp.max(val, axis=0, keepdims=True)
        idx = jnp.min(jnp.where(val == top, blk, LANES), axis=0, keepdims=True)
        pick = blk == idx
        chosen = jnp.where(pick & past_all, 1.0, chosen)
        val = jnp.where(pick, BELOW_NEG, val)
    chosen_scr[...] = (chosen - 1.0) * BIG

    _reset_state(m_scr, l_scr, acc_scr)
    k_in_blk = lax.broadcasted_iota(jnp.int32, (MOBA_BLOCK, 1), 0)

    def scores(n, with_own=False):
        off = pl.multiple_of(n * MOBA_BLOCK, MOBA_BLOCK)
        bias_rows = chosen_scr[pl.ds(n, 1), :]
        if with_own:
            own_bias = jnp.where((own == n) & (off + k_in_blk <= tq), 0.0, NEG)
        out = []
        for h in range(nh):
            c = slice(h * qb, (h + 1) * qb)
            kt = k_ref[0, pl.ds(off, MOBA_BLOCK), (h // 2) * LANES:(h // 2 + 1) * LANES]
            v_tile = (nh + h) // 2 - (nh // 2)
            vt_t = vt_ref[0, v_tile * LANES:(v_tile + 1) * LANES, pl.ds(off, MOBA_BLOCK)]
            bias = jnp.maximum(bias_rows[:, c], own_bias) if with_own else bias_rows[:, c]
            out += [_dot(kt, q_t[h]) + bias, vt_t]
        return out

    def update(*sv):
        for h in range(nh):
            _online_step_t(sv[2 * h], sv[2 * h + 1], m_scr, l_scr, acc_scr, slice(h * qb, (h + 1) * qb))

    n_own = t0 // MOBA_BLOCK
    _chunk_pairs(n_own, scores, update)
    update(*scores(n_own, True))
    o_t = acc_scr[...] / l_scr[...]
    for h in range(nh):
        o_ref[0, :, h * LANES:(h + 1) * LANES] = o_t[:, h * qb:(h + 1) * qb].T.astype(o_ref.dtype)


def _moba_t_call(q, k, vt, *, qb):
    ns, t, _ = q.shape
    assert MOBA_BLOCK % qb == 0 and qb % LANES == 0
    cols = MOBA_HEADS * qb
    k_tiles = -(-MOBA_HEADS // 2)
    return pl.pallas_call(
        functools.partial(_moba_t_kernel, qb=qb),
        grid=(ns, t // qb),
        in_specs=[pl.BlockSpec((1, qb, W_QB), lambda b, i: (b, i, 0)),
                  pl.BlockSpec((1,) + k.shape[1:], lambda b, i: (b, 0, 0)),
                  pl.BlockSpec((1,) + vt.shape[1:], lambda b, i: (b, 0, 0))],
        out_specs=pl.BlockSpec((1, qb, W_OMOBA), lambda b, i: (b, i, 0)),
        out_shape=jax.ShapeDtypeStruct((ns, t, W_OMOBA), MXU),
        scratch_shapes=[pltpu.VMEM((LANES, k_tiles * LANES), F32), pltpu.VMEM((LANES, cols), F32),
                        pltpu.VMEM((1, cols), F32), pltpu.VMEM((1, cols), F32), pltpu.VMEM((LANES, cols), F32)],
        compiler_params=_params(("parallel", "arbitrary")),
        name="moba_mixer_t",
    )(q, k, vt)


def _nsa_t_kernel(qa_ref, qar_ref, gw_ref, kc_ref, vct_ref, kvs_ref, vst_ref, win_ref, wvt_ref, ovlt_ref, expt_ref,
                  o_ref, m_scr, l_scr, acc_scr, *, qb, ch, slab):
    t0 = pl.program_id(1) * qb
    tq = t0 + lax.broadcasted_iota(jnp.int32, (1, qb), 1)
    ncp = kc_ref.shape[1]
    win_rows = win_ref.shape[1]
    g, kv, nh = NSA_GROUP, NSA_KV_HEADS, NSA_HEADS
    qa_t = jnp.concatenate([_transpose_tile(qa_ref, h) for h in range(nh)], axis=1)
    qr_t = jnp.concatenate([_transpose_tile(qar_ref, h) for h in range(nh)], axis=1)

    c_end = lax.broadcasted_iota(jnp.int32, (ncp, 1), 0) * CMP_STRIDE + (CMP_LEN - 1)
    bias_c = jnp.where(c_end <= tq, 0.0, NEG)
    p_c = _softmax_t(_dot(kc_ref[0], qa_t) + _tile_lanes(bias_c, nh))
    o_cmp = _dot(vct_ref[0], p_c.astype(MXU))
    sums = []
    for k in range(kv):
        acc = p_c[:, (k * g) * qb:(k * g + 1) * qb]
        for j in range(1, g):
            acc = acc + p_c[:, (k * g + j) * qb:(k * g + j + 1) * qb]
        sums.append(acc)
    p_sum = jnp.concatenate(sums, axis=1)
    p_hi = p_sum.astype(MXU)
    p_lo = (p_sum - p_hi.astype(F32)).astype(MXU)
    imp = _dot(ovlt_ref[...], p_hi) + _dot(ovlt_ref[...], p_lo)
    blk = lax.broadcasted_iota(jnp.int32, (LANES, 1), 0)
    tb = tq // SEL_BLOCK
    forced = _tile_lanes((blk == 0) | (blk == tb) | (blk == tb - 1), kv)
    admissible = _tile_lanes(blk * SEL_BLOCK <= tq, kv)
    val = jnp.where(forced, BIG, jnp.where(admissible, imp, NEG))
    sel = jnp.zeros(val.shape, F32)
    for _ in range(SEL_TOPN):
        top = jnp.max(val, axis=0, keepdims=True)
        idx = jnp.min(jnp.where(val == top, blk, LANES), axis=0, keepdims=True)
        pick = blk == idx
        sel = jnp.where(pick, 1.0, sel)
        val = jnp.where(pick, BELOW_NEG, val)
    sel_b = sel.astype(MXU)

    _reset_state(m_scr, l_scr, acc_scr)

    def scores(c, causal=False):
        off = pl.multiple_of(c * ch, ch)
        kt = kvs_ref[0, pl.ds(off, ch), 0:LANES]
        bias = (_dot(expt_ref[pl.ds(off, ch), :], sel_b) - 1.0) * BIG
        if causal:
            k_pos = off + lax.broadcasted_iota(jnp.int32, (ch, 1), 0)
            bias = bias + _tile_lanes(jnp.where(k_pos <= tq, 0.0, NEG), kv)
        bias = jnp.concatenate([bias[:, k * qb:(k + 1) * qb] for k in range(kv) for _ in range(g)], axis=1)
        return _dot(kt, qr_t) + bias, vst_ref[0, :, pl.ds(off, ch)]

    def update(sm, vt_t):
        _online_step_t(sm, vt_t, m_scr, l_scr, acc_scr, slice(None))

    c_diag = t0 // ch
    _chunk_pairs(c_diag, scores, update)
    update(*scores(c_diag, True))
    o_sel = acc_scr[...] / l_scr[...]

    w_start = pl.multiple_of(jnp.clip(t0 - WINDOW, 0, win_rows - slab), LANES)
    w_pos = w_start + lax.broadcasted_iota(jnp.int32, (slab, 1), 0)
    dist = tq - w_pos
    bias_w = jnp.where((dist >= 0) & (dist <= WINDOW), 0.0, NEG)
    kt = win_ref[0, pl.ds(w_start, slab), 0:LANES]
    p_w = _softmax_t(_dot(kt, qr_t) + _tile_lanes(bias_w, nh))
    o_win = _dot(wvt_ref[0, :, pl.ds(w_start, slab)], p_w.astype(MXU))

    gate_t = jax.nn.sigmoid(gw_ref[0]).T
    for h in range(nh):
        c = slice(h * qb, (h + 1) * qb)
        o = (gate_t[3 * h:3 * h + 1] * o_cmp[:, c] + gate_t[3 * h + 1:3 * h + 2] * o_sel[:, c]
             + gate_t[3 * h + 2:3 * h + 3] * o_win[:, c])
        o_ref[0, :, h * LANES:(h + 1) * LANES] = o.T.astype(o_ref.dtype)


def _nsa_t_call(qa, qar, gw, kc, vc, kvs, winb, ovl, *, qb, ch):
    ns, t, _ = qa.shape
    lpad = kvs.shape[1]
    assert ch % qb == 0 and qb % LANES == 0 and lpad == t
    slab = WINDOW + qb
    swap = lambda a: jnp.swapaxes(a, 1, 2)
    vct, vst, wvt = swap(vc), swap(kvs[:, :, LANES:]), swap(winb[:, :, LANES:])
    ovlt = ovl.T
    expt = jnp.asarray((np.arange(lpad)[:, None] // SEL_BLOCK) == np.arange(LANES)[None, :], MXU)
    tok = lambda w: pl.BlockSpec((1, qb, w), lambda b, i: (b, i, 0))
    seq = lambda a: pl.BlockSpec((1,) + a.shape[1:], lambda b, i: (b, 0, 0))
    full = lambda a: pl.BlockSpec(a.shape, lambda b, i: (0,) * a.ndim)
    cols = NSA_HEADS * qb
    return pl.pallas_call(
        functools.partial(_nsa_t_kernel, qb=qb, ch=ch, slab=slab),
        grid=(ns, t // qb),
        in_specs=[tok(W_QA), tok(W_QA), tok(W_GW), seq(kc), seq(vct), seq(kvs), seq(vst), seq(winb), seq(wvt),
                  full(ovlt), full(expt)],
        out_specs=tok(W_ONSA),
        out_shape=jax.ShapeDtypeStruct((ns, t, W_ONSA), MXU),
        scratch_shapes=[pltpu.VMEM((1, cols), F32), pltpu.VMEM((1, cols), F32), pltpu.VMEM((LANES, cols), F32)],
        compiler_params=_params(("parallel", "arbitrary")),
        name="nsa_mixer_t",
    )(qa, qar, gw, kc, vct, kvs, vst, winb, wvt, ovlt, expt)


def _dsa_t_kernel(qc_ref, qi_ref, gw_ref, kv_ref, kvt_ref, o_ref, key_scr, cut_scr, m_scr, l_scr, acc_scr,
                  *, qb, ch, k_top):
    t0 = pl.program_id(1) * qb
    tq = t0 + lax.broadcasted_iota(jnp.int32, (1, qb), 1)
    lpad = kv_ref.shape[1]
    n_ch = t0 // ch + 1
    gw_t = gw_ref[0].T
    w_idx = [gw_t[GW_WI + h:GW_WI + h + 1] for h in range(IDX_HEADS)]
    qi_t = [_transpose_tile(qi_ref, h) for h in range(IDX_HEADS)]
    slab = 64
    k_in_ch = lax.broadcasted_iota(jnp.int32, (ch, 1), 0)

    def score_body(c, carry):
        off = pl.multiple_of(c * ch, ch)
        kt = kv_ref[0, pl.ds(off, ch), LANES:2 * LANES]
        sc = w_idx[0] * jnp.maximum(_dot(kt, qi_t[0]), 0.0)
        for h in range(1, IDX_HEADS):
            sc = sc + w_idx[h] * jnp.maximum(_dot(kt, qi_t[h]), 0.0)
        sc = jnp.where(sc == 0.0, 0.0, sc)
        bits = lax.bitcast_convert_type(sc, jnp.int32)
        key = bits ^ (lax.shift_right_arithmetic(bits, 31) & 0x7FFFFFFF)
        key_scr[pl.ds(off, ch), :] = jnp.where(off + k_in_ch <= tq, key, INT_MIN)
        return carry

    lax.fori_loop(0, n_ch, score_body, 0)

    def count(pred):
        def body(c, acc):
            off = pl.multiple_of(c * ch, ch)
            for j in range(ch // slab):
                acc = acc + jnp.where(pred(key_scr[pl.ds(off + j * slab, slab), :], off + j * slab), 1.0, 0.0)
            return acc
        acc = lax.fori_loop(0, n_ch, body, jnp.zeros((slab, qb), F32))
        return jnp.sum(acc, axis=0, keepdims=True)

    few = tq + 1 <= k_top
    bits_per_check = 4

    def bit_cond(st):
        it, _, n_sel = st
        open_cols = jnp.where(few | (n_sel == k_top), 0.0, 1.0)
        return (it < 32) & (jnp.max(open_cols) > 0.0)

    def bit_body(st):
        it, prefix, n_sel = st
        for j in range(bits_per_check):
            cand = prefix | lax.shift_left(jnp.int32(1), 31 - (it + j))
            cand_key = cand ^ INT_MIN
            cnt = count(lambda k, off: k >= cand_key)
            take = (cnt >= k_top) & (n_sel != k_top)
            prefix, n_sel = jnp.where(take, cand, prefix), jnp.where(take, cnt, n_sel)
        return it + bits_per_check, prefix, n_sel

    _, prefix, n_sel = lax.while_loop(
        bit_cond, bit_body,
        (jnp.int32(0), jnp.zeros((1, qb), jnp.int32), jnp.full((1, qb), float(2 * lpad), F32)))
    thr = jnp.where(few, INT_MIN + 1, jnp.maximum(prefix ^ INT_MIN, INT_MIN + 1))

    overflow = (n_sel > k_top) & jnp.logical_not(few)
    cut_scr[...] = jnp.full((1, qb), lpad, jnp.int32)

    @pl.when(jnp.max(jnp.where(overflow, 1.0, 0.0)) > 0.0)
    def _():
        need = k_top - count(lambda k, off: k > thr)

        def cut_body(it, lo_hi):
            lo, hi = lo_hi
            mid = (lo + hi) // 2
            cnt = count(lambda k, off: (k == thr) & (off + k_in_ch[0:slab] <= mid))
            ok = cnt >= need
            return jnp.where(ok, lo, mid + 1), jnp.where(ok, mid, hi)
        lo, _ = lax.fori_loop(0, max(1, math.ceil(math.log2(lpad))), cut_body,
                              (jnp.zeros((1, qb), jnp.int32), jnp.full((1, qb), lpad - 1, jnp.int32)))
        cut_scr[...] = jnp.where(overflow, lo, lpad)

    cut = cut_scr[...]

    q_t = jnp.concatenate([_transpose_tile(qc_ref, h) for h in range(DSA_HEADS)], axis=1)
    _reset_state(m_scr, l_scr, acc_scr)

    def scores(c):
        off = pl.multiple_of(c * ch, ch)
        kt = kv_ref[0, pl.ds(off, ch), 0:LANES]
        key = key_scr[pl.ds(off, ch), :]
        bias = jnp.where((key > thr) | ((key == thr) & (off + k_in_ch <= cut)), 0.0, NEG)
        return _dot(kt, q_t) + _tile_lanes(bias, DSA_HEADS), kvt_ref[0, :, pl.ds(off, ch)]

    def update(sm, vt_t):
        _online_step_t(sm, vt_t, m_scr, l_scr, acc_scr, slice(None))

    _chunk_pairs(n_ch, scores, update)
    o_t = acc_scr[...] / l_scr[...]
    for h in range(DSA_HEADS):
        o_ref[0, :, h * LANES:(h + 1) * LANES] = o_t[:, h * qb:(h + 1) * qb].T.astype(o_ref.dtype)


def _dsa_t_call(qc, qi, gw, kv, kvt, *, qb, ch, k_top):
    ns, t, _ = qc.shape
    lpad = kv.shape[1]
    assert ch % qb == 0 and qb % LANES == 0 and lpad == t
    cols = DSA_HEADS * qb
    tok = lambda w: pl.BlockSpec((1, qb, w), lambda b, i: (b, i, 0))
    seq = lambda a: pl.BlockSpec((1,) + a.shape[1:], lambda b, i: (b, 0, 0))
    return pl.pallas_call(
        functools.partial(_dsa_t_kernel, qb=qb, ch=ch, k_top=k_top),
        grid=(ns, t // qb),
        in_specs=[tok(W_QC), tok(W_QI), tok(W_GW), seq(kv), seq(kvt)],
        out_specs=tok(W_ODSA),
        out_shape=jax.ShapeDtypeStruct((ns, t, W_ODSA), MXU),
        scratch_shapes=[pltpu.VMEM((lpad, qb), jnp.int32), pltpu.VMEM((1, qb), jnp.int32),
                        pltpu.VMEM((1, cols), F32), pltpu.VMEM((1, cols), F32), pltpu.VMEM((LANES, cols), F32)],
        compiler_params=_params(("parallel", "arbitrary")),
        name="dsa_mixer_t",
    )(qc, qi, gw, kv, kvt)


def _dsa_kernel(qc_ref, qi_ref, gw_ref, kv_ref, o_ref, key_scr, cut_scr, m_scr, l_scr, acc_scr,
                *, qb, q0, ch, k_top):
    t0 = q0 + pl.program_id(1) * qb
    tq = t0 + lax.broadcasted_iota(jnp.int32, (qb, 1), 0)
    lpad = kv_ref.shape[1]
    n_ch = t0 // ch + 1
    gw = gw_ref[0]
    w_idx = [gw[:, GW_WI + h:GW_WI + h + 1] for h in range(IDX_HEADS)]
    q_idx = [qi_ref[0, :, h * LANES:(h + 1) * LANES] for h in range(IDX_HEADS)]
    lane_ch = lax.broadcasted_iota(jnp.int32, (1, ch), 1)

    def score_body(c, carry):
        off = pl.multiple_of(c * ch, ch)
        kt = kv_ref[0, pl.ds(off, ch), LANES:2 * LANES]
        sc = w_idx[0] * jnp.maximum(_dot_t(q_idx[0], kt), 0.0)
        for h in range(1, IDX_HEADS):
            sc = sc + w_idx[h] * jnp.maximum(_dot_t(q_idx[h], kt), 0.0)
        sc = jnp.where(sc == 0.0, 0.0, sc)
        bits = lax.bitcast_convert_type(sc, jnp.int32)
        key = bits ^ (lax.shift_right_arithmetic(bits, 31) & 0x7FFFFFFF)
        key_scr[:, pl.ds(off, ch)] = jnp.where(off + lane_ch <= tq, key, INT_MIN)
        return carry

    lax.fori_loop(0, n_ch, score_body, 0)

    def count(pred):
        def body(c, acc):
            off = pl.multiple_of(c * ch, ch)
            hit = jnp.where(pred(key_scr[:, pl.ds(off, ch)], off), 1.0, 0.0)
            part = hit[:, 0:LANES]
            for j in range(1, ch // LANES):
                part = part + hit[:, j * LANES:(j + 1) * LANES]
            return acc + part
        acc = lax.fori_loop(0, n_ch, body, jnp.zeros((qb, LANES), F32))
        return jnp.sum(acc, axis=-1, keepdims=True)

    few = tq + 1 <= k_top

    bits_per_check = 4

    def bit_cond(st):
        it, _, n_sel = st
        open_rows = jnp.where(few | (n_sel == k_top), 0.0, 1.0)
        return (it < 32) & (jnp.max(open_rows) > 0.0)

    def bit_body(st):
        it, prefix, n_sel = st
        for j in range(bits_per_check):
            cand = prefix | lax.shift_left(jnp.int32(1), 31 - (it + j))
            cand_key = cand ^ INT_MIN
            cnt = count(lambda k, off: k >= cand_key)
            take = (cnt >= k_top) & (n_sel != k_top)
            prefix, n_sel = jnp.where(take, cand, prefix), jnp.where(take, cnt, n_sel)
        return it + bits_per_check, prefix, n_sel

    _, prefix, n_sel = lax.while_loop(
        bit_cond, bit_body,
        (jnp.int32(0), jnp.zeros((qb, 1), jnp.int32), jnp.full((qb, 1), float(2 * lpad), F32)))
    thr = jnp.where(few, INT_MIN + 1, jnp.maximum(prefix ^ INT_MIN, INT_MIN + 1))

    overflow = (n_sel > k_top) & jnp.logical_not(few)
    cut_scr[...] = jnp.full((qb, 1), lpad, jnp.int32)

    @pl.when(jnp.max(jnp.where(overflow, 1.0, 0.0)) > 0.0)
    def _():
        need = k_top - count(lambda k, off: k > thr)

        def cut_body(it, lo_hi):
            lo, hi = lo_hi
            mid = (lo + hi) // 2
            cnt = count(lambda k, off: (k == thr) & (off + lane_ch <= mid))
            ok = cnt >= need
            return jnp.where(ok, lo, mid + 1), jnp.where(ok, mid, hi)
        lo, _ = lax.fori_loop(0, max(1, math.ceil(math.log2(lpad))), cut_body,
                              (jnp.zeros((qb, 1), jnp.int32), jnp.full((qb, 1), lpad - 1, jnp.int32)))
        cut_scr[...] = jnp.where(overflow, lo, lpad)

    cut = cut_scr[...]

    q = _stack_heads(qc_ref, range(DSA_HEADS))
    _reset_state(m_scr, l_scr, acc_scr)

    def att_body(c, carry):
        off = pl.multiple_of(c * ch, ch)
        kt = kv_ref[0, pl.ds(off, ch), 0:LANES]
        key = key_scr[:, pl.ds(off, ch)]
        bias = jnp.where((key > thr) | ((key == thr) & (off + lane_ch <= cut)), 0.0, NEG)
        _online_step(_add_bias(_dot_t(q, kt), bias, qb), kt, m_scr, l_scr, acc_scr, slice(None))
        return carry

    lax.fori_loop(0, n_ch, att_body, 0)
    o = acc_scr[...] / l_scr[...]
    for h in range(DSA_HEADS):
        o_ref[0, :, h * LANES:(h + 1) * LANES] = o[h * qb:(h + 1) * qb].astype(o_ref.dtype)


def _dsa_call(qc, qi, gw, kv, *, qb, q0, ch, k_top):
    ns, t, _ = qc.shape
    lpad = kv.shape[1]
    assert ch % qb == 0 and (q0 % qb == 0 or ch == lpad)
    rows = DSA_HEADS * qb
    tok = lambda w: pl.BlockSpec((1, qb, w), lambda b, i: (b, i, 0))
    return pl.pallas_call(
        functools.partial(_dsa_kernel, qb=qb, q0=q0, ch=ch, k_top=k_top),
        grid=(ns, t // qb),
        in_specs=[tok(W_QC), tok(W_QI), tok(W_GW), pl.BlockSpec((1,) + kv.shape[1:], lambda b, i: (b, 0, 0))],
        out_specs=tok(W_ODSA),
        out_shape=jax.ShapeDtypeStruct((ns, t, W_ODSA), MXU),
        scratch_shapes=[pltpu.VMEM((qb, lpad), jnp.int32), pltpu.VMEM((qb, 1), jnp.int32),
                        pltpu.VMEM((rows, 1), F32), pltpu.VMEM((rows, 1), F32), pltpu.VMEM((rows, LANES), F32)],
        compiler_params=_params(("parallel", "arbitrary")),
        name="dsa_mixer",
    )(qc, qi, gw, kv)


def _post_attn_kernel(oa_ref, ob_ref, oc_ref, x_ref, g1_ref, sh_ref, sc_ref, wo_ref, lng_ref, lnb_ref,
                      wr_ref, br_ref, x1_ref, h2_ref, gate_ref):
    x = x_ref[...]
    s, r, d = x.shape
    att = None
    off = 0
    for o_ref in (oa_ref, ob_ref, oc_ref):
        w = o_ref.shape[2]
        part = _dot(o_ref[...].reshape(s * r, w), wo_ref[off:off + w, :])
        att = part if att is None else att + part
        off += w
    att = att.reshape(s, r, d)
    x1 = _ln(ALPHA * x + g1_ref[...] * att) * lng_ref[...] + lnb_ref[...]
    x1_ref[...] = x1
    h2 = _ln(x1) * (1.0 + sc_ref[...]) + sh_ref[...]
    h2b = h2.reshape(s * r, d).astype(MXU)
    h2_ref[...] = h2b.reshape(s, r, d)

    logit = _dot(h2b, wr_ref[...]) + br_ref[...]
    lane = lax.broadcasted_iota(jnp.int32, logit.shape, 1)
    is_g = (lane >= N_EXPERTS) & (lane < N_EXPERTS + N_GROUPS)
    lg = jnp.where(is_g, logit, NEG)
    mg = jnp.max(lg, axis=-1, keepdims=True)
    g_lane = jnp.min(jnp.where(lg == mg, lane, LANES), axis=-1, keepdims=True)
    g_den = jnp.sum(jnp.where(is_g, jnp.exp(lg - mg), 0.0), axis=-1, keepdims=True)
    g_w = 1.0 / g_den
    g_idx = g_lane - N_EXPERTS
    in_grp = (lane >= g_idx * EXPERTS_PER_GROUP) & (lane < (g_idx + 1) * EXPERTS_PER_GROUP)
    le = jnp.where(in_grp, logit, NEG)
    me = jnp.max(le, axis=-1, keepdims=True)
    ee = jnp.where(in_grp, jnp.exp(le - me), 0.0)
    pe = ee / jnp.sum(ee, axis=-1, keepdims=True)
    p1 = jnp.max(pe, axis=-1, keepdims=True)
    i1 = jnp.min(jnp.where((pe == p1) & in_grp, lane, LANES), axis=-1, keepdims=True)
    rest = jnp.where(in_grp & (lane != i1), pe, -1.0)
    p2 = jnp.max(rest, axis=-1, keepdims=True)
    i2 = jnp.min(jnp.where(rest == p2, lane, LANES), axis=-1, keepdims=True)
    tot = p1 + p2
    gate = jnp.where(lane == i1, g_w * p1 / tot, jnp.where(lane == i2, g_w * p2 / tot, 0.0))
    gate_ref[...] = gate.reshape(s, r, LANES)


def _post_attn_call(o_parts, x, g1, sh2, sc2, wo, lng, lnb, wr, br, s_blk, r_blk):
    ns, rt, _ = x.shape
    tok = lambda w: pl.BlockSpec((s_blk, r_blk, w), lambda i, j: (i, j, 0))
    mod = pl.BlockSpec((s_blk, 1, D_MODEL), lambda i, j: (i, 0, 0))
    full = lambda a: pl.BlockSpec(a.shape, lambda i, j: (0,) * a.ndim)
    return pl.pallas_call(
        _post_attn_kernel,
        grid=(ns // s_blk, rt // r_blk),
        in_specs=[tok(o.shape[2]) for o in o_parts] + [tok(D_MODEL), mod, mod, mod,
                                                       full(wo), full(lng), full(lnb), full(wr), full(br)],
        out_specs=[tok(D_MODEL), tok(D_MODEL), tok(LANES)],
        out_shape=[jax.ShapeDtypeStruct((ns, rt, D_MODEL), F32),
                   jax.ShapeDtypeStruct((ns, rt, D_MODEL), MXU),
                   jax.ShapeDtypeStruct((ns, rt, LANES), F32)],
        compiler_params=_params(("parallel", "parallel")),
        name="wout_ln_router",
    )(*o_parts, x, g1, sh2, sc2, wo, lng, lnb, wr, br)


def _moe_kernel(h_ref, gate_ref, x_ref, g2_ref, wi_ref, wo_ref, lng_ref, lnb_ref, out_ref, acc_ref):
    e = pl.program_id(2)
    s, r, d = x_ref.shape

    @pl.when(e == 0)
    def _():
        acc_ref[...] = jnp.zeros_like(acc_ref)

    h = h_ref[...].reshape(s * r, d)
    ab = _dot(h, wi_ref[0])
    a, b = ab[:, :D_EXPERT], ab[:, D_EXPERT:]
    gate = gate_ref[...].reshape(s * r, LANES)
    lane = lax.broadcasted_iota(jnp.int32, gate.shape, 1)
    gcol = jnp.sum(jnp.where(lane == e, gate, 0.0), axis=-1, keepdims=True)
    hid = (a * jax.nn.sigmoid(a) * b * gcol).astype(MXU)
    acc_ref[...] += _dot(hid, wo_ref[0])

    @pl.when(e == N_EXPERTS - 1)
    def _():
        f = acc_ref[...].reshape(s, r, d)
        out_ref[...] = _ln(ALPHA * x_ref[...] + g2_ref[...] * f) * lng_ref[...] + lnb_ref[...]


def _moe_call(h2, gate, x1, g2, wi, wo, lng, lnb, s_blk, r_blk):
    ns, rt, _ = x1.shape
    tok = lambda w: pl.BlockSpec((s_blk, r_blk, w), lambda i, j, e: (i, j, 0))
    mod = pl.BlockSpec((s_blk, 1, D_MODEL), lambda i, j, e: (i, 0, 0))
    vec = pl.BlockSpec((1, 1, D_MODEL), lambda i, j, e: (0, 0, 0))
    return pl.pallas_call(
        _moe_kernel,
        grid=(ns // s_blk, rt // r_blk, N_EXPERTS),
        in_specs=[tok(D_MODEL), tok(LANES), tok(D_MODEL), mod,
                  pl.BlockSpec((1, D_MODEL, 2 * D_EXPERT), lambda i, j, e: (e, 0, 0)),
                  pl.BlockSpec((1, D_EXPERT, D_MODEL), lambda i, j, e: (e, 0, 0)),
                  vec, vec],
        out_specs=tok(D_MODEL),
        out_shape=jax.ShapeDtypeStruct((ns, rt, D_MODEL), F32),
        scratch_shapes=[pltpu.VMEM((s_blk * r_blk, D_MODEL), F32)],
        compiler_params=_params(("parallel", "parallel", "arbitrary")),
        name="moe_ln",
    )(h2, gate, x1, g2, wi, wo, lng, lnb)


def _mixers(pr, cmp_w, past, cfg):
    (nsa, win, moba, dsa, gw, qa, qar, qb_, qc, qi, kc, vc, kvs, winb, mobab, dsab) = pr
    ns, t, _ = qa.shape
    if past is None:
        l_total, lpad, win_pos0 = t, t, 0
        win_out = win[:, -min(WINDOW, t):]
    else:
        cache_nsa, cache_moba, cache_dsa, state_win, table, l = past
        past_len = table.shape[1] * cache_nsa.shape[2]
        l_total = past_len + t
        lpad = -(-l_total // MOBA_BLOCK) * MOBA_BLOCK
        kc, vc, kvs, mobab, dsab = _assemble_call(table, cache_nsa, cache_moba, cache_dsa, l,
                                                 nsa, moba, dsa, lpad)
        w_buf = state_win.shape[2]
        rows_pad = -(-(w_buf + t) // LANES) * LANES
        winb, win_out = _win_call(state_win, l, win, rows_pad)
        win_pos0 = past_len - w_buf
    q0 = l_total - t
    chunk_rows = LANES * CMP_STRIDE
    nch = lpad // CMP_STRIDE
    ncp = -(-nch // (2 * LANES)) * 2 * LANES
    kcmp, vcmp = _cmp_call(kc.reshape(ns, nch, chunk_rows), vc.reshape(ns, nch, chunk_rows), *cmp_w, ncp)
    ovl, expand = _nsa_tables(l_total, ncp, lpad)
    qb = cfg['qb']
    k_top = min(DSA_TOPK, l_total // 4)
    if past is None and qb % LANES == 0:
        swap = lambda a: jnp.swapaxes(a, 1, 2)
        o_nsa = _nsa_t_call(qa, qar, gw, kcmp, vcmp, kvs, winb, ovl, qb=qb, ch=cfg['ch_nsa'])
        o_moba = _moba_t_call(qb_, mobab, swap(mobab[:, :, (MOBA_HEADS // 2) * LANES:]), qb=qb)
        o_dsa = _dsa_t_call(qc, qi, gw, dsab, swap(dsab[:, :, :LANES]), qb=qb, ch=cfg['ch_dsa'], k_top=k_top)
    else:
        o_nsa = _nsa_call(qa, qar, gw, kcmp, vcmp, kvs, winb, ovl, expand,
                          qb=qb, q0=q0, ch=cfg['ch_nsa'], win_pos0=win_pos0)
        o_moba = _moba_call(qb_, mobab, qb=qb, q0=q0, ch=cfg['ch_moba'])
        o_dsa = _dsa_call(qc, qi, gw, dsab, qb=qb, q0=q0, ch=cfg['ch_dsa'], k_top=k_top)
    return (o_nsa, o_moba, o_dsa), win_out


def _layer(x, mod, wts, tabs, cmp_w, past, cfg):
    sh1, sc1, g1, sh2, sc2, g2 = mod
    w_perm, wo, ln1g, ln1b, ln2g, ln2b, wr, br, wei, weo = wts
    pr = _proj_call(x, sh1, sc1, w_perm, tabs, *cfg['blk_a'])
    o_parts, win_out = _mixers(pr, cmp_w, past, cfg)
    x1, h2, gate = _post_attn_call(o_parts, x, g1, sh2, sc2, wo, ln1g, ln1b, wr, br, *cfg['blk_a'])
    y = _moe_call(h2, gate, x1, g2, wei, weo, ln2g, ln2b, *cfg['blk_m'])
    return y, pr[0], pr[2], pr[3], win_out


def _layer_weights(p, l):
    n_pad = LANES - N_EXPERTS - N_GROUPS
    wr = jnp.concatenate([p['w_router_expert'][l], p['w_router_group'][l],
                          jnp.zeros((D_MODEL, n_pad), F32)], axis=1).astype(MXU)
    br = jnp.concatenate([p['b_router_expert'][l], p['b_router_group'][l], jnp.zeros((n_pad,), F32)])[None, :]
    vec = lambda a: a[l].reshape(1, 1, D_MODEL)
    return (_permute_w_in(p['w_in'][l]), _pad_w_out(p['w_out'][l]), vec(p['ln1_g']), vec(p['ln1_b']),
            vec(p['ln2_g']), vec(p['ln2_b']), wr, br,
            p['w_expert_in'][l].astype(MXU), p['w_expert_out'][l].astype(MXU))


def kernel(x_prompt, x_sample, c_prompt, c_sample, cache_nsa, cache_moba, cache_dsa, state_win, page_table, w_ada, b_ada, w_in, cmp_pe, cmp_w1, cmp_w2, w_out, ln1_g, ln1_b, ln2_g, ln2_b, w_router_group, b_router_group, w_router_expert, b_router_expert, w_expert_in, w_expert_out):
    p = dict(w_in=w_in, w_out=w_out, ln1_g=ln1_g, ln1_b=ln1_b, ln2_g=ln2_g, ln2_b=ln2_b,
             w_router_group=w_router_group, b_router_group=b_router_group,
             w_router_expert=w_router_expert, b_router_expert=b_router_expert,
             w_expert_in=w_expert_in, w_expert_out=w_expert_out)
    nb_p, t_p, _ = x_prompt.shape
    nb_s, t_s, _ = x_sample.shape
    past_len = page_table.shape[1] * cache_nsa.shape[2]
    lpad_s = -(-(past_len + t_s) // MOBA_BLOCK) * MOBA_BLOCK

    n_c = nb_p + nb_s
    n_c_pad = -(-n_c // 8) * 8
    c_all = jnp.concatenate([c_prompt, c_sample, jnp.zeros((n_c_pad - n_c, D_MODEL), F32)], axis=0)
    mod_all = _ada_call(c_all, w_ada, b_ada)

    tabs_p = _rope_tables(jnp.arange(t_p))
    tabs_s = _rope_tables(past_len + jnp.arange(t_s))
    cfg_p = dict(blk_a=(1, 512), blk_m=(1, 1024), qb=128, ch_nsa=512, ch_moba=MOBA_BLOCK, ch_dsa=512)
    cfg_s = dict(blk_a=(32, t_s), blk_m=(nb_s, t_s), qb=t_s, ch_nsa=lpad_s, ch_moba=lpad_s, ch_dsa=lpad_s)

    y_p, y_s = x_prompt, x_sample
    st_p, st_s = [], []
    for l in range(DEPTH):
        mod_l = mod_all[l]
        split = lambda m: tuple(m[:, None, k * D_MODEL:(k + 1) * D_MODEL] for k in range(6))
        mod_p = split(mod_l[:nb_p])
        mod_s = split(mod_l[nb_p:n_c])
        wts = _layer_weights(p, l)
        cmp_w = _cmp_weights(cmp_pe[l], cmp_w1[l], cmp_w2[l])
        y_p, *sp = _layer(y_p, mod_p, wts, tabs_p, cmp_w, None, cfg_p)
        past = (cache_nsa, cache_moba, cache_dsa, state_win, page_table, l)
        y_s, *ss = _layer(y_s, mod_s, wts, tabs_s, cmp_w, past, cfg_s)
        st_p.append(sp)
        st_s.append(ss)

    def stack(st, k, tail):
        a = jnp.stack([s[k] for s in st])
        return a.reshape(a.shape[:3] + tail)

    nsa_t = (4, NSA_KV_HEADS, HEAD_DIM)
    moba_t = (2, MOBA_HEADS, HEAD_DIM)
    dsa_t = (3, HEAD_DIM)
    win_t = (2, NSA_KV_HEADS, HEAD_DIM)
    return (y_p, y_s, stack(st_p, 0, nsa_t), stack(st_s, 0, nsa_t), stack(st_p, 1, moba_t),
            stack(st_s, 1, moba_t), stack(st_p, 2, dsa_t), stack(st_s, 2, dsa_t),
            stack(st_p, 3, win_t), stack(st_s, 3, win_t))
```

```python
import functools
import math

import numpy as np
import jax
import jax.numpy as jnp
from jax import lax
from jax.experimental import pallas as pl
from jax.experimental.pallas import tpu as pltpu

D_MODEL = 1024
DEPTH = 2
HEAD_DIM = 64
NSA_HEADS = 6
NSA_KV_HEADS = 2
NSA_GROUP = NSA_HEADS // NSA_KV_HEADS
MOBA_HEADS = 5
DSA_HEADS = 5
D_MIX = (NSA_HEADS + MOBA_HEADS + DSA_HEADS) * HEAD_DIM
CMP_LEN = 32
CMP_STRIDE = 16
CMP_HIDDEN = 256
SEL_BLOCK = 64
SEL_TOPN = 16
WINDOW = 512
MOBA_BLOCK = 256
MOBA_TOPK = 3
IDX_HEADS = 4
DSA_TOPK = 256
ROPE_THETA = 500000.0
ROT_DIM = HEAD_DIM // 4
N_GROUPS = 4
EXPERTS_PER_GROUP = 8
N_EXPERTS = N_GROUPS * EXPERTS_PER_GROUP
D_EXPERT = 256
ALPHA = (2 * DEPTH) ** 0.25
LN_EPS = 1e-5
NEG = -1e30
BIG = 1e30
BELOW_NEG = -3e38
M_INIT = -1e29
INT_MIN = -2 ** 31

LANES = 128
F32 = jnp.float32
BF16 = jnp.bfloat16
MXU = BF16
Q_SCALE = HEAD_DIM ** -0.5

W_QA = NSA_HEADS * LANES
W_KVA = 6 * NSA_KV_HEADS * HEAD_DIM
W_NSA = 4 * NSA_KV_HEADS * HEAD_DIM
W_WIN = 2 * NSA_KV_HEADS * HEAD_DIM
W_QB = MOBA_HEADS * LANES
W_QC = DSA_HEADS * LANES
W_MOBA = 2 * MOBA_HEADS * HEAD_DIM
W_DSA = 3 * HEAD_DIM
W_DSA_PAD = 2 * LANES
W_QI = IDX_HEADS * LANES
W_GW = LANES
OFF_QA = 0
OFF_KVA = OFF_QA + W_QA
OFF_QB = OFF_KVA + W_KVA
OFF_QC = OFF_QB + W_QB
OFF_MOBA = OFF_QC + W_QC
OFF_DSA = OFF_MOBA + W_MOBA
OFF_QI = OFF_DSA + W_DSA_PAD
OFF_GW = OFF_QI + W_QI
W_PROJ = OFF_GW + W_GW
GW_WI = 3 * NSA_HEADS

W_ONSA = NSA_HEADS * LANES
W_OMOBA = MOBA_HEADS * LANES
W_ODSA = DSA_HEADS * LANES
W_OALL = W_ONSA + W_OMOBA + W_ODSA

VMEM_LIMIT = 56 * 1024 * 1024


def _params(sem):
    return pltpu.CompilerParams(dimension_semantics=sem, vmem_limit_bytes=VMEM_LIMIT)


def _ln(x):
    mu = jnp.mean(x, axis=-1, keepdims=True)
    xc = x - mu
    var = jnp.mean(xc * xc, axis=-1, keepdims=True)
    return xc * lax.rsqrt(var + LN_EPS)


def _dot(a, b):
    return jnp.dot(a, b, preferred_element_type=F32)


def _dot_t(a, b):
    return lax.dot_general(a, b, (((1,), (1,)), ((), ())), preferred_element_type=F32)


def _ada_kernel(c_ref, w_ref, b_ref, o_ref):
    c = c_ref[...]
    a = (c * jax.nn.sigmoid(c)).astype(MXU)
    o_ref[0] = _dot(a, w_ref[0].astype(MXU)) + b_ref[0]


def _ada_call(c_all, w_ada, b_ada):
    n = c_all.shape[0]
    tn = 1536
    return pl.pallas_call(
        _ada_kernel,
        grid=(DEPTH, 6 * D_MODEL // tn),
        in_specs=[
            pl.BlockSpec((n, D_MODEL), lambda l, j: (0, 0)),
            pl.BlockSpec((1, D_MODEL, tn), lambda l, j: (l, 0, j)),
            pl.BlockSpec((1, 1, tn), lambda l, j: (l, 0, j)),
        ],
        out_specs=pl.BlockSpec((1, n, tn), lambda l, j: (l, 0, j)),
        out_shape=jax.ShapeDtypeStruct((DEPTH, n, 6 * D_MODEL), F32),
        compiler_params=_params(("arbitrary", "arbitrary")),
        name="ada_mod",
    )(c_all, w_ada, b_ada.reshape(DEPTH, 1, 6 * D_MODEL))


def _rope_tile(t, cos, s_lo, s_hi, first_half_only):
    if first_half_only:
        lane = lax.broadcasted_iota(jnp.int32, cos.shape, 2)
        keep = lane < HEAD_DIM
        cos = jnp.where(keep, cos, 1.0)
        s_lo = jnp.where(keep, s_lo, 0.0)
        s_hi = jnp.where(keep, s_hi, 0.0)
    half = ROT_DIM // 2
    up = pltpu.roll(t, LANES - half, axis=2)
    dn = pltpu.roll(t, half, axis=2)
    return t * cos + up * s_lo + dn * s_hi


def _rope_slab(p, cos, s_lo, s_hi, modes):
    tiles = []
    for j, m in enumerate(modes):
        t = p[:, :, j * LANES:(j + 1) * LANES]
        if m != 'n':
            t = _rope_tile(t, cos, s_lo, s_hi, m == 'h')
        tiles.append(t)
    return tiles[0] if len(tiles) == 1 else jnp.concatenate(tiles, axis=2)


PROJ_F32_WIDTHS = (W_NSA, W_WIN, W_MOBA, W_DSA, W_GW)
PROJ_MXU_WIDTHS = (W_QA, W_QA, W_QB, W_QC, W_QI, LANES, LANES, 2 * LANES, W_WIN, W_MOBA, W_DSA_PAD)


def _proj_kernel(x_ref, sh_ref, sc_ref, w_ref, cos_ref, slo_ref, shi_ref,
                 nsa_ref, win_ref, moba_ref, dsa_ref, gw_ref,
                 qa_ref, qar_ref, qb_ref, qc_ref, qi_ref,
                 kc_ref, vc_ref, kvs_ref, winb_ref, mobab_ref, dsab_ref):
    x = x_ref[...]
    s, r, d = x.shape
    h = _ln(x) * (1.0 + sc_ref[...]) + sh_ref[...]
    hb = h.reshape(s * r, d).astype(MXU)
    rope = functools.partial(_rope_slab, cos=cos_ref[...], s_lo=slo_ref[...], s_hi=shi_ref[...])

    def seg(off, width):
        return _dot(hb, w_ref[:, off:off + width]).reshape(s, r, width)

    qa = seg(OFF_QA, W_QA)
    qa_ref[...] = qa.astype(MXU)
    qar_ref[...] = rope(qa, modes='f' * NSA_HEADS).astype(MXU)
    kva = seg(OFF_KVA, W_KVA)
    nsa = rope(kva[:, :, :W_NSA], modes='nnfn')
    nsa_ref[...] = nsa
    kc_ref[...] = nsa[:, :, 0:LANES].astype(MXU)
    vc_ref[...] = nsa[:, :, LANES:2 * LANES].astype(MXU)
    kvs_ref[...] = nsa[:, :, 2 * LANES:].astype(MXU)
    win = rope(kva[:, :, W_NSA:], modes='fn')
    win_ref[...] = win
    winb_ref[...] = win.astype(MXU)
    qb_ref[...] = rope(seg(OFF_QB, W_QB), modes='f' * MOBA_HEADS).astype(MXU)
    qc_ref[...] = rope(seg(OFF_QC, W_QC), modes='f' * DSA_HEADS).astype(MXU)
    moba = rope(seg(OFF_MOBA, W_MOBA), modes='ffhnn')
    moba_ref[...] = moba
    mobab_ref[...] = moba.astype(MXU)
    dsa = rope(seg(OFF_DSA, W_DSA_PAD), modes='hh')
    dsa_ref[...] = dsa[:, :, :W_DSA]
    dsab_ref[...] = dsa.astype(MXU)
    qi_ref[...] = rope(seg(OFF_QI, W_QI), modes='f' * IDX_HEADS).astype(MXU)
    gw_ref[...] = seg(OFF_GW, W_GW)


def _proj_call(x, sh, sc, w_perm, tabs, s_blk, r_blk):
    ns, rt, _ = x.shape
    cos, s_lo, s_hi = tabs
    tok = lambda w: pl.BlockSpec((s_blk, r_blk, w), lambda i, j: (i, j, 0))
    mod = pl.BlockSpec((s_blk, 1, D_MODEL), lambda i, j: (i, 0, 0))
    tab = pl.BlockSpec((1, r_blk, LANES), lambda i, j: (0, j, 0))
    widths = PROJ_F32_WIDTHS + PROJ_MXU_WIDTHS
    dtypes = (F32,) * len(PROJ_F32_WIDTHS) + (MXU,) * len(PROJ_MXU_WIDTHS)
    return pl.pallas_call(
        _proj_kernel,
        grid=(ns // s_blk, rt // r_blk),
        in_specs=[tok(D_MODEL), mod, mod,
                  pl.BlockSpec((D_MODEL, W_PROJ), lambda i, j: (0, 0)),
                  tab, tab, tab],
        out_specs=[tok(w) for w in widths],
        out_shape=[jax.ShapeDtypeStruct((ns, rt, w), dt) for w, dt in zip(widths, dtypes)],
        compiler_params=_params(("parallel", "parallel")),
        name="ln_proj_rope",
    )(x, sh, sc, w_perm, cos, s_lo, s_hi)


def _pad_heads(w, halves, scale):
    z = jnp.zeros((w.shape[0], HEAD_DIM), w.dtype)
    cols = []
    for h, half in enumerate(halves):
        wh = w[:, h * HEAD_DIM:(h + 1) * HEAD_DIM] * scale
        cols += [wh, z] if half == 0 else [z, wh]
    return jnp.concatenate(cols, axis=1)


def _permute_w_in(w_in_l):
    sizes = (NSA_HEADS * HEAD_DIM, 6 * NSA_KV_HEADS * HEAD_DIM, 3 * NSA_HEADS,
             MOBA_HEADS * HEAD_DIM, MOBA_HEADS * HEAD_DIM, MOBA_HEADS * HEAD_DIM,
             DSA_HEADS * HEAD_DIM, 2 * HEAD_DIM, IDX_HEADS * HEAD_DIM, HEAD_DIM, IDX_HEADS)
    offs = np.concatenate([[0], np.cumsum(sizes)])
    part = lambda k: w_in_l[:, int(offs[k]):int(offs[k + 1])]
    zeros = lambda n: jnp.zeros((D_MODEL, n), w_in_l.dtype)
    cols = [_pad_heads(part(0), [h // NSA_GROUP for h in range(NSA_HEADS)], Q_SCALE),
            part(1),
            _pad_heads(part(3), [h % 2 for h in range(MOBA_HEADS)], Q_SCALE),
            _pad_heads(part(6), [0] * DSA_HEADS, Q_SCALE),
            part(4), part(5),
            part(7), part(9), zeros(W_DSA_PAD - W_DSA),
            _pad_heads(part(8), [0] * IDX_HEADS, 1.0),
            part(2), part(10), zeros(W_GW - 3 * NSA_HEADS - IDX_HEADS)]
    return jnp.concatenate(cols, axis=1).astype(MXU)


def _pad_w_out(w_out_l):
    z = jnp.zeros((HEAD_DIM, D_MODEL), w_out_l.dtype)
    rows = []
    head = lambda h: w_out_l[h * HEAD_DIM:(h + 1) * HEAD_DIM]
    for h in range(NSA_HEADS):
        rows += [head(h), z] if h // NSA_GROUP == 0 else [z, head(h)]
    for h in range(MOBA_HEADS):
        rows += [head(NSA_HEADS + h), z] if (MOBA_HEADS + h) % 2 == 0 else [z, head(NSA_HEADS + h)]
    for h in range(DSA_HEADS):
        rows += [z, head(NSA_HEADS + MOBA_HEADS + h)]
    return jnp.concatenate(rows, axis=0).astype(MXU)


def _rope_tables(pos):
    half = ROT_DIM // 2
    inv = ROPE_THETA ** (-jnp.arange(0, ROT_DIM, 2, dtype=F32) / ROT_DIM)
    ang = pos.astype(F32)[:, None] * inv[None, :]
    cos, sin = jnp.cos(ang), jnp.sin(ang)
    t = pos.shape[0]
    ones = jnp.ones((t, HEAD_DIM - ROT_DIM), F32)
    zeros = jnp.zeros((t, HEAD_DIM - ROT_DIM), F32)
    zh = jnp.zeros((t, half), F32)
    c = jnp.concatenate([cos, cos, ones], axis=1)
    s_lo = jnp.concatenate([-sin, zh, zeros], axis=1)
    s_hi = jnp.concatenate([zh, sin, zeros], axis=1)
    rep = lambda a: jnp.concatenate([a, a], axis=1)[None]
    return rep(c), rep(s_lo), rep(s_hi)


def _assemble_kernel(tbl_ref, *refs, n_pages, ppb):
    pools = [refs[3 * j:3 * j + 3] for j in range(ppb)]
    nsa_new_ref, moba_new_ref, dsa_new_ref = refs[3 * ppb:3 * ppb + 3]
    kc_ref, vc_ref, kvs_ref, mob_ref, dsao_ref = refs[3 * ppb + 3:]
    p = pl.program_id(1)
    rows = kc_ref.shape[1]

    def emit(nsa, moba, dsa):
        kc_ref[0] = nsa[:, 0:LANES].astype(MXU)
        vc_ref[0] = nsa[:, LANES:2 * LANES].astype(MXU)
        kvs_ref[0] = nsa[:, 2 * LANES:].astype(MXU)
        mob_ref[0] = moba.astype(MXU)
        dsao_ref[0] = jnp.concatenate([dsa, jnp.zeros((rows, W_DSA_PAD - W_DSA), F32)], axis=1).astype(MXU)

    cat = lambda parts: parts[0] if len(parts) == 1 else jnp.concatenate(parts, axis=0)

    @pl.when(p * ppb < n_pages)
    def _():
        emit(*(cat([pools[j][k][0, 0] for j in range(ppb)]) for k in range(3)))

    @pl.when(p * ppb == n_pages)
    def _():
        t_new = nsa_new_ref.shape[1]
        tail = lambda new: jnp.concatenate([new, jnp.zeros((rows - t_new, new.shape[1]), F32)], axis=0)
        emit(tail(nsa_new_ref[0]), tail(moba_new_ref[0]), tail(dsa_new_ref[0]))

    @pl.when(p * ppb > n_pages)
    def _():
        emit(jnp.zeros((rows, W_NSA), F32), jnp.zeros((rows, W_MOBA), F32), jnp.zeros((rows, W_DSA), F32))


def _assemble_call(table, cache_nsa, cache_moba, cache_dsa, l, nsa_new, moba_new, dsa_new, lpad):
    ns, n_pages = table.shape
    page = cache_nsa.shape[2]
    ppb = MOBA_BLOCK // page
    assert n_pages % ppb == 0 and lpad % (ppb * page) == 0 and nsa_new.shape[1] <= ppb * page
    n_blk = lpad // (ppb * page)

    def pool(w, j):
        return pl.BlockSpec((1, 1, page, w),
                            lambda b, p, tbl: (l, tbl[b, jnp.minimum(p * ppb + j, n_pages - 1)], 0, 0))

    new = lambda w: pl.BlockSpec((1, nsa_new.shape[1], w), lambda b, p, tbl: (b, 0, 0))
    out = lambda w: pl.BlockSpec((1, ppb * page, w), lambda b, p, tbl: (b, p, 0))
    widths = (LANES, LANES, 2 * LANES, W_MOBA, W_DSA_PAD)
    pools = (cache_nsa.reshape(cache_nsa.shape[:3] + (W_NSA,)),
             cache_moba.reshape(cache_moba.shape[:3] + (W_MOBA,)),
             cache_dsa.reshape(cache_dsa.shape[:3] + (W_DSA,)))
    return pl.pallas_call(
        functools.partial(_assemble_kernel, n_pages=n_pages, ppb=ppb),
        grid_spec=pltpu.PrefetchScalarGridSpec(
            num_scalar_prefetch=1,
            grid=(ns, n_blk),
            in_specs=[pool(w, j) for j in range(ppb) for w in (W_NSA, W_MOBA, W_DSA)]
            + [new(W_NSA), new(W_MOBA), new(W_DSA)],
            out_specs=[out(w) for w in widths],
        ),
        out_shape=[jax.ShapeDtypeStruct((ns, lpad, w), MXU) for w in widths],
        compiler_params=_params(("parallel", "arbitrary")),
        name="assemble_pages",
    )(table, *(pools * ppb), nsa_new, moba_new, dsa_new)


def _win_kernel(buf_ref, new_ref, winb_ref, wout_ref):
    buf = buf_ref[0, 0]
    new = new_ref[0]
    pad = winb_ref.shape[1] - buf.shape[0] - new.shape[0]
    winb_ref[0] = jnp.concatenate([buf, new, jnp.zeros((pad, buf.shape[1]), F32)], axis=0).astype(MXU)
    wout_ref[0] = jnp.concatenate([buf[new.shape[0]:], new], axis=0)


def _win_call(state_win, l, win_new, rows_pad):
    ns, w_buf = state_win.shape[1], state_win.shape[2]
    t_new = win_new.shape[1]
    return pl.pallas_call(
        _win_kernel,
        grid=(ns,),
        in_specs=[pl.BlockSpec((1, 1, w_buf, W_WIN), lambda b: (l, b, 0, 0)),
                  pl.BlockSpec((1, t_new, W_WIN), lambda b: (b, 0, 0))],
        out_specs=[pl.BlockSpec((1, rows_pad, W_WIN), lambda b: (b, 0, 0)),
                   pl.BlockSpec((1, w_buf, W_WIN), lambda b: (b, 0, 0))],
        out_shape=[jax.ShapeDtypeStruct((ns, rows_pad, W_WIN), MXU),
                   jax.ShapeDtypeStruct((ns, w_buf, W_WIN), F32)],
        compiler_params=_params(("parallel",)),
        name="window_buffer",
    )(state_win.reshape(state_win.shape[:3] + (W_WIN,)), win_new)


def _cmp_kernel(kx_ref, vx_ref, pe_ref, w1_ref, w2_ref, kc_ref, vc_ref):
    nch = kx_ref.shape[1]
    nout = kc_ref.shape[1]
    for j, (x_ref, o_ref) in enumerate(((kx_ref, kc_ref), (vx_ref, vc_ref))):
        x = x_ref[0].astype(F32)
        a = _dot((x + pe_ref[j, 0]).astype(MXU), w1_ref[j, 0])
        b = _dot((x + pe_ref[j, 1]).astype(MXU), w1_ref[j, 1])
        hid = a + pltpu.roll(b, nch - 1, axis=0)
        out = _dot(jax.nn.gelu(hid).astype(MXU), w2_ref[j])
        o_ref[0, 0:nch] = out.astype(o_ref.dtype)
        if nout > nch:
            o_ref[0, nch:nout] = jnp.zeros((nout - nch, LANES), o_ref.dtype)


def _cmp_call(kx, vx, pe_e, w1_e, w2_e, nout):
    ns, nch, wx = kx.shape
    full = lambda a: pl.BlockSpec(a.shape, lambda b: (0,) * a.ndim)
    return pl.pallas_call(
        _cmp_kernel,
        grid=(ns,),
        in_specs=[pl.BlockSpec((1, nch, wx), lambda b: (b, 0, 0))] * 2 + [full(pe_e), full(w1_e), full(w2_e)],
        out_specs=[pl.BlockSpec((1, nout, LANES), lambda b: (b, 0, 0))] * 2,
        out_shape=[jax.ShapeDtypeStruct((ns, nout, LANES), MXU)] * 2,
        compiler_params=_params(("parallel",)),
        name="nsa_compress",
    )(kx, vx, pe_e, w1_e, w2_e)


def _cmp_weights(cmp_pe_l, cmp_w1_l, cmp_w2_l):
    half_rows = CMP_LEN // 2
    kv = NSA_KV_HEADS
    w1 = cmp_w1_l.reshape(2, 2, half_rows, HEAD_DIM, CMP_HIDDEN)
    w1_e = jnp.zeros((2, 2, half_rows, kv, HEAD_DIM, kv, CMP_HIDDEN), F32)
    for k in range(kv):
        w1_e = w1_e.at[:, :, :, k, :, k, :].set(w1)
    w1_e = w1_e.reshape(2, 2, half_rows * kv * HEAD_DIM, kv * CMP_HIDDEN).astype(MXU)
    w2_e = jnp.zeros((2, kv, CMP_HIDDEN, kv, HEAD_DIM), F32)
    for k in range(kv):
        w2_e = w2_e.at[:, k, :, k, :].set(cmp_w2_l)
    w2_e = w2_e.reshape(2, kv * CMP_HIDDEN, kv * HEAD_DIM).astype(MXU)
    pe = cmp_pe_l.reshape(2, 2, half_rows, 1, HEAD_DIM)
    pe_e = jnp.broadcast_to(pe, (2, 2, half_rows, kv, HEAD_DIM)).reshape(2, 2, 1, half_rows * kv * HEAD_DIM)
    return pe_e, w1_e, w2_e


def _online_step(sm, vt, m_scr, l_scr, acc_scr, rows):
    m_prev = m_scr[rows, :]
    m_new = jnp.maximum(m_prev, jnp.max(sm, axis=-1, keepdims=True))
    alpha = jnp.exp(m_prev - m_new)
    p = jnp.exp(sm - m_new)
    l_scr[rows, :] = alpha * l_scr[rows, :] + jnp.sum(p, axis=-1, keepdims=True)
    acc_scr[rows, :] = alpha * acc_scr[rows, :] + _dot(p.astype(MXU), vt)
    m_scr[rows, :] = m_new


def _reset_state(m_scr, l_scr, acc_scr):
    m_scr[...] = jnp.full(m_scr.shape, M_INIT, F32)
    l_scr[...] = jnp.zeros(l_scr.shape, F32)
    acc_scr[...] = jnp.zeros(acc_scr.shape, F32)


def _biased_softmax(sm):
    m = jnp.maximum(jnp.max(sm, axis=-1, keepdims=True), M_INIT)
    e = jnp.exp(sm - m)
    den = jnp.sum(e, axis=-1, keepdims=True)
    return e * jnp.where(den > 0.0, 1.0 / den, 0.0)


def _add_bias(s, bias, qb):
    c = bias.shape[1]
    outer = bias.shape[0] // qb
    groups = s.shape[0] // bias.shape[0]
    return (s.reshape(outer, groups, qb, c) + bias.reshape(outer, 1, qb, c)).reshape(s.shape)


def _stack_heads(ref, tiles):
    parts = [ref[0, :, t * LANES:(t + 1) * LANES].astype(F32) for t in tiles]
    return jnp.concatenate(parts, axis=0).astype(MXU)


def _top_blocks(imp, t0, qb):
    qp = max(qb, LANES)
    kv = imp.shape[0] // qb
    blk = lax.broadcasted_iota(jnp.int32, (LANES, 1), 0)
    tq = t0 + lax.broadcasted_iota(jnp.int32, (1, qp), 1)
    tb = tq // SEL_BLOCK
    forced = (blk == 0) | (blk == tb) | (blk == tb - 1)
    admissible = blk * SEL_BLOCK <= tq
    vals = []
    for k in range(kv):
        part = imp[k * qb:(k + 1) * qb]
        if qp > qb:
            part = jnp.concatenate([part, jnp.zeros((qp - qb, LANES), F32)], axis=0)
        vals.append(jnp.where(forced, BIG, jnp.where(admissible, part.T, NEG)))
    val = jnp.concatenate(vals, axis=1)
    sel = jnp.zeros(val.shape, F32)
    for _ in range(SEL_TOPN):
        top = jnp.max(val, axis=0, keepdims=True)
        idx = jnp.min(jnp.where(val == top, blk, LANES), axis=0, keepdims=True)
        pick = blk == idx
        sel = jnp.where(pick, 1.0, sel)
        val = jnp.where(pick, BELOW_NEG, val)
    return jnp.concatenate([sel[:, k * qp:(k + 1) * qp].T[:qb] for k in range(kv)], axis=0)


def _nsa_kernel(qa_ref, qar_ref, gw_ref, kc_ref, vc_ref, kvs_ref, win_ref, ovl_ref, exp_ref, o_ref,
                m_scr, l_scr, acc_scr, *, qb, q0, ch, win_pos0, slab):
    t0 = q0 + pl.program_id(1) * qb
    tq = t0 + lax.broadcasted_iota(jnp.int32, (qb, 1), 0)
    ncp = kc_ref.shape[1]
    win_rows = win_ref.shape[1]
    g, kv = NSA_GROUP, NSA_KV_HEADS
    qa = _stack_heads(qa_ref, range(NSA_HEADS))
    qr = _stack_heads(qar_ref, range(NSA_HEADS))

    c_end = lax.broadcasted_iota(jnp.int32, (1, ncp), 1) * CMP_STRIDE + (CMP_LEN - 1)
    bias_c = jnp.where(c_end <= tq, 0.0, NEG)
    p_c = _biased_softmax(_add_bias(_dot_t(qa, kc_ref[0]), bias_c, qb))
    o_cmp = _dot(p_c.astype(MXU), vc_ref[0])
    p4 = p_c.reshape(kv, g, qb, ncp)
    p_sum = p4[:, 0]
    for j in range(1, g):
        p_sum = p_sum + p4[:, j]
    p_sum = p_sum.reshape(kv * qb, ncp)
    p_hi = p_sum.astype(MXU)
    p_lo = (p_sum - p_hi.astype(F32)).astype(MXU)
    imp = _dot(p_hi, ovl_ref[...]) + _dot(p_lo, ovl_ref[...])
    sel_b = _top_blocks(imp, t0, qb).astype(MXU)

    _reset_state(m_scr, l_scr, acc_scr)

    def sel_step(c, causal):
        off = pl.multiple_of(c * ch, ch)
        kt = kvs_ref[0, pl.ds(off, ch), 0:LANES]
        vt = kvs_ref[0, pl.ds(off, ch), LANES:2 * LANES]
        bias = (_dot(sel_b, exp_ref[:, pl.ds(off, ch)]) - 1.0) * BIG
        if causal:
            k_pos = off + lax.broadcasted_iota(jnp.int32, (1, ch), 1)
            bias = _add_bias(bias, jnp.where(k_pos <= tq, 0.0, NEG), qb)
        _online_step(_add_bias(_dot_t(qr, kt), bias, qb), vt, m_scr, l_scr, acc_scr, slice(None))

    c_diag = t0 // ch

    def sel_body(c, carry):
        sel_step(c, False)
        return carry

    lax.fori_loop(0, c_diag, sel_body, 0)
    sel_step(c_diag, True)
    o_sel = acc_scr[...] / l_scr[...]

    if win_rows > slab:
        w_start = pl.multiple_of(jnp.clip(t0 - WINDOW - win_pos0, 0, win_rows - slab), 16)
    else:
        w_start = 0
    w_pos = win_pos0 + w_start + lax.broadcasted_iota(jnp.int32, (1, slab), 1)
    dist = tq - w_pos
    bias_w = jnp.where((dist >= 0) & (dist <= WINDOW), 0.0, NEG)
    kt = win_ref[0, pl.ds(w_start, slab), 0:LANES]
    vt = win_ref[0, pl.ds(w_start, slab), LANES:2 * LANES]
    p_w = _biased_softmax(_add_bias(_dot_t(qr, kt), bias_w, qb))
    o_win = _dot(p_w.astype(MXU), vt)

    gate = jax.nn.sigmoid(gw_ref[0])
    for h in range(NSA_HEADS):
        r = slice(h * qb, (h + 1) * qb)
        o = (gate[:, 3 * h:3 * h + 1] * o_cmp[r] + gate[:, 3 * h + 1:3 * h + 2] * o_sel[r]
             + gate[:, 3 * h + 2:3 * h + 3] * o_win[r])
        o_ref[0, :, h * LANES:(h + 1) * LANES] = o.astype(o_ref.dtype)


def _nsa_call(qa, qar, gw, kc, vc, kvs, winb, ovl, expand, *, qb, q0, ch, win_pos0):
    ns, t, _ = qa.shape
    assert ch % qb == 0 and (q0 % qb == 0 or ch == kvs.shape[1])
    slab = min(-(-(WINDOW + qb) // LANES) * LANES, winb.shape[1])
    tok = lambda w: pl.BlockSpec((1, qb, w), lambda b, i: (b, i, 0))
    seq = lambda a: pl.BlockSpec((1,) + a.shape[1:], lambda b, i: (b, 0, 0))
    full = lambda a: pl.BlockSpec(a.shape, lambda b, i: (0,) * a.ndim)
    rows = NSA_HEADS * qb
    return pl.pallas_call(
        functools.partial(_nsa_kernel, qb=qb, q0=q0, ch=ch, win_pos0=win_pos0, slab=slab),
        grid=(ns, t // qb),
        in_specs=[tok(W_QA), tok(W_QA), tok(W_GW), seq(kc), seq(vc), seq(kvs), seq(winb), full(ovl), full(expand)],
        out_specs=tok(W_ONSA),
        out_shape=jax.ShapeDtypeStruct((ns, t, W_ONSA), MXU),
        scratch_shapes=[pltpu.VMEM((rows, 1), F32), pltpu.VMEM((rows, 1), F32), pltpu.VMEM((rows, LANES), F32)],
        compiler_params=_params(("parallel", "arbitrary")),
        name="nsa_mixer",
    )(qa, qar, gw, kc, vc, kvs, winb, ovl, expand)


def _nsa_tables(l_total, ncp, lpad):
    n_cmp = (l_total - CMP_LEN) // CMP_STRIDE + 1
    c = np.arange(ncp)[:, None]
    n = np.arange(LANES)[None, :]
    ovl = ((c * CMP_STRIDE < n * SEL_BLOCK + SEL_BLOCK) & (c * CMP_STRIDE + CMP_LEN - 1 >= n * SEL_BLOCK)
           & (c < n_cmp))
    s = np.arange(lpad)[None, :]
    expand = (s // SEL_BLOCK) == np.arange(LANES)[:, None]
    return jnp.asarray(ovl, MXU), jnp.asarray(expand, MXU)


def _moba_kernel(q_ref, kv_ref, avg_ref, o_ref, kmean_scr, m_scr, l_scr, acc_scr, *, qb, q0, ch):
    i = pl.program_id(1)
    nh = MOBA_HEADS
    k_tiles = -(-nh // 2)
    blocks_per_chunk = ch // MOBA_BLOCK

    @pl.when(i == 0)
    def _():
        kmean_scr[...] = _dot(avg_ref[...], kv_ref[0, :, 0:k_tiles * LANES])

    t0 = q0 + i * qb
    tq = t0 + lax.broadcasted_iota(jnp.int32, (qb, 1), 0)
    own = tq // MOBA_BLOCK
    lane = lax.broadcasted_iota(jnp.int32, (1, LANES), 1)
    qs = [q_ref[0, :, h * LANES:(h + 1) * LANES] for h in range(nh)]
    kmean = kmean_scr[...].astype(MXU)
    vals = []
    for h in range(nh):
        s_blk = _dot_t(qs[h], kmean[:, (h // 2) * LANES:(h // 2 + 1) * LANES])
        vals.append(jnp.where(lane < own, s_blk, NEG))
    val = jnp.concatenate(vals, axis=0)
    picks = []
    for _ in range(MOBA_TOPK):
        top = jnp.max(val, axis=-1, keepdims=True)
        idx = jnp.min(jnp.where(val == top, lane, LANES), axis=-1, keepdims=True)
        picks.append(idx)
        val = jnp.where(lane == idx, BELOW_NEG, val)

    _reset_state(m_scr, l_scr, acc_scr)
    pos_in_block = lax.broadcasted_iota(jnp.int32, (1, MOBA_BLOCK), 1)

    def step(c, with_own):
        off = pl.multiple_of(c * ch, ch)
        sms, vts = [], []
        for h in range(nh):
            r = slice(h * qb, (h + 1) * qb)
            kt = kv_ref[0, pl.ds(off, ch), (h // 2) * LANES:(h // 2 + 1) * LANES]
            v_tile = (nh + h) // 2
            vts.append(kv_ref[0, pl.ds(off, ch), v_tile * LANES:(v_tile + 1) * LANES])
            s = _dot_t(qs[h], kt)
            parts = []
            for j in range(blocks_per_chunk):
                n = c * blocks_per_chunk + j
                chosen = ((picks[0][r] == n) | (picks[1][r] == n) | (picks[2][r] == n)) & (n < own)
                bias = jnp.where(chosen, 0.0, NEG)
                if with_own:
                    k_pos = n * MOBA_BLOCK + pos_in_block
                    bias = jnp.maximum(bias, jnp.where((own == n) & (k_pos <= tq), 0.0, NEG))
                parts.append(s[:, j * MOBA_BLOCK:(j + 1) * MOBA_BLOCK] + bias)
            sms.append(parts[0] if len(parts) == 1 else jnp.concatenate(parts, axis=1))
        sm = jnp.concatenate(sms, axis=0)
        m_prev = m_scr[...]
        m_new = jnp.maximum(m_prev, jnp.max(sm, axis=-1, keepdims=True))
        alpha = jnp.exp(m_prev - m_new)
        p = jnp.exp(sm - m_new)
        l_scr[...] = alpha * l_scr[...] + jnp.sum(p, axis=-1, keepdims=True)
        pv = jnp.concatenate([_dot(p[h * qb:(h + 1) * qb].astype(MXU), vts[h]) for h in range(nh)], axis=0)
        acc_scr[...] = alpha * acc_scr[...] + pv
        m_scr[...] = m_new

    c_diag = t0 // ch

    def body(c, carry):
        step(c, False)
        return carry

    lax.fori_loop(0, c_diag, body, 0)
    step(c_diag, True)
    o = acc_scr[...] / l_scr[...]
    for h in range(nh):
        o_ref[0, :, h * LANES:(h + 1) * LANES] = o[h * qb:(h + 1) * qb].astype(o_ref.dtype)


def _moba_call(q, kv, *, qb, q0, ch):
    ns, t, _ = q.shape
    assert ch % MOBA_BLOCK == 0 and ch % qb == 0 and (q0 % qb == 0 or ch == kv.shape[1])
    rows = MOBA_HEADS * qb
    k_tiles = -(-MOBA_HEADS // 2)
    lpad = kv.shape[1]
    avg = jnp.asarray((np.arange(lpad)[None, :] // MOBA_BLOCK == np.arange(LANES)[:, None]) / MOBA_BLOCK, MXU)
    return pl.pallas_call(
        functools.partial(_moba_kernel, qb=qb, q0=q0, ch=ch),
        grid=(ns, t // qb),
        in_specs=[pl.BlockSpec((1, qb, W_QB), lambda b, i: (b, i, 0)),
                  pl.BlockSpec((1,) + kv.shape[1:], lambda b, i: (b, 0, 0)),
                  pl.BlockSpec(avg.shape, lambda b, i: (0, 0))],
        out_specs=pl.BlockSpec((1, qb, W_OMOBA), lambda b, i: (b, i, 0)),
        out_shape=jax.ShapeDtypeStruct((ns, t, W_OMOBA), MXU),
        scratch_shapes=[pltpu.VMEM((LANES, k_tiles * LANES), F32),
                        pltpu.VMEM((rows, 1), F32), pltpu.VMEM((rows, 1), F32), pltpu.VMEM((rows, LANES), F32)],
        compiler_params=_params(("parallel", "arbitrary")),
        name="moba_mixer",
    )(q, kv, avg)


def _online_step_t(sm, vt_t, m_scr, l_scr, acc_scr, cols):
    m_prev = m_scr[:, cols]
    m_new = jnp.maximum(m_prev, jnp.max(sm, axis=0, keepdims=True))
    alpha = jnp.exp(m_prev - m_new)
    p = jnp.exp(sm - m_new)
    l_scr[:, cols] = alpha * l_scr[:, cols] + jnp.sum(p, axis=0, keepdims=True)
    acc_scr[:, cols] = alpha * acc_scr[:, cols] + _dot(vt_t, p.astype(MXU))
    m_scr[:, cols] = m_new


def _softmax_t(sm):
    m = jnp.maximum(jnp.max(sm, axis=0, keepdims=True), M_INIT)
    e = jnp.exp(sm - m)
    den = jnp.sum(e, axis=0, keepdims=True)
    return e * jnp.where(den > 0.0, 1.0 / den, 0.0)


def _transpose_tile(ref, t):
    return ref[0, :, t * LANES:(t + 1) * LANES].astype(F32).T.astype(MXU)


def _tile_lanes(x, n):
    return x if n == 1 else jnp.concatenate([x] * n, axis=1)


def _chunk_pairs(n, scores, update):
    def pair(c2, carry):
        a = scores(2 * c2)
        b = scores(2 * c2 + 1)
        update(*a)
        update(*b)
        return carry

    lax.fori_loop(0, n // 2, pair, 0)

    @pl.when(n % 2 == 1)
    def _():
        update(*scores(n - 1))


def _moba_t_kernel(q_ref, k_ref, vt_ref, o_ref, kmean_scr, chosen_scr, m_scr, l_scr, acc_scr, *, qb):
    i = pl.program_id(1)
    lpad = k_ref.shape[1]
    nh = MOBA_HEADS
    k_tiles = -(-nh // 2)

    @pl.when(i == 0)
    def _():
        kmean_scr[...] = jnp.zeros(kmean_scr.shape, F32)
        ones = jnp.ones((8, MOBA_BLOCK), MXU)

        def mean_body(n, carry):
            off = pl.multiple_of(n * MOBA_BLOCK, MOBA_BLOCK)
            tot = _dot(ones, k_ref[0, pl.ds(off, MOBA_BLOCK), 0:k_tiles * LANES])
            kmean_scr[pl.ds(n, 1), :] = tot[0:1] * (1.0 / MOBA_BLOCK)
            return carry

        lax.fori_loop(0, lpad // MOBA_BLOCK, mean_body, 0)

    t0 = i * qb
    tq = t0 + lax.broadcasted_iota(jnp.int32, (1, qb), 1)
    own = tq // MOBA_BLOCK
    blk = lax.broadcasted_iota(jnp.int32, (LANES, 1), 0)
    q_t = [_transpose_tile(q_ref, h) for h in range(nh)]
    kmean = kmean_scr[...].astype(MXU)
    past = blk < own
    val = jnp.concatenate(
        [jnp.where(past, _dot(kmean[:, (h // 2) * LANES:(h // 2 + 1) * LANES], q_t[h]), NEG) for h in range(nh)],
        axis=1)
    past_all = _tile_lanes(past, nh)
    chosen = jnp.zeros(val.shape, F32)
    for _ in range(MOBA_TOPK):
        top = jnp.max(val, axis=0, keepdims=True)
        idx = jnp.min(jnp.where(val == top, blk, LANES), axis=0, keepdims=True)
        pick = blk == idx
        chosen = jnp.where(pick & past_all, 1.0, chosen)
        val = jnp.where(pick, BELOW_NEG, val)
    chosen_scr[...] = (chosen - 1.0) * BIG

    _reset_state(m_scr, l_scr, acc_scr)
    k_in_blk = lax.broadcasted_iota(jnp.int32, (MOBA_BLOCK, 1), 0)

    def scores(n, with_own=False):
        off = pl.multiple_of(n * MOBA_BLOCK, MOBA_BLOCK)
        bias_rows = chosen_scr[pl.ds(n, 1), :]
        if with_own:
            own_bias = jnp.where((own == n) & (off + k_in_blk <= tq), 0.0, NEG)
        out = []
        for h in range(nh):
            c = slice(h * qb, (h + 1) * qb)
            kt = k_ref[0, pl.ds(off, MOBA_BLOCK), (h // 2) * LANES:(h // 2 + 1) * LANES]
            v_tile = (nh + h) // 2 - (nh // 2)
            vt_t = vt_ref[0, v_tile * LANES:(v_tile + 1) * LANES, pl.ds(off, MOBA_BLOCK)]
            bias = jnp.maximum(bias_rows[:, c], own_bias) if with_own else bias_rows[:, c]
            out += [_dot(kt, q_t[h]) + bias, vt_t]
        return out

    def update(*sv):
        for h in range(nh):
            _online_step_t(sv[2 * h], sv[2 * h + 1], m_scr, l_scr, acc_scr, slice(h * qb, (h + 1) * qb))

    n_own = t0 // MOBA_BLOCK
    _chunk_pairs(n_own, scores, update)
    update(*scores(n_own, True))
    o_t = acc_scr[...] / l_scr[...]
    for h in range(nh):
        o_ref[0, :, h * LANES:(h + 1) * LANES] = o_t[:, h * qb:(h + 1) * qb].T.astype(o_ref.dtype)


def _moba_t_call(q, k, vt, *, qb):
    ns, t, _ = q.shape
    assert MOBA_BLOCK % qb == 0 and qb % LANES == 0
    cols = MOBA_HEADS * qb
    k_tiles = -(-MOBA_HEADS // 2)
    return pl.pallas_call(
        functools.partial(_moba_t_kernel, qb=qb),
        grid=(ns, t // qb),
        in_specs=[pl.BlockSpec((1, qb, W_QB), lambda b, i: (b, i, 0)),
                  pl.BlockSpec((1,) + k.shape[1:], lambda b, i: (b, 0, 0)),
                  pl.BlockSpec((1,) + vt.shape[1:], lambda b, i: (b, 0, 0))],
        out_specs=pl.BlockSpec((1, qb, W_OMOBA), lambda b, i: (b, i, 0)),
        out_shape=jax.ShapeDtypeStruct((ns, t, W_OMOBA), MXU),
        scratch_shapes=[pltpu.VMEM((LANES, k_tiles * LANES), F32), pltpu.VMEM((LANES, cols), F32),
                        pltpu.VMEM((1, cols), F32), pltpu.VMEM((1, cols), F32), pltpu.VMEM((LANES, cols), F32)],
        compiler_params=_params(("parallel", "arbitrary")),
        name="moba_mixer_t",
    )(q, k, vt)


def _nsa_t_kernel(qa_ref, qar_ref, gw_ref, kc_ref, vct_ref, kvs_ref, vst_ref, win_ref, wvt_ref, ovlt_ref, expt_ref,
                  o_ref, m_scr, l_scr, acc_scr, *, qb, ch, slab):
    t0 = pl.program_id(1) * qb
    tq = t0 + lax.broadcasted_iota(jnp.int32, (1, qb), 1)
    ncp = kc_ref.shape[1]
    win_rows = win_ref.shape[1]
    g, kv, nh = NSA_GROUP, NSA_KV_HEADS, NSA_HEADS
    qa_t = jnp.concatenate([_transpose_tile(qa_ref, h) for h in range(nh)], axis=1)
    qr_t = jnp.concatenate([_transpose_tile(qar_ref, h) for h in range(nh)], axis=1)

    c_end = lax.broadcasted_iota(jnp.int32, (ncp, 1), 0) * CMP_STRIDE + (CMP_LEN - 1)
    bias_c = jnp.where(c_end <= tq, 0.0, NEG)
    p_c = _softmax_t(_dot(kc_ref[0], qa_t) + _tile_lanes(bias_c, nh))
    o_cmp = _dot(vct_ref[0], p_c.astype(MXU))
    sums = []
    for k in range(kv):
        acc = p_c[:, (k * g) * qb:(k * g + 1) * qb]
        for j in range(1, g):
            acc = acc + p_c[:, (k * g + j) * qb:(k * g + j + 1) * qb]
        sums.append(acc)
    p_sum = jnp.concatenate(sums, axis=1)
    p_hi = p_sum.astype(MXU)
    p_lo = (p_sum - p_hi.astype(F32)).astype(MXU)
    imp = _dot(ovlt_ref[...], p_hi) + _dot(ovlt_ref[...], p_lo)
    blk = lax.broadcasted_iota(jnp.int32, (LANES, 1), 0)
    tb = tq // SEL_BLOCK
    forced = _tile_lanes((blk == 0) | (blk == tb) | (blk == tb - 1), kv)
    admissible = _tile_lanes(blk * SEL_BLOCK <= tq, kv)
    val = jnp.where(forced, BIG, jnp.where(admissible, imp, NEG))
    sel = jnp.zeros(val.shape, F32)
    for _ in range(SEL_TOPN):
        top = jnp.max(val, axis=0, keepdims=True)
        idx = jnp.min(jnp.where(val == top, blk, LANES), axis=0, keepdims=True)
        pick = blk == idx
        sel = jnp.where(pick, 1.0, sel)
        val = jnp.where(pick, BELOW_NEG, val)
    sel_b = sel.astype(MXU)

    _reset_state(m_scr, l_scr, acc_scr)

    def scores(c, causal=False):
        off = pl.multiple_of(c * ch, ch)
        kt = kvs_ref[0, pl.ds(off, ch), 0:LANES]
        bias = (_dot(expt_ref[pl.ds(off, ch), :], sel_b) - 1.0) * BIG
        if causal:
            k_pos = off + lax.broadcasted_iota(jnp.int32, (ch, 1), 0)
            bias = bias + _tile_lanes(jnp.where(k_pos <= tq, 0.0, NEG), kv)
        bias = jnp.concatenate([bias[:, k * qb:(k + 1) * qb] for k in range(kv) for _ in range(g)], axis=1)
        return _dot(kt, qr_t) + bias, vst_ref[0, :, pl.ds(off, ch)]

    def update(sm, vt_t):
        _online_step_t(sm, vt_t, m_scr, l_scr, acc_scr, slice(None))

    c_diag = t0 // ch
    _chunk_pairs(c_diag, scores, update)
    update(*scores(c_diag, True))
    o_sel = acc_scr[...] / l_scr[...]

    w_start = pl.multiple_of(jnp.clip(t0 - WINDOW, 0, win_rows - slab), LANES)
    w_pos = w_start + lax.broadcasted_iota(jnp.int32, (slab, 1), 0)
    dist = tq - w_pos
    bias_w = jnp.where((dist >= 0) & (dist <= WINDOW), 0.0, NEG)
    kt = win_ref[0, pl.ds(w_start, slab), 0:LANES]
    p_w = _softmax_t(_dot(kt, qr_t) + _tile_lanes(bias_w, nh))
    o_win = _dot(wvt_ref[0, :, pl.ds(w_start, slab)], p_w.astype(MXU))

    gate_t = jax.nn.sigmoid(gw_ref[0]).T
    for h in range(nh):
        c = slice(h * qb, (h + 1) * qb)
        o = (gate_t[3 * h:3 * h + 1] * o_cmp[:, c] + gate_t[3 * h + 1:3 * h + 2] * o_sel[:, c]
             + gate_t[3 * h + 2:3 * h + 3] * o_win[:, c])
        o_ref[0, :, h * LANES:(h + 1) * LANES] = o.T.astype(o_ref.dtype)


def _nsa_t_call(qa, qar, gw, kc, vc, kvs, winb, ovl, *, qb, ch):
    ns, t, _ = qa.shape
    lpad = kvs.shape[1]
    assert ch % qb == 0 and qb % LANES == 0 and lpad == t
    slab = WINDOW + qb
    swap = lambda a: jnp.swapaxes(a, 1, 2)
    vct, vst, wvt = swap(vc), swap(kvs[:, :, LANES:]), swap(winb[:, :, LANES:])
    ovlt = ovl.T
    expt = jnp.asarray((np.arange(lpad)[:, None] // SEL_BLOCK) == np.arange(LANES)[None, :], MXU)
    tok = lambda w: pl.BlockSpec((1, qb, w), lambda b, i: (b, i, 0))
    seq = lambda a: pl.BlockSpec((1,) + a.shape[1:], lambda b, i: (b, 0, 0))
    full = lambda a: pl.BlockSpec(a.shape, lambda b, i: (0,) * a.ndim)
    cols = NSA_HEADS * qb
    return pl.pallas_call(
        functools.partial(_nsa_t_kernel, qb=qb, ch=ch, slab=slab),
        grid=(ns, t // qb),
        in_specs=[tok(W_QA), tok(W_QA), tok(W_GW), seq(kc), seq(vct), seq(kvs), seq(vst), seq(winb), seq(wvt),
                  full(ovlt), full(expt)],
        out_specs=tok(W_ONSA),
        out_shape=jax.ShapeDtypeStruct((ns, t, W_ONSA), MXU),
        scratch_shapes=[pltpu.VMEM((1, cols), F32), pltpu.VMEM((1, cols), F32), pltpu.VMEM((LANES, cols), F32)],
        compiler_params=_params(("parallel", "arbitrary")),
        name="nsa_mixer_t",
    )(qa, qar, gw, kc, vct, kvs, vst, winb, wvt, ovlt, expt)


def _dsa_t_kernel(qc_ref, qi_ref, gw_ref, kv_ref, kvt_ref, o_ref, key_scr, cut_scr, m_scr, l_scr, acc_scr,
                  *, qb, ch, k_top):
    t0 = pl.program_id(1) * qb
    tq = t0 + lax.broadcasted_iota(jnp.int32, (1, qb), 1)
    lpad = kv_ref.shape[1]
    n_ch = t0 // ch + 1
    gw_t = gw_ref[0].T
    w_idx = [gw_t[GW_WI + h:GW_WI + h + 1] for h in range(IDX_HEADS)]
    qi_t = [_transpose_tile(qi_ref, h) for h in range(IDX_HEADS)]
    slab = 64
    k_in_ch = lax.broadcasted_iota(jnp.int32, (ch, 1), 0)

    def score_body(c, carry):
        off = pl.multiple_of(c * ch, ch)
        kt = kv_ref[0, pl.ds(off, ch), LANES:2 * LANES]
        sc = w_idx[0] * jnp.maximum(_dot(kt, qi_t[0]), 0.0)
        for h in range(1, IDX_HEADS):
            sc = sc + w_idx[h] * jnp.maximum(_dot(kt, qi_t[h]), 0.0)
        sc = jnp.where(sc == 0.0, 0.0, sc)
        bits = lax.bitcast_convert_type(sc, jnp.int32)
        key = bits ^ (lax.shift_right_arithmetic(bits, 31) & 0x7FFFFFFF)
        key_scr[pl.ds(off, ch), :] = jnp.where(off + k_in_ch <= tq, key, INT_MIN)
        return carry

    lax.fori_loop(0, n_ch, score_body, 0)

    def count(pred):
        def body(c, acc):
            off = pl.multiple_of(c * ch, ch)
            for j in range(ch // slab):
                acc = acc + jnp.where(pred(key_scr[pl.ds(off + j * slab, slab), :], off + j * slab), 1.0, 0.0)
            return acc
        acc = lax.fori_loop(0, n_ch, body, jnp.zeros((slab, qb), F32))
        return jnp.sum(acc, axis=0, keepdims=True)

    few = tq + 1 <= k_top
    bits_per_check = 8

    def bit_cond(st):
        it, _, n_sel = st
        open_cols = jnp.where(few | (n_sel == k_top), 0.0, 1.0)
        return (it < 32) & (jnp.max(open_cols) > 0.0)

    def bit_body(st):
        it, prefix, n_sel = st
        for j in range(bits_per_check):
            cand = prefix | lax.shift_left(jnp.int32(1), 31 - (it + j))
            cand_key = cand ^ INT_MIN
            cnt = count(lambda k, off: k >= cand_key)
            take = (cnt >= k_top) & (n_sel != k_top)
            prefix, n_sel = jnp.where(take, cand, prefix), jnp.where(take, cnt, n_sel)
        return it + bits_per_check, prefix, n_sel

    _, prefix, n_sel = lax.while_loop(
        bit_cond, bit_body,
        (jnp.int32(0), jnp.zeros((1, qb), jnp.int32), jnp.full((1, qb), float(2 * lpad), F32)))
    thr = jnp.where(few, INT_MIN + 1, jnp.maximum(prefix ^ INT_MIN, INT_MIN + 1))

    overflow = (n_sel > k_top) & jnp.logical_not(few)
    cut_scr[...] = jnp.full((1, qb), lpad, jnp.int32)

    @pl.when(jnp.max(jnp.where(overflow, 1.0, 0.0)) > 0.0)
    def _():
        need = k_top - count(lambda k, off: k > thr)

        def cut_body(it, lo_hi):
            lo, hi = lo_hi
            mid = (lo + hi) // 2
            cnt = count(lambda k, off: (k == thr) & (off + k_in_ch[0:slab] <= mid))
            ok = cnt >= need
            return jnp.where(ok, lo, mid + 1), jnp.where(ok, mid, hi)
        lo, _ = lax.fori_loop(0, max(1, math.ceil(math.log2(lpad))), cut_body,
                              (jnp.zeros((1, qb), jnp.int32), jnp.full((1, qb), lpad - 1, jnp.int32)))
        cut_scr[...] = jnp.where(overflow, lo, lpad)

    cut = cut_scr[...]

    q_t = jnp.concatenate([_transpose_tile(qc_ref, h) for h in range(DSA_HEADS)], axis=1)
    _reset_state(m_scr, l_scr, acc_scr)

    def scores(c):
        off = pl.multiple_of(c * ch, ch)
        kt = kv_ref[0, pl.ds(off, ch), 0:LANES]
        key = key_scr[pl.ds(off, ch), :]
        bias = jnp.where((key > thr) | ((key == thr) & (off + k_in_ch <= cut)), 0.0, NEG)
        return _dot(kt, q_t) + _tile_lanes(bias, DSA_HEADS), kvt_ref[0, :, pl.ds(off, ch)]

    def update(sm, vt_t):
        _online_step_t(sm, vt_t, m_scr, l_scr, acc_scr, slice(None))

    _chunk_pairs(n_ch, scores, update)
    o_t = acc_scr[...] / l_scr[...]
    for h in range(DSA_HEADS):
        o_ref[0, :, h * LANES:(h + 1) * LANES] = o_t[:, h * qb:(h + 1) * qb].T.astype(o_ref.dtype)


def _dsa_t_call(qc, qi, gw, kv, kvt, *, qb, ch, k_top):
    ns, t, _ = qc.shape
    lpad = kv.shape[1]
    assert ch % qb == 0 and qb % LANES == 0 and lpad == t
    cols = DSA_HEADS * qb
    tok = lambda w: pl.BlockSpec((1, qb, w), lambda b, i: (b, i, 0))
    seq = lambda a: pl.BlockSpec((1,) + a.shape[1:], lambda b, i: (b, 0, 0))
    return pl.pallas_call(
        functools.partial(_dsa_t_kernel, qb=qb, ch=ch, k_top=k_top),
        grid=(ns, t // qb),
        in_specs=[tok(W_QC), tok(W_QI), tok(W_GW), seq(kv), seq(kvt)],
        out_specs=tok(W_ODSA),
        out_shape=jax.ShapeDtypeStruct((ns, t, W_ODSA), MXU),
        scratch_shapes=[pltpu.VMEM((lpad, qb), jnp.int32), pltpu.VMEM((1, qb), jnp.int32),
                        pltpu.VMEM((1, cols), F32), pltpu.VMEM((1, cols), F32), pltpu.VMEM((LANES, cols), F32)],
        compiler_params=_params(("parallel", "arbitrary")),
        name="dsa_mixer_t",
    )(qc, qi, gw, kv, kvt)


def _dsa_kernel(qc_ref, qi_ref, gw_ref, kv_ref, o_ref, key_scr, cut_scr, m_scr, l_scr, acc_scr,
                *, qb, q0, ch, k_top):
    t0 = q0 + pl.program_id(1) * qb
    tq = t0 + lax.broadcasted_iota(jnp.int32, (qb, 1), 0)
    lpad = kv_ref.shape[1]
    n_ch = t0 // ch + 1
    gw = gw_ref[0]
    w_idx = [gw[:, GW_WI + h:GW_WI + h + 1] for h in range(IDX_HEADS)]
    q_idx = [qi_ref[0, :, h * LANES:(h + 1) * LANES] for h in range(IDX_HEADS)]
    lane_ch = lax.broadcasted_iota(jnp.int32, (1, ch), 1)

    def score_body(c, carry):
        off = pl.multiple_of(c * ch, ch)
        kt = kv_ref[0, pl.ds(off, ch), LANES:2 * LANES]
        sc = w_idx[0] * jnp.maximum(_dot_t(q_idx[0], kt), 0.0)
        for h in range(1, IDX_HEADS):
            sc = sc + w_idx[h] * jnp.maximum(_dot_t(q_idx[h], kt), 0.0)
        sc = jnp.where(sc == 0.0, 0.0, sc)
        bits = lax.bitcast_convert_type(sc, jnp.int32)
        key = bits ^ (lax.shift_right_arithmetic(bits, 31) & 0x7FFFFFFF)
        key_scr[:, pl.ds(off, ch)] = jnp.where(off + lane_ch <= tq, key, INT_MIN)
        return carry

    lax.fori_loop(0, n_ch, score_body, 0)

    def count(pred):
        def body(c, acc):
            off = pl.multiple_of(c * ch, ch)
            hit = jnp.where(pred(key_scr[:, pl.ds(off, ch)], off), 1.0, 0.0)
            part = hit[:, 0:LANES]
            for j in range(1, ch // LANES):
                part = part + hit[:, j * LANES:(j + 1) * LANES]
            return acc + part
        acc = lax.fori_loop(0, n_ch, body, jnp.zeros((qb, LANES), F32))
        return jnp.sum(acc, axis=-1, keepdims=True)

    few = tq + 1 <= k_top

    bits_per_check = 8

    def bit_cond(st):
        it, _, n_sel = st
        open_rows = jnp.where(few | (n_sel == k_top), 0.0, 1.0)
        return (it < 32) & (jnp.max(open_rows) > 0.0)

    def bit_body(st):
        it, prefix, n_sel = st
        for j in range(bits_per_check):
            cand = prefix | lax.shift_left(jnp.int32(1), 31 - (it + j))
            cand_key = cand ^ INT_MIN
            cnt = count(lambda k, off: k >= cand_key)
            take = (cnt >= k_top) & (n_sel != k_top)
            prefix, n_sel = jnp.where(take, cand, prefix), jnp.where(take, cnt, n_sel)
        return it + bits_per_check, prefix, n_sel

    _, prefix, n_sel = lax.while_loop(
        bit_cond, bit_body,
        (jnp.int32(0), jnp.zeros((qb, 1), jnp.int32), jnp.full((qb, 1), float(2 * lpad), F32)))
    thr = jnp.where(few, INT_MIN + 1, jnp.maximum(prefix ^ INT_MIN, INT_MIN + 1))

    overflow = (n_sel > k_top) & jnp.logical_not(few)
    cut_scr[...] = jnp.full((qb, 1), lpad, jnp.int32)

    @pl.when(jnp.max(jnp.where(overflow, 1.0, 0.0)) > 0.0)
    def _():
        need = k_top - count(lambda k, off: k > thr)

        def cut_body(it, lo_hi):
            lo, hi = lo_hi
            mid = (lo + hi) // 2
            cnt = count(lambda k, off: (k == thr) & (off + lane_ch <= mid))
            ok = cnt >= need
            return jnp.where(ok, lo, mid + 1), jnp.where(ok, mid, hi)
        lo, _ = lax.fori_loop(0, max(1, math.ceil(math.log2(lpad))), cut_body,
                              (jnp.zeros((qb, 1), jnp.int32), jnp.full((qb, 1), lpad - 1, jnp.int32)))
        cut_scr[...] = jnp.where(overflow, lo, lpad)

    cut = cut_scr[...]

    q = _stack_heads(qc_ref, range(DSA_HEADS))
    _reset_state(m_scr, l_scr, acc_scr)

    def att_body(c, carry):
        off = pl.multiple_of(c * ch, ch)
        kt = kv_ref[0, pl.ds(off, ch), 0:LANES]
        key = key_scr[:, pl.ds(off, ch)]
        bias = jnp.where((key > thr) | ((key == thr) & (off + lane_ch <= cut)), 0.0, NEG)
        _online_step(_add_bias(_dot_t(q, kt), bias, qb), kt, m_scr, l_scr, acc_scr, slice(None))
        return carry

    lax.fori_loop(0, n_ch, att_body, 0)
    o = acc_scr[...] / l_scr[...]
    for h in range(DSA_HEADS):
        o_ref[0, :, h * LANES:(h + 1) * LANES] = o[h * qb:(h + 1) * qb].astype(o_ref.dtype)


def _dsa_call(qc, qi, gw, kv, *, qb, q0, ch, k_top):
    ns, t, _ = qc.shape
    lpad = kv.shape[1]
    assert ch % qb == 0 and (q0 % qb == 0 or ch == lpad)
    rows = DSA_HEADS * qb
    tok = lambda w: pl.BlockSpec((1, qb, w), lambda b, i: (b, i, 0))
    return pl.pallas_call(
        functools.partial(_dsa_kernel, qb=qb, q0=q0, ch=ch, k_top=k_top),
        grid=(ns, t // qb),
        in_specs=[tok(W_QC), tok(W_QI), tok(W_GW), pl.BlockSpec((1,) + kv.shape[1:], lambda b, i: (b, 0, 0))],
        out_specs=tok(W_ODSA),
        out_shape=jax.ShapeDtypeStruct((ns, t, W_ODSA), MXU),
        scratch_shapes=[pltpu.VMEM((qb, lpad), jnp.int32), pltpu.VMEM((qb, 1), jnp.int32),
                        pltpu.VMEM((rows, 1), F32), pltpu.VMEM((rows, 1), F32), pltpu.VMEM((rows, LANES), F32)],
        compiler_params=_params(("parallel", "arbitrary")),
        name="dsa_mixer",
    )(qc, qi, gw, kv)


def _post_attn_kernel(oa_ref, ob_ref, oc_ref, x_ref, g1_ref, sh_ref, sc_ref, wo_ref, lng_ref, lnb_ref,
                      wr_ref, br_ref, x1_ref, h2_ref, gate_ref):
    x = x_ref[...]
    s, r, d = x.shape
    att = None
    off = 0
    for o_ref in (oa_ref, ob_ref, oc_ref):
        w = o_ref.shape[2]
        part = _dot(o_ref[...].reshape(s * r, w), wo_ref[off:off + w, :])
        att = part if att is None else att + part
        off += w
    att = att.reshape(s, r, d)
    x1 = _ln(ALPHA * x + g1_ref[...] * att) * lng_ref[...] + lnb_ref[...]
    x1_ref[...] = x1
    h2 = _ln(x1) * (1.0 + sc_ref[...]) + sh_ref[...]
    h2b = h2.reshape(s * r, d).astype(MXU)
    h2_ref[...] = h2b.reshape(s, r, d)

    logit = _dot(h2b, wr_ref[...]) + br_ref[...]
    lane = lax.broadcasted_iota(jnp.int32, logit.shape, 1)
    is_g = (lane >= N_EXPERTS) & (lane < N_EXPERTS + N_GROUPS)
    lg = jnp.where(is_g, logit, NEG)
    mg = jnp.max(lg, axis=-1, keepdims=True)
    g_lane = jnp.min(jnp.where(lg == mg, lane, LANES), axis=-1, keepdims=True)
    g_den = jnp.sum(jnp.where(is_g, jnp.exp(lg - mg), 0.0), axis=-1, keepdims=True)
    g_w = 1.0 / g_den
    g_idx = g_lane - N_EXPERTS
    in_grp = (lane >= g_idx * EXPERTS_PER_GROUP) & (lane < (g_idx + 1) * EXPERTS_PER_GROUP)
    le = jnp.where(in_grp, logit, NEG)
    me = jnp.max(le, axis=-1, keepdims=True)
    ee = jnp.where(in_grp, jnp.exp(le - me), 0.0)
    pe = ee / jnp.sum(ee, axis=-1, keepdims=True)
    p1 = jnp.max(pe, axis=-1, keepdims=True)
    i1 = jnp.min(jnp.where((pe == p1) & in_grp, lane, LANES), axis=-1, keepdims=True)
    rest = jnp.where(in_grp & (lane != i1), pe, -1.0)
    p2 = jnp.max(rest, axis=-1, keepdims=True)
    i2 = jnp.min(jnp.where(rest == p2, lane, LANES), axis=-1, keepdims=True)
    tot = p1 + p2
    gate = jnp.where(lane == i1, g_w * p1 / tot, jnp.where(lane == i2, g_w * p2 / tot, 0.0))
    gate_ref[...] = gate.reshape(s, r, LANES)


def _post_attn_call(o_parts, x, g1, sh2, sc2, wo, lng, lnb, wr, br, s_blk, r_blk):
    ns, rt, _ = x.shape
    tok = lambda w: pl.BlockSpec((s_blk, r_blk, w), lambda i, j: (i, j, 0))
    mod = pl.BlockSpec((s_blk, 1, D_MODEL), lambda i, j: (i, 0, 0))
    full = lambda a: pl.BlockSpec(a.shape, lambda i, j: (0,) * a.ndim)
    return pl.pallas_call(
        _post_attn_kernel,
        grid=(ns // s_blk, rt // r_blk),
        in_specs=[tok(o.shape[2]) for o in o_parts] + [tok(D_MODEL), mod, mod, mod,
                                                       full(wo), full(lng), full(lnb), full(wr), full(br)],
        out_specs=[tok(D_MODEL), tok(D_MODEL), tok(LANES)],
        out_shape=[jax.ShapeDtypeStruct((ns, rt, D_MODEL), F32),
                   jax.ShapeDtypeStruct((ns, rt, D_MODEL), MXU),
                   jax.ShapeDtypeStruct((ns, rt, LANES), F32)],
        compiler_params=_params(("parallel", "parallel")),
        name="wout_ln_router",
    )(*o_parts, x, g1, sh2, sc2, wo, lng, lnb, wr, br)


def _moe_kernel(h_ref, gate_ref, x_ref, g2_ref, wi_ref, wo_ref, lng_ref, lnb_ref, out_ref, acc_ref):
    e = pl.program_id(2)
    s, r, d = x_ref.shape

    @pl.when(e == 0)
    def _():
        acc_ref[...] = jnp.zeros_like(acc_ref)

    h = h_ref[...].reshape(s * r, d)
    ab = _dot(h, wi_ref[0])
    a, b = ab[:, :D_EXPERT], ab[:, D_EXPERT:]
    gate = gate_ref[...].reshape(s * r, LANES)
    lane = lax.broadcasted_iota(jnp.int32, gate.shape, 1)
    gcol = jnp.sum(jnp.where(lane == e, gate, 0.0), axis=-1, keepdims=True)
    hid = (a * jax.nn.sigmoid(a) * b * gcol).astype(MXU)
    acc_ref[...] += _dot(hid, wo_ref[0])

    @pl.when(e == N_EXPERTS - 1)
    def _():
        f = acc_ref[...].reshape(s, r, d)
        out_ref[...] = _ln(ALPHA * x_ref[...] + g2_ref[...] * f) * lng_ref[...] + lnb_ref[...]


def _moe_call(h2, gate, x1, g2, wi, wo, lng, lnb, s_blk, r_blk):
    ns, rt, _ = x1.shape
    tok = lambda w: pl.BlockSpec((s_blk, r_blk, w), lambda i, j, e: (i, j, 0))
    mod = pl.BlockSpec((s_blk, 1, D_MODEL), lambda i, j, e: (i, 0, 0))
    vec = pl.BlockSpec((1, 1, D_MODEL), lambda i, j, e: (0, 0, 0))
    return pl.pallas_call(
        _moe_kernel,
        grid=(ns // s_blk, rt // r_blk, N_EXPERTS),
        in_specs=[tok(D_MODEL), tok(LANES), tok(D_MODEL), mod,
                  pl.BlockSpec((1, D_MODEL, 2 * D_EXPERT), lambda i, j, e: (e, 0, 0)),
                  pl.BlockSpec((1, D_EXPERT, D_MODEL), lambda i, j, e: (e, 0, 0)),
                  vec, vec],
        out_specs=tok(D_MODEL),
        out_shape=jax.ShapeDtypeStruct((ns, rt, D_MODEL), F32),
        scratch_shapes=[pltpu.VMEM((s_blk * r_blk, D_MODEL), F32)],
        compiler_params=_params(("parallel", "parallel", "arbitrary")),
        name="moe_ln",
    )(h2, gate, x1, g2, wi, wo, lng, lnb)


def _mixers(pr, cmp_w, past, cfg):
    (nsa, win, moba, dsa, gw, qa, qar, qb_, qc, qi, kc, vc, kvs, winb, mobab, dsab) = pr
    ns, t, _ = qa.shape
    if past is None:
        l_total, lpad, win_pos0 = t, t, 0
        win_out = win[:, -min(WINDOW, t):]
    else:
        cache_nsa, cache_moba, cache_dsa, state_win, table, l = past
        past_len = table.shape[1] * cache_nsa.shape[2]
        l_total = past_len + t
        lpad = -(-l_total // MOBA_BLOCK) * MOBA_BLOCK
        kc, vc, kvs, mobab, dsab = _assemble_call(table, cache_nsa, cache_moba, cache_dsa, l,
                                                 nsa, moba, dsa, lpad)
        w_buf = state_win.shape[2]
        rows_pad = -(-(w_buf + t) // LANES) * LANES
        winb, win_out = _win_call(state_win, l, win, rows_pad)
        win_pos0 = past_len - w_buf
    q0 = l_total - t
    chunk_rows = LANES * CMP_STRIDE
    nch = lpad // CMP_STRIDE
    ncp = -(-nch // (2 * LANES)) * 2 * LANES
    kcmp, vcmp = _cmp_call(kc.reshape(ns, nch, chunk_rows), vc.reshape(ns, nch, chunk_rows), *cmp_w, ncp)
    ovl, expand = _nsa_tables(l_total, ncp, lpad)
    qb = cfg['qb']
    k_top = min(DSA_TOPK, l_total // 4)
    if past is None and qb % LANES == 0:
        swap = lambda a: jnp.swapaxes(a, 1, 2)
        o_nsa = _nsa_t_call(qa, qar, gw, kcmp, vcmp, kvs, winb, ovl, qb=qb, ch=cfg['ch_nsa'])
        o_moba = _moba_t_call(qb_, mobab, swap(mobab[:, :, (MOBA_HEADS // 2) * LANES:]), qb=qb)
        o_dsa = _dsa_t_call(qc, qi, gw, dsab, swap(dsab[:, :, :LANES]), qb=qb, ch=cfg['ch_dsa'], k_top=k_top)
    else:
        o_nsa = _nsa_call(qa, qar, gw, kcmp, vcmp, kvs, winb, ovl, expand,
                          qb=qb, q0=q0, ch=cfg['ch_nsa'], win_pos0=win_pos0)
        o_moba = _moba_call(qb_, mobab, qb=qb, q0=q0, ch=cfg['ch_moba'])
        o_dsa = _dsa_call(qc, qi, gw, dsab, qb=qb, q0=q0, ch=cfg['ch_dsa'], k_top=k_top)
    return (o_nsa, o_moba, o_dsa), win_out


def _layer(x, mod, wts, tabs, cmp_w, past, cfg):
    sh1, sc1, g1, sh2, sc2, g2 = mod
    w_perm, wo, ln1g, ln1b, ln2g, ln2b, wr, br, wei, weo = wts
    pr = _proj_call(x, sh1, sc1, w_perm, tabs, *cfg['blk_a'])
    o_parts, win_out = _mixers(pr, cmp_w, past, cfg)
    x1, h2, gate = _post_attn_call(o_parts, x, g1, sh2, sc2, wo, ln1g, ln1b, wr, br, *cfg['blk_a'])
    y = _moe_call(h2, gate, x1, g2, wei, weo, ln2g, ln2b, *cfg['blk_m'])
    return y, pr[0], pr[2], pr[3], win_out


def _layer_weights(p, l):
    n_pad = LANES - N_EXPERTS - N_GROUPS
    wr = jnp.concatenate([p['w_router_expert'][l], p['w_router_group'][l],
                          jnp.zeros((D_MODEL, n_pad), F32)], axis=1).astype(MXU)
    br = jnp.concatenate([p['b_router_expert'][l], p['b_router_group'][l], jnp.zeros((n_pad,), F32)])[None, :]
    vec = lambda a: a[l].reshape(1, 1, D_MODEL)
    return (_permute_w_in(p['w_in'][l]), _pad_w_out(p['w_out'][l]), vec(p['ln1_g']), vec(p['ln1_b']),
            vec(p['ln2_g']), vec(p['ln2_b']), wr, br,
            p['w_expert_in'][l].astype(MXU), p['w_expert_out'][l].astype(MXU))


def kernel(x_prompt, x_sample, c_prompt, c_sample, cache_nsa, cache_moba, cache_dsa, state_win, page_table, w_ada, b_ada, w_in, cmp_pe, cmp_w1, cmp_w2, w_out, ln1_g, ln1_b, ln2_g, ln2_b, w_router_group, b_router_group, w_router_expert, b_router_expert, w_expert_in, w_expert_out):
    p = dict(w_in=w_in, w_out=w_out, ln1_g=ln1_g, ln1_b=ln1_b, ln2_g=ln2_g, ln2_b=ln2_b,
             w_router_group=w_router_group, b_router_group=b_router_group,
             w_router_expert=w_router_expert, b_router_expert=b_router_expert,
             w_expert_in=w_expert_in, w_expert_out=w_expert_out)
    nb_p, t_p, _ = x_prompt.shape
    nb_s, t_s, _ = x_sample.shape
    past_len = page_table.shape[1] * cache_nsa.shape[2]
    lpad_s = -(-(past_len + t_s) // MOBA_BLOCK) * MOBA_BLOCK

    n_c = nb_p + nb_s
    n_c_pad = -(-n_c // 8) * 8
    c_all = jnp.concatenate([c_prompt, c_sample, jnp.zeros((n_c_pad - n_c, D_MODEL), F32)], axis=0)
    mod_all = _ada_call(c_all, w_ada, b_ada)

    tabs_p = _rope_tables(jnp.arange(t_p))
    tabs_s = _rope_tables(past_len + jnp.arange(t_s))
    cfg_p = dict(blk_a=(1, 512), blk_m=(1, 1024), qb=128, ch_nsa=512, ch_moba=MOBA_BLOCK, ch_dsa=512)
    cfg_s = dict(blk_a=(32, t_s), blk_m=(nb_s, t_s), qb=t_s, ch_nsa=lpad_s, ch_moba=lpad_s, ch_dsa=lpad_s)

    y_p, y_s = x_prompt, x_sample
    st_p, st_s = [], []
    for l in range(DEPTH):
        mod_l = mod_all[l]
        split = lambda m: tuple(m[:, None, k * D_MODEL:(k + 1) * D_MODEL] for k in range(6))
        mod_p = split(mod_l[:nb_p])
        mod_s = split(mod_l[nb_p:n_c])
        wts = _layer_weights(p, l)
        cmp_w = _cmp_weights(cmp_pe[l], cmp_w1[l], cmp_w2[l])
        y_p, *sp = _layer(y_p, mod_p, wts, tabs_p, cmp_w, None, cfg_p)
        past = (cache_nsa, cache_moba, cache_dsa, state_win, page_table, l)
        y_s, *ss = _layer(y_s, mod_s, wts, tabs_s, cmp_w, past, cfg_s)
        st_p.append(sp)
        st_s.append(ss)

    def stack(st, k, tail):
        a = jnp.stack([s[k] for s in st])
        return a.reshape(a.shape[:3] + tail)

    nsa_t = (4, NSA_KV_HEADS, HEAD_DIM)
    moba_t = (2, MOBA_HEADS, HEAD_DIM)
    dsa_t = (3, HEAD_DIM)
    win_t = (2, NSA_KV_HEADS, HEAD_DIM)
    return (y_p, y_s, stack(st_p, 0, nsa_t), stack(st_s, 0, nsa_t), stack(st_p, 1, moba_t),
            stack(st_s, 1, moba_t), stack(st_p, 2, dsa_t), stack(st_s, 2, dsa_t),
            stack(st_p, 3, win_t), stack(st_s, 3, win_t))
```

```python
import functools
import math

import numpy as np
import jax
import jax.numpy as jnp
from jax import lax
from jax.experimental import pallas as pl
from jax.experimental.pallas import tpu as pltpu

D_MODEL = 1024
DEPTH = 2
HEAD_DIM = 64
NSA_HEADS = 6
NSA_KV_HEADS = 2
NSA_GROUP = NSA_HEADS // NSA_KV_HEADS
MOBA_HEADS = 5
DSA_HEADS = 5
D_MIX = (NSA_HEADS + MOBA_HEADS + DSA_HEADS) * HEAD_DIM
CMP_LEN = 32
CMP_STRIDE = 16
CMP_HIDDEN = 256
SEL_BLOCK = 64
SEL_TOPN = 16
WINDOW = 512
MOBA_BLOCK = 256
MOBA_TOPK = 3
IDX_HEADS = 4
DSA_TOPK = 256
ROPE_THETA = 500000.0
ROT_DIM = HEAD_DIM // 4
N_GROUPS = 4
EXPERTS_PER_GROUP = 8
N_EXPERTS = N_GROUPS * EXPERTS_PER_GROUP
D_EXPERT = 256
ALPHA = (2 * DEPTH) ** 0.25
LN_EPS = 1e-5
NEG = -1e30
BIG = 1e30
BELOW_NEG = -3e38
M_INIT = -1e29
INT_MIN = -2 ** 31

LANES = 128
F32 = jnp.float32
BF16 = jnp.bfloat16
MXU = BF16
Q_SCALE = HEAD_DIM ** -0.5

W_QA = NSA_HEADS * LANES
W_KVA = 6 * NSA_KV_HEADS * HEAD_DIM
W_NSA = 4 * NSA_KV_HEADS * HEAD_DIM
W_WIN = 2 * NSA_KV_HEADS * HEAD_DIM
W_QB = MOBA_HEADS * LANES
W_QC = DSA_HEADS * LANES
W_MOBA = 2 * MOBA_HEADS * HEAD_DIM
W_DSA = 3 * HEAD_DIM
W_DSA_PAD = 2 * LANES
W_QI = IDX_HEADS * LANES
W_GW = LANES
OFF_QA = 0
OFF_KVA = OFF_QA + W_QA
OFF_QB = OFF_KVA + W_KVA
OFF_QC = OFF_QB + W_QB
OFF_MOBA = OFF_QC + W_QC
OFF_DSA = OFF_MOBA + W_MOBA
OFF_QI = OFF_DSA + W_DSA_PAD
OFF_GW = OFF_QI + W_QI
W_PROJ = OFF_GW + W_GW
GW_WI = 3 * NSA_HEADS

W_ONSA = NSA_HEADS * LANES
W_OMOBA = MOBA_HEADS * LANES
W_ODSA = DSA_HEADS * LANES
W_OALL = W_ONSA + W_OMOBA + W_ODSA

VMEM_LIMIT = 56 * 1024 * 1024


def _params(sem):
    return pltpu.CompilerParams(dimension_semantics=sem, vmem_limit_bytes=VMEM_LIMIT)


def _ln(x):
    mu = jnp.mean(x, axis=-1, keepdims=True)
    xc = x - mu
    var = jnp.mean(xc * xc, axis=-1, keepdims=True)
    return xc * lax.rsqrt(var + LN_EPS)


def _dot(a, b):
    return jnp.dot(a, b, preferred_element_type=F32)


def _dot_t(a, b):
    return lax.dot_general(a, b, (((1,), (1,)), ((), ())), preferred_element_type=F32)


def _ada_kernel(c_ref, w_ref, b_ref, o_ref):
    c = c_ref[...]
    a = (c * jax.nn.sigmoid(c)).astype(MXU)
    o_ref[0] = _dot(a, w_ref[0].astype(MXU)) + b_ref[0]


def _ada_call(c_all, w_ada, b_ada):
    n = c_all.shape[0]
    tn = 1536
    return pl.pallas_call(
        _ada_kernel,
        grid=(DEPTH, 6 * D_MODEL // tn),
        in_specs=[
            pl.BlockSpec((n, D_MODEL), lambda l, j: (0, 0)),
            pl.BlockSpec((1, D_MODEL, tn), lambda l, j: (l, 0, j)),
            pl.BlockSpec((1, 1, tn), lambda l, j: (l, 0, j)),
        ],
        out_specs=pl.BlockSpec((1, n, tn), lambda l, j: (l, 0, j)),
        out_shape=jax.ShapeDtypeStruct((DEPTH, n, 6 * D_MODEL), F32),
        compiler_params=_params(("arbitrary", "arbitrary")),
        name="ada_mod",
    )(c_all, w_ada, b_ada.reshape(DEPTH, 1, 6 * D_MODEL))


def _rope_tile(t, cos, s_lo, s_hi, first_half_only):
    if first_half_only:
        lane = lax.broadcasted_iota(jnp.int32, cos.shape, 2)
        keep = lane < HEAD_DIM
        cos = jnp.where(keep, cos, 1.0)
        s_lo = jnp.where(keep, s_lo, 0.0)
        s_hi = jnp.where(keep, s_hi, 0.0)
    half = ROT_DIM // 2
    up = pltpu.roll(t, LANES - half, axis=2)
    dn = pltpu.roll(t, half, axis=2)
    return t * cos + up * s_lo + dn * s_hi


def _rope_slab(p, cos, s_lo, s_hi, modes):
    tiles = []
    for j, m in enumerate(modes):
        t = p[:, :, j * LANES:(j + 1) * LANES]
        if m != 'n':
            t = _rope_tile(t, cos, s_lo, s_hi, m == 'h')
        tiles.append(t)
    return tiles[0] if len(tiles) == 1 else jnp.concatenate(tiles, axis=2)


PROJ_F32_WIDTHS = (W_NSA, W_WIN, W_MOBA, W_DSA, W_GW)
PROJ_MXU_WIDTHS = (W_QA, W_QA, W_QB, W_QC, W_QI, LANES, LANES, 2 * LANES, W_WIN, W_MOBA, W_DSA_PAD)


def _proj_kernel(x_ref, sh_ref, sc_ref, w_ref, cos_ref, slo_ref, shi_ref,
                 nsa_ref, win_ref, moba_ref, dsa_ref, gw_ref,
                 qa_ref, qar_ref, qb_ref, qc_ref, qi_ref,
                 kc_ref, vc_ref, kvs_ref, winb_ref, mobab_ref, dsab_ref):
    x = x_ref[...]
    s, r, d = x.shape
    h = _ln(x) * (1.0 + sc_ref[...]) + sh_ref[...]
    hb = h.reshape(s * r, d).astype(MXU)
    rope = functools.partial(_rope_slab, cos=cos_ref[...], s_lo=slo_ref[...], s_hi=shi_ref[...])

    def seg(off, width):
        return _dot(hb, w_ref[:, off:off + width]).reshape(s, r, width)

    qa = seg(OFF_QA, W_QA)
    qa_ref[...] = qa.astype(MXU)
    qar_ref[...] = rope(qa, modes='f' * NSA_HEADS).astype(MXU)
    kva = seg(OFF_KVA, W_KVA)
    nsa = rope(kva[:, :, :W_NSA], modes='nnfn')
    nsa_ref[...] = nsa
    kc_ref[...] = nsa[:, :, 0:LANES].astype(MXU)
    vc_ref[...] = nsa[:, :, LANES:2 * LANES].astype(MXU)
    kvs_ref[...] = nsa[:, :, 2 * LANES:].astype(MXU)
    win = rope(kva[:, :, W_NSA:], modes='fn')
    win_ref[...] = win
    winb_ref[...] = win.astype(MXU)
    qb_ref[...] = rope(seg(OFF_QB, W_QB), modes='f' * MOBA_HEADS).astype(MXU)
    qc_ref[...] = rope(seg(OFF_QC, W_QC), modes='f' * DSA_HEADS).astype(MXU)
    moba = rope(seg(OFF_MOBA, W_MOBA), modes='ffhnn')
    moba_ref[...] = moba
    mobab_ref[...] = moba.astype(MXU)
    dsa = rope(seg(OFF_DSA, W_DSA_PAD), modes='hh')
    dsa_ref[...] = dsa[:, :, :W_DSA]
    dsab_ref[...] = dsa.astype(MXU)
    qi_ref[...] = rope(seg(OFF_QI, W_QI), modes='f' * IDX_HEADS).astype(MXU)
    gw_ref[...] = seg(OFF_GW, W_GW)


def _proj_call(x, sh, sc, w_perm, tabs, s_blk, r_blk):
    ns, rt, _ = x.shape
    cos, s_lo, s_hi = tabs
    tok = lambda w: pl.BlockSpec((s_blk, r_blk, w), lambda i, j: (i, j, 0))
    mod = pl.BlockSpec((s_blk, 1, D_MODEL), lambda i, j: (i, 0, 0))
    tab = pl.BlockSpec((1, r_blk, LANES), lambda i, j: (0, j, 0))
    widths = PROJ_F32_WIDTHS + PROJ_MXU_WIDTHS
    dtypes = (F32,) * len(PROJ_F32_WIDTHS) + (MXU,) * len(PROJ_MXU_WIDTHS)
    return pl.pallas_call(
        _proj_kernel,
        grid=(ns // s_blk, rt // r_blk),
        in_specs=[tok(D_MODEL), mod, mod,
                  pl.BlockSpec((D_MODEL, W_PROJ), lambda i, j: (0, 0)),
                  tab, tab, tab],
        out_specs=[tok(w) for w in widths],
        out_shape=[jax.ShapeDtypeStruct((ns, rt, w), dt) for w, dt in zip(widths, dtypes)],
        compiler_params=_params(("parallel", "parallel")),
        name="ln_proj_rope",
    )(x, sh, sc, w_perm, cos, s_lo, s_hi)


def _pad_heads(w, halves, scale):
    z = jnp.zeros((w.shape[0], HEAD_DIM), w.dtype)
    cols = []
    for h, half in enumerate(halves):
        wh = w[:, h * HEAD_DIM:(h + 1) * HEAD_DIM] * scale
        cols += [wh, z] if half == 0 else [z, wh]
    return jnp.concatenate(cols, axis=1)


def _permute_w_in(w_in_l):
    sizes = (NSA_HEADS * HEAD_DIM, 6 * NSA_KV_HEADS * HEAD_DIM, 3 * NSA_HEADS,
             MOBA_HEADS * HEAD_DIM, MOBA_HEADS * HEAD_DIM, MOBA_HEADS * HEAD_DIM,
             DSA_HEADS * HEAD_DIM, 2 * HEAD_DIM, IDX_HEADS * HEAD_DIM, HEAD_DIM, IDX_HEADS)
    offs = np.concatenate([[0], np.cumsum(sizes)])
    part = lambda k: w_in_l[:, int(offs[k]):int(offs[k + 1])]
    zeros = lambda n: jnp.zeros((D_MODEL, n), w_in_l.dtype)
    cols = [_pad_heads(part(0), [h // NSA_GROUP for h in range(NSA_HEADS)], Q_SCALE),
            part(1),
            _pad_heads(part(3), [h % 2 for h in range(MOBA_HEADS)], Q_SCALE),
            _pad_heads(part(6), [0] * DSA_HEADS, Q_SCALE),
            part(4), part(5),
            part(7), part(9), zeros(W_DSA_PAD - W_DSA),
            _pad_heads(part(8), [0] * IDX_HEADS, 1.0),
            part(2), part(10), zeros(W_GW - 3 * NSA_HEADS - IDX_HEADS)]
    return jnp.concatenate(cols, axis=1).astype(MXU)


def _pad_w_out(w_out_l):
    z = jnp.zeros((HEAD_DIM, D_MODEL), w_out_l.dtype)
    rows = []
    head = lambda h: w_out_l[h * HEAD_DIM:(h + 1) * HEAD_DIM]
    for h in range(NSA_HEADS):
        rows += [head(h), z] if h // NSA_GROUP == 0 else [z, head(h)]
    for h in range(MOBA_HEADS):
        rows += [head(NSA_HEADS + h), z] if (MOBA_HEADS + h) % 2 == 0 else [z, head(NSA_HEADS + h)]
    for h in range(DSA_HEADS):
        rows += [z, head(NSA_HEADS + MOBA_HEADS + h)]
    return jnp.concatenate(rows, axis=0).astype(MXU)


def _rope_tables(pos):
    half = ROT_DIM // 2
    inv = ROPE_THETA ** (-jnp.arange(0, ROT_DIM, 2, dtype=F32) / ROT_DIM)
    ang = pos.astype(F32)[:, None] * inv[None, :]
    cos, sin = jnp.cos(ang), jnp.sin(ang)
    t = pos.shape[0]
    ones = jnp.ones((t, HEAD_DIM - ROT_DIM), F32)
    zeros = jnp.zeros((t, HEAD_DIM - ROT_DIM), F32)
    zh = jnp.zeros((t, half), F32)
    c = jnp.concatenate([cos, cos, ones], axis=1)
    s_lo = jnp.concatenate([-sin, zh, zeros], axis=1)
    s_hi = jnp.concatenate([zh, sin, zeros], axis=1)
    rep = lambda a: jnp.concatenate([a, a], axis=1)[None]
    return rep(c), rep(s_lo), rep(s_hi)


def _assemble_kernel(tbl_ref, *refs, n_pages, ppb):
    pools = [refs[3 * j:3 * j + 3] for j in range(ppb)]
    nsa_new_ref, moba_new_ref, dsa_new_ref = refs[3 * ppb:3 * ppb + 3]
    kc_ref, vc_ref, kvs_ref, mob_ref, dsao_ref = refs[3 * ppb + 3:]
    p = pl.program_id(1)
    rows = kc_ref.shape[1]

    def emit(nsa, moba, dsa):
        kc_ref[0] = nsa[:, 0:LANES].astype(MXU)
        vc_ref[0] = nsa[:, LANES:2 * LANES].astype(MXU)
        kvs_ref[0] = nsa[:, 2 * LANES:].astype(MXU)
        mob_ref[0] = moba.astype(MXU)
        dsao_ref[0] = jnp.concatenate([dsa, jnp.zeros((rows, W_DSA_PAD - W_DSA), F32)], axis=1).astype(MXU)

    cat = lambda parts: parts[0] if len(parts) == 1 else jnp.concatenate(parts, axis=0)

    def page(ref):
        x = ref[0, 0]
        pad = -x.shape[0] % LANES
        if pad:
            x = jnp.concatenate([x, jnp.zeros((pad, x.shape[1]), F32)], axis=0)
        return x.T[:, :ref.shape[2]]

    @pl.when(p * ppb < n_pages)
    def _():
        emit(*(cat([page(pools[j][k]) for j in range(ppb)]) for k in range(3)))

    @pl.when(p * ppb == n_pages)
    def _():
        t_new = nsa_new_ref.shape[1]
        tail = lambda new: jnp.concatenate([new, jnp.zeros((rows - t_new, new.shape[1]), F32)], axis=0)
        emit(tail(nsa_new_ref[0]), tail(moba_new_ref[0]), tail(dsa_new_ref[0]))

    @pl.when(p * ppb > n_pages)
    def _():
        emit(jnp.zeros((rows, W_NSA), F32), jnp.zeros((rows, W_MOBA), F32), jnp.zeros((rows, W_DSA), F32))


def _assemble_call(table, cache_nsa, cache_moba, cache_dsa, l, nsa_new, moba_new, dsa_new, lpad):
    ns, n_pages = table.shape
    page = cache_nsa.shape[2]
    ppb = MOBA_BLOCK // page
    assert n_pages % ppb == 0 and lpad % (ppb * page) == 0 and nsa_new.shape[1] <= ppb * page
    n_blk = lpad // (ppb * page)

    def pool(w, j):
        return pl.BlockSpec((1, 1, w, page),
                            lambda b, p, tbl: (l, tbl[b, jnp.minimum(p * ppb + j, n_pages - 1)], 0, 0))

    new = lambda w: pl.BlockSpec((1, nsa_new.shape[1], w), lambda b, p, tbl: (b, 0, 0))
    out = lambda w: pl.BlockSpec((1, ppb * page, w), lambda b, p, tbl: (b, p, 0))
    widths = (LANES, LANES, 2 * LANES, W_MOBA, W_DSA_PAD)
    as_pages = lambda c, w: jnp.swapaxes(c.reshape(c.shape[:3] + (w,)), 2, 3)
    pools = (as_pages(cache_nsa, W_NSA), as_pages(cache_moba, W_MOBA), as_pages(cache_dsa, W_DSA))
    return pl.pallas_call(
        functools.partial(_assemble_kernel, n_pages=n_pages, ppb=ppb),
        grid_spec=pltpu.PrefetchScalarGridSpec(
            num_scalar_prefetch=1,
            grid=(ns, n_blk),
            in_specs=[pool(w, j) for j in range(ppb) for w in (W_NSA, W_MOBA, W_DSA)]
            + [new(W_NSA), new(W_MOBA), new(W_DSA)],
            out_specs=[out(w) for w in widths],
        ),
        out_shape=[jax.ShapeDtypeStruct((ns, lpad, w), MXU) for w in widths],
        compiler_params=_params(("parallel", "arbitrary")),
        name="assemble_pages",
    )(table, *(pools * ppb), nsa_new, moba_new, dsa_new)


def _win_kernel(buf_ref, new_ref, winb_ref, wout_ref):
    buf = buf_ref[0, 0]
    new = new_ref[0]
    pad = winb_ref.shape[1] - buf.shape[0] - new.shape[0]
    winb_ref[0] = jnp.concatenate([buf, new, jnp.zeros((pad, buf.shape[1]), F32)], axis=0).astype(MXU)
    wout_ref[0] = jnp.concatenate([buf[new.shape[0]:], new], axis=0)


def _win_call(state_win, l, win_new, rows_pad):
    ns, w_buf = state_win.shape[1], state_win.shape[2]
    t_new = win_new.shape[1]
    return pl.pallas_call(
        _win_kernel,
        grid=(ns,),
        in_specs=[pl.BlockSpec((1, 1, w_buf, W_WIN), lambda b: (l, b, 0, 0)),
                  pl.BlockSpec((1, t_new, W_WIN), lambda b: (b, 0, 0))],
        out_specs=[pl.BlockSpec((1, rows_pad, W_WIN), lambda b: (b, 0, 0)),
                   pl.BlockSpec((1, w_buf, W_WIN), lambda b: (b, 0, 0))],
        out_shape=[jax.ShapeDtypeStruct((ns, rows_pad, W_WIN), MXU),
                   jax.ShapeDtypeStruct((ns, w_buf, W_WIN), F32)],
        compiler_params=_params(("parallel",)),
        name="window_buffer",
    )(state_win.reshape(state_win.shape[:3] + (W_WIN,)), win_new)


def _cmp_kernel(kx_ref, vx_ref, pe_ref, w1_ref, w2_ref, kc_ref, vc_ref):
    nch = kx_ref.shape[1]
    nout = kc_ref.shape[1]
    for j, (x_ref, o_ref) in enumerate(((kx_ref, kc_ref), (vx_ref, vc_ref))):
        x = x_ref[0].astype(F32)
        a = _dot((x + pe_ref[j, 0]).astype(MXU), w1_ref[j, 0])
        b = _dot((x + pe_ref[j, 1]).astype(MXU), w1_ref[j, 1])
        hid = a + pltpu.roll(b, nch - 1, axis=0)
        out = _dot(jax.nn.gelu(hid).astype(MXU), w2_ref[j])
        o_ref[0, 0:nch] = out.astype(o_ref.dtype)
        if nout > nch:
            o_ref[0, nch:nout] = jnp.zeros((nout - nch, LANES), o_ref.dtype)


def _cmp_call(kx, vx, pe_e, w1_e, w2_e, nout):
    ns, nch, wx = kx.shape
    full = lambda a: pl.BlockSpec(a.shape, lambda b: (0,) * a.ndim)
    return pl.pallas_call(
        _cmp_kernel,
        grid=(ns,),
        in_specs=[pl.BlockSpec((1, nch, wx), lambda b: (b, 0, 0))] * 2 + [full(pe_e), full(w1_e), full(w2_e)],
        out_specs=[pl.BlockSpec((1, nout, LANES), lambda b: (b, 0, 0))] * 2,
        out_shape=[jax.ShapeDtypeStruct((ns, nout, LANES), MXU)] * 2,
        compiler_params=_params(("parallel",)),
        name="nsa_compress",
    )(kx, vx, pe_e, w1_e, w2_e)


def _cmp_weights(cmp_pe_l, cmp_w1_l, cmp_w2_l):
    half_rows = CMP_LEN // 2
    kv = NSA_KV_HEADS
    w1 = cmp_w1_l.reshape(2, 2, half_rows, HEAD_DIM, CMP_HIDDEN)
    w1_e = jnp.zeros((2, 2, half_rows, kv, HEAD_DIM, kv, CMP_HIDDEN), F32)
    for k in range(kv):
        w1_e = w1_e.at[:, :, :, k, :, k, :].set(w1)
    w1_e = w1_e.reshape(2, 2, half_rows * kv * HEAD_DIM, kv * CMP_HIDDEN).astype(MXU)
    w2_e = jnp.zeros((2, kv, CMP_HIDDEN, kv, HEAD_DIM), F32)
    for k in range(kv):
        w2_e = w2_e.at[:, k, :, k, :].set(cmp_w2_l)
    w2_e = w2_e.reshape(2, kv * CMP_HIDDEN, kv * HEAD_DIM).astype(MXU)
    pe = cmp_pe_l.reshape(2, 2, half_rows, 1, HEAD_DIM)
    pe_e = jnp.broadcast_to(pe, (2, 2, half_rows, kv, HEAD_DIM)).reshape(2, 2, 1, half_rows * kv * HEAD_DIM)
    return pe_e, w1_e, w2_e


def _online_step(sm, vt, m_scr, l_scr, acc_scr, rows):
    m_prev = m_scr[rows, :]
    m_new = jnp.maximum(m_prev, jnp.max(sm, axis=-1, keepdims=True))
    alpha = jnp.exp(m_prev - m_new)
    p = jnp.exp(sm - m_new)
    l_scr[rows, :] = alpha * l_scr[rows, :] + jnp.sum(p, axis=-1, keepdims=True)
    acc_scr[rows, :] = alpha * acc_scr[rows, :] + _dot(p.astype(MXU), vt)
    m_scr[rows, :] = m_new


def _reset_state(m_scr, l_scr, acc_scr):
    m_scr[...] = jnp.full(m_scr.shape, M_INIT, F32)
    l_scr[...] = jnp.zeros(l_scr.shape, F32)
    acc_scr[...] = jnp.zeros(acc_scr.shape, F32)


def _biased_softmax(sm):
    m = jnp.maximum(jnp.max(sm, axis=-1, keepdims=True), M_INIT)
    e = jnp.exp(sm - m)
    den = jnp.sum(e, axis=-1, keepdims=True)
    return e * jnp.where(den > 0.0, 1.0 / den, 0.0)


def _add_bias(s, bias, qb):
    c = bias.shape[1]
    outer = bias.shape[0] // qb
    groups = s.shape[0] // bias.shape[0]
    return (s.reshape(outer, groups, qb, c) + bias.reshape(outer, 1, qb, c)).reshape(s.shape)


def _stack_heads(ref, tiles):
    parts = [ref[0, :, t * LANES:(t + 1) * LANES].astype(F32) for t in tiles]
    return jnp.concatenate(parts, axis=0).astype(MXU)


def _top_blocks(imp, t0, qb):
    qp = max(qb, LANES)
    kv = imp.shape[0] // qb
    blk = lax.broadcasted_iota(jnp.int32, (LANES, 1), 0)
    tq = t0 + lax.broadcasted_iota(jnp.int32, (1, qp), 1)
    tb = tq // SEL_BLOCK
    forced = (blk == 0) | (blk == tb) | (blk == tb - 1)
    admissible = blk * SEL_BLOCK <= tq
    vals = []
    for k in range(kv):
        part = imp[k * qb:(k + 1) * qb]
        if qp > qb:
            part = jnp.concatenate([part, jnp.zeros((qp - qb, LANES), F32)], axis=0)
        vals.append(jnp.where(forced, BIG, jnp.where(admissible, part.T, NEG)))
    val = jnp.concatenate(vals, axis=1)
    sel = jnp.zeros(val.shape, F32)
    for _ in range(SEL_TOPN):
        top = jnp.max(val, axis=0, keepdims=True)
        idx = jnp.min(jnp.where(val == top, blk, LANES), axis=0, keepdims=True)
        pick = blk == idx
        sel = jnp.where(pick, 1.0, sel)
        val = jnp.where(pick, BELOW_NEG, val)
    return jnp.concatenate([sel[:, k * qp:(k + 1) * qp].T[:qb] for k in range(kv)], axis=0)


def _nsa_kernel(qa_ref, qar_ref, gw_ref, kc_ref, vc_ref, kvs_ref, win_ref, ovl_ref, exp_ref, o_ref,
                m_scr, l_scr, acc_scr, *, qb, q0, ch, win_pos0, slab):
    t0 = q0 + pl.program_id(1) * qb
    tq = t0 + lax.broadcasted_iota(jnp.int32, (qb, 1), 0)
    ncp = kc_ref.shape[1]
    win_rows = win_ref.shape[1]
    g, kv = NSA_GROUP, NSA_KV_HEADS
    qa = _stack_heads(qa_ref, range(NSA_HEADS))
    qr = _stack_heads(qar_ref, range(NSA_HEADS))

    c_end = lax.broadcasted_iota(jnp.int32, (1, ncp), 1) * CMP_STRIDE + (CMP_LEN - 1)
    bias_c = jnp.where(c_end <= tq, 0.0, NEG)
    p_c = _biased_softmax(_add_bias(_dot_t(qa, kc_ref[0]), bias_c, qb))
    o_cmp = _dot(p_c.astype(MXU), vc_ref[0])
    p4 = p_c.reshape(kv, g, qb, ncp)
    p_sum = p4[:, 0]
    for j in range(1, g):
        p_sum = p_sum + p4[:, j]
    p_sum = p_sum.reshape(kv * qb, ncp)
    p_hi = p_sum.astype(MXU)
    p_lo = (p_sum - p_hi.astype(F32)).astype(MXU)
    imp = _dot(p_hi, ovl_ref[...]) + _dot(p_lo, ovl_ref[...])
    sel_b = _top_blocks(imp, t0, qb).astype(MXU)

    _reset_state(m_scr, l_scr, acc_scr)

    def sel_step(c, causal):
        off = pl.multiple_of(c * ch, ch)
        kt = kvs_ref[0, pl.ds(off, ch), 0:LANES]
        vt = kvs_ref[0, pl.ds(off, ch), LANES:2 * LANES]
        bias = (_dot(sel_b, exp_ref[:, pl.ds(off, ch)]) - 1.0) * BIG
        if causal:
            k_pos = off + lax.broadcasted_iota(jnp.int32, (1, ch), 1)
            bias = _add_bias(bias, jnp.where(k_pos <= tq, 0.0, NEG), qb)
        _online_step(_add_bias(_dot_t(qr, kt), bias, qb), vt, m_scr, l_scr, acc_scr, slice(None))

    c_diag = t0 // ch

    def sel_body(c, carry):
        sel_step(c, False)
        return carry

    lax.fori_loop(0, c_diag, sel_body, 0)
    sel_step(c_diag, True)
    o_sel = acc_scr[...] / l_scr[...]

    if win_rows > slab:
        w_start = pl.multiple_of(jnp.clip(t0 - WINDOW - win_pos0, 0, win_rows - slab), 16)
    else:
        w_start = 0
    w_pos = win_pos0 + w_start + lax.broadcasted_iota(jnp.int32, (1, slab), 1)
    dist = tq - w_pos
    bias_w = jnp.where((dist >= 0) & (dist <= WINDOW), 0.0, NEG)
    kt = win_ref[0, pl.ds(w_start, slab), 0:LANES]
    vt = win_ref[0, pl.ds(w_start, slab), LANES:2 * LANES]
    p_w = _biased_softmax(_add_bias(_dot_t(qr, kt), bias_w, qb))
    o_win = _dot(p_w.astype(MXU), vt)

    gate = jax.nn.sigmoid(gw_ref[0])
    for h in range(NSA_HEADS):
        r = slice(h * qb, (h + 1) * qb)
        o = (gate[:, 3 * h:3 * h + 1] * o_cmp[r] + gate[:, 3 * h + 1:3 * h + 2] * o_sel[r]
             + gate[:, 3 * h + 2:3 * h + 3] * o_win[r])
        o_ref[0, :, h * LANES:(h + 1) * LANES] = o.astype(o_ref.dtype)


def _nsa_call(qa, qar, gw, kc, vc, kvs, winb, ovl, expand, *, qb, q0, ch, win_pos0):
    ns, t, _ = qa.shape
    assert ch % qb == 0 and (q0 % qb == 0 or ch == kvs.shape[1])
    slab = min(-(-(WINDOW + qb) // LANES) * LANES, winb.shape[1])
    tok = lambda w: pl.BlockSpec((1, qb, w), lambda b, i: (b, i, 0))
    seq = lambda a: pl.BlockSpec((1,) + a.shape[1:], lambda b, i: (b, 0, 0))
    full = lambda a: pl.BlockSpec(a.shape, lambda b, i: (0,) * a.ndim)
    rows = NSA_HEADS * qb
    return pl.pallas_call(
        functools.partial(_nsa_kernel, qb=qb, q0=q0, ch=ch, win_pos0=win_pos0, slab=slab),
        grid=(ns, t // qb),
        in_specs=[tok(W_QA), tok(W_QA), tok(W_GW), seq(kc), seq(vc), seq(kvs), seq(winb), full(ovl), full(expand)],
        out_specs=tok(W_ONSA),
        out_shape=jax.ShapeDtypeStruct((ns, t, W_ONSA), MXU),
        scratch_shapes=[pltpu.VMEM((rows, 1), F32), pltpu.VMEM((rows, 1), F32), pltpu.VMEM((rows, LANES), F32)],
        compiler_params=_params(("parallel", "arbitrary")),
        name="nsa_mixer",
    )(qa, qar, gw, kc, vc, kvs, winb, ovl, expand)


def _nsa_tables(l_total, ncp, lpad):
    n_cmp = (l_total - CMP_LEN) // CMP_STRIDE + 1
    c = np.arange(ncp)[:, None]
    n = np.arange(LANES)[None, :]
    ovl = ((c * CMP_STRIDE < n * SEL_BLOCK + SEL_BLOCK) & (c * CMP_STRIDE + CMP_LEN - 1 >= n * SEL_BLOCK)
           & (c < n_cmp))
    s = np.arange(lpad)[None, :]
    expand = (s // SEL_BLOCK) == np.arange(LANES)[:, None]
    return jnp.asarray(ovl, MXU), jnp.asarray(expand, MXU)


def _moba_kernel(q_ref, kv_ref, avg_ref, o_ref, kmean_scr, m_scr, l_scr, acc_scr, *, qb, q0, ch):
    i = pl.program_id(1)
    nh = MOBA_HEADS
    k_tiles = -(-nh // 2)
    blocks_per_chunk = ch // MOBA_BLOCK

    @pl.when(i == 0)
    def _():
        kmean_scr[...] = _dot(avg_ref[...], kv_ref[0, :, 0:k_tiles * LANES])

    t0 = q0 + i * qb
    tq = t0 + lax.broadcasted_iota(jnp.int32, (qb, 1), 0)
    own = tq // MOBA_BLOCK
    lane = lax.broadcasted_iota(jnp.int32, (1, LANES), 1)
    qs = [q_ref[0, :, h * LANES:(h + 1) * LANES] for h in range(nh)]
    kmean = kmean_scr[...].astype(MXU)
    vals = []
    for h in range(nh):
        s_blk = _dot_t(qs[h], kmean[:, (h // 2) * LANES:(h // 2 + 1) * LANES])
        vals.append(jnp.where(lane < own, s_blk, NEG))
    val = jnp.concatenate(vals, axis=0)
    picks = []
    for _ in range(MOBA_TOPK):
        top = jnp.max(val, axis=-1, keepdims=True)
        idx = jnp.min(jnp.where(val == top, lane, LANES), axis=-1, keepdims=True)
        picks.append(idx)
        val = jnp.where(lane == idx, BELOW_NEG, val)

    _reset_state(m_scr, l_scr, acc_scr)
    pos_in_block = lax.broadcasted_iota(jnp.int32, (1, MOBA_BLOCK), 1)

    def step(c, with_own):
        off = pl.multiple_of(c * ch, ch)
        sms, vts = [], []
        for h in range(nh):
            r = slice(h * qb, (h + 1) * qb)
            kt = kv_ref[0, pl.ds(off, ch), (h // 2) * LANES:(h // 2 + 1) * LANES]
            v_tile = (nh + h) // 2
            vts.append(kv_ref[0, pl.ds(off, ch), v_tile * LANES:(v_tile + 1) * LANES])
            s = _dot_t(qs[h], kt)
            parts = []
            for j in range(blocks_per_chunk):
                n = c * blocks_per_chunk + j
                chosen = ((picks[0][r] == n) | (picks[1][r] == n) | (picks[2][r] == n)) & (n < own)
                bias = jnp.where(chosen, 0.0, NEG)
                if with_own:
                    k_pos = n * MOBA_BLOCK + pos_in_block
                    bias = jnp.maximum(bias, jnp.where((own == n) & (k_pos <= tq), 0.0, NEG))
                parts.append(s[:, j * MOBA_BLOCK:(j + 1) * MOBA_BLOCK] + bias)
            sms.append(parts[0] if len(parts) == 1 else jnp.concatenate(parts, axis=1))
        sm = jnp.concatenate(sms, axis=0)
        m_prev = m_scr[...]
        m_new = jnp.maximum(m_prev, jnp.max(sm, axis=-1, keepdims=True))
        alpha = jnp.exp(m_prev - m_new)
        p = jnp.exp(sm - m_new)
        l_scr[...] = alpha * l_scr[...] + jnp.sum(p, axis=-1, keepdims=True)
        pv = jnp.concatenate([_dot(p[h * qb:(h + 1) * qb].astype(MXU), vts[h]) for h in range(nh)], axis=0)
        acc_scr[...] = alpha * acc_scr[...] + pv
        m_scr[...] = m_new

    c_diag = t0 // ch

    def body(c, carry):
        step(c, False)
        return carry

    lax.fori_loop(0, c_diag, body, 0)
    step(c_diag, True)
    o = acc_scr[...] / l_scr[...]
    for h in range(nh):
        o_ref[0, :, h * LANES:(h + 1) * LANES] = o[h * qb:(h + 1) * qb].astype(o_ref.dtype)


def _moba_call(q, kv, *, qb, q0, ch):
    ns, t, _ = q.shape
    assert ch % MOBA_BLOCK == 0 and ch % qb == 0 and (q0 % qb == 0 or ch == kv.shape[1])
    rows = MOBA_HEADS * qb
    k_tiles = -(-MOBA_HEADS // 2)
    lpad = kv.shape[1]
    avg = jnp.asarray((np.arange(lpad)[None, :] // MOBA_BLOCK == np.arange(LANES)[:, None]) / MOBA_BLOCK, MXU)
    return pl.pallas_call(
        functools.partial(_moba_kernel, qb=qb, q0=q0, ch=ch),
        grid=(ns, t // qb),
        in_specs=[pl.BlockSpec((1, qb, W_QB), lambda b, i: (b, i, 0)),
                  pl.BlockSpec((1,) + kv.shape[1:], lambda b, i: (b, 0, 0)),
                  pl.BlockSpec(avg.shape, lambda b, i: (0, 0))],
        out_specs=pl.BlockSpec((1, qb, W_OMOBA), lambda b, i: (b, i, 0)),
        out_shape=jax.ShapeDtypeStruct((ns, t, W_OMOBA), MXU),
        scratch_shapes=[pltpu.VMEM((LANES, k_tiles * LANES), F32),
                        pltpu.VMEM((rows, 1), F32), pltpu.VMEM((rows, 1), F32), pltpu.VMEM((rows, LANES), F32)],
        compiler_params=_params(("parallel", "arbitrary")),
        name="moba_mixer",
    )(q, kv, avg)


def _online_step_t(sm, vt_t, m_scr, l_scr, acc_scr, cols):
    m_prev = m_scr[:, cols]
    m_new = jnp.maximum(m_prev, jnp.max(sm, axis=0, keepdims=True))
    alpha = jnp.exp(m_prev - m_new)
    p = jnp.exp(sm - m_new)
    l_scr[:, cols] = alpha * l_scr[:, cols] + jnp.sum(p, axis=0, keepdims=True)
    acc_scr[:, cols] = alpha * acc_scr[:, cols] + _dot(vt_t, p.astype(MXU))
    m_scr[:, cols] = m_new


def _softmax_t(sm):
    m = jnp.maximum(jnp.max(sm, axis=0, keepdims=True), M_INIT)
    e = jnp.exp(sm - m)
    den = jnp.sum(e, axis=0, keepdims=True)
    return e * jnp.where(den > 0.0, 1.0 / den, 0.0)


def _transpose_tile(ref, t):
    return ref[0, :, t * LANES:(t + 1) * LANES].astype(F32).T.astype(MXU)


def _tile_lanes(x, n):
    return x if n == 1 else jnp.concatenate([x] * n, axis=1)


def _chunk_pairs(n, scores, update):
    def pair(c2, carry):
        a = scores(2 * c2)
        b = scores(2 * c2 + 1)
        update(*a)
        update(*b)
        return carry

    lax.fori_loop(0, n // 2, pair, 0)

    @pl.when(n % 2 == 1)
    def _():
        update(*scores(n - 1))


def _moba_t_kernel(q_ref, k_ref, vt_ref, o_ref, kmean_scr, chosen_scr, m_scr, l_scr, acc_scr, *, qb):
    i = pl.program_id(1)
    lpad = k_ref.shape[1]
    nh = MOBA_HEADS
    k_tiles = -(-nh // 2)

    @pl.when(i == 0)
    def _():
        kmean_scr[...] = jnp.zeros(kmean_scr.shape, F32)
        ones = jnp.ones((8, MOBA_BLOCK), MXU)

        def mean_body(n, carry):
            off = pl.multiple_of(n * MOBA_BLOCK, MOBA_BLOCK)
            tot = _dot(ones, k_ref[0, pl.ds(off, MOBA_BLOCK), 0:k_tiles * LANES])
            kmean_scr[pl.ds(n, 1), :] = tot[0:1] * (1.0 / MOBA_BLOCK)
            return carry

        lax.fori_loop(0, lpad // MOBA_BLOCK, mean_body, 0)

    t0 = i * qb
    tq = t0 + lax.broadcasted_iota(jnp.int32, (1, qb), 1)
    own = tq // MOBA_BLOCK
    blk = lax.broadcasted_iota(jnp.int32, (LANES, 1), 0)
    q_t = [_transpose_tile(q_ref, h) for h in range(nh)]
    kmean = kmean_scr[...].astype(MXU)
    past = blk < own
    val = jnp.concatenate(
        [jnp.where(past, _dot(kmean[:, (h // 2) * LANES:(h // 2 + 1) * LANES], q_t[h]), NEG) for h in range(nh)],
        axis=1)
    past_all = _tile_lanes(past, nh)
    chosen = jnp.zeros(val.shape, F32)
    for _ in range(MOBA_TOPK):
        top = jnp.max(val, axis=0, keepdims=True)
        idx = jnp.min(jnp.where(val == top, blk, LANES), axis=0, keepdims=True)
        pick = blk == idx
        chosen = jnp.where(pick & past_all, 1.0, chosen)
        val = jnp.where(pick, BELOW_NEG, val)
    chosen_scr[...] = (chosen - 1.0) * BIG

    _reset_state(m_scr, l_scr, acc_scr)
    k_in_blk = lax.broadcasted_iota(jnp.int32, (MOBA_BLOCK, 1), 0)

    def scores(n, with_own=False):
        off = pl.multiple_of(n * MOBA_BLOCK, MOBA_BLOCK)
        bias_rows = chosen_scr[pl.ds(n, 1), :]
        if with_own:
            own_bias = jnp.where((own == n) & (off + k_in_blk <= tq), 0.0, NEG)
        out = []
        for h in range(nh):
            c = slice(h * qb, (h + 1) * qb)
            kt = k_ref[0, pl.ds(off, MOBA_BLOCK), (h // 2) * LANES:(h // 2 + 1) * LANES]
            v_tile = (nh + h) // 2 - (nh // 2)
            vt_t = vt_ref[0, v_tile * LANES:(v_tile + 1) * LANES, pl.ds(off, MOBA_BLOCK)]
            bias = jnp.maximum(bias_rows[:, c], own_bias) if with_own else bias_rows[:, c]
            out += [_dot(kt, q_t[h]) + bias, vt_t]
        return out

    def update(*sv):
        for h in range(nh):
            _online_step_t(sv[2 * h], sv[2 * h + 1], m_scr, l_scr, acc_scr, slice(h * qb, (h + 1) * qb))

    n_own = t0 // MOBA_BLOCK
    _chunk_pairs(n_own, scores, update)
    update(*scores(n_own, True))
    o_t = acc_scr[...] / l_scr[...]
    for h in range(nh):
        o_ref[0, :, h * LANES:(h + 1) * LANES] = o_t[:, h * qb:(h + 1) * qb].T.astype(o_ref.dtype)


def _moba_t_call(q, k, vt, *, qb):
    ns, t, _ = q.shape
    assert MOBA_BLOCK % qb == 0 and qb % LANES == 0
    cols = MOBA_HEADS * qb
    k_tiles = -(-MOBA_HEADS // 2)
    return pl.pallas_call(
        functools.partial(_moba_t_kernel, qb=qb),
        grid=(ns, t // qb),
        in_specs=[pl.BlockSpec((1, qb, W_QB), lambda b, i: (b, i, 0)),
                  pl.BlockSpec((1,) + k.shape[1:], lambda b, i: (b, 0, 0)),
                  pl.BlockSpec((1,) + vt.shape[1:], lambda b, i: (b, 0, 0))],
        out_specs=pl.BlockSpec((1, qb, W_OMOBA), lambda b, i: (b, i, 0)),
        out_shape=jax.ShapeDtypeStruct((ns, t, W_OMOBA), MXU),
        scratch_shapes=[pltpu.VMEM((LANES, k_tiles * LANES), F32), pltpu.VMEM((LANES, cols), F32),
                        pltpu.VMEM((1, cols), F32), pltpu.VMEM((1, cols), F32), pltpu.VMEM((LANES, cols), F32)],
        compiler_params=_params(("parallel", "arbitrary")),
        name="moba_mixer_t",
    )(q, k, vt)


def _nsa_t_kernel(qa_ref, qar_ref, gw_ref, kc_ref, vct_ref, kvs_ref, vst_ref, win_ref, wvt_ref, ovlt_ref, expt_ref,
                  o_ref, m_scr, l_scr, acc_scr, *, qb, ch, slab):
    t0 = pl.program_id(1) * qb
    tq = t0 + lax.broadcasted_iota(jnp.int32, (1, qb), 1)
    ncp = kc_ref.shape[1]
    win_rows = win_ref.shape[1]
    g, kv, nh = NSA_GROUP, NSA_KV_HEADS, NSA_HEADS
    qa_t = jnp.concatenate([_transpose_tile(qa_ref, h) for h in range(nh)], axis=1)
    qr_t = jnp.concatenate([_transpose_tile(qar_ref, h) for h in range(nh)], axis=1)

    c_end = lax.broadcasted_iota(jnp.int32, (ncp, 1), 0) * CMP_STRIDE + (CMP_LEN - 1)
    bias_c = jnp.where(c_end <= tq, 0.0, NEG)
    p_c = _softmax_t(_dot(kc_ref[0], qa_t) + _tile_lanes(bias_c, nh))
    o_cmp = _dot(vct_ref[0], p_c.astype(MXU))
    sums = []
    for k in range(kv):
        acc = p_c[:, (k * g) * qb:(k * g + 1) * qb]
        for j in range(1, g):
            acc = acc + p_c[:, (k * g + j) * qb:(k * g + j + 1) * qb]
        sums.append(acc)
    p_sum = jnp.concatenate(sums, axis=1)
    p_hi = p_sum.astype(MXU)
    p_lo = (p_sum - p_hi.astype(F32)).astype(MXU)
    imp = _dot(ovlt_ref[...], p_hi) + _dot(ovlt_ref[...], p_lo)
    blk = lax.broadcasted_iota(jnp.int32, (LANES, 1), 0)
    tb = tq // SEL_BLOCK
    forced = _tile_lanes((blk == 0) | (blk == tb) | (blk == tb - 1), kv)
    admissible = _tile_lanes(blk * SEL_BLOCK <= tq, kv)
    val = jnp.where(forced, BIG, jnp.where(admissible, imp, NEG))
    sel = jnp.zeros(val.shape, F32)
    for _ in range(SEL_TOPN):
        top = jnp.max(val, axis=0, keepdims=True)
        idx = jnp.min(jnp.where(val == top, blk, LANES), axis=0, keepdims=True)
        pick = blk == idx
        sel = jnp.where(pick, 1.0, sel)
        val = jnp.where(pick, BELOW_NEG, val)
    sel_b = sel.astype(MXU)

    _reset_state(m_scr, l_scr, acc_scr)

    def scores(c, causal=False):
        off = pl.multiple_of(c * ch, ch)
        kt = kvs_ref[0, pl.ds(off, ch), 0:LANES]
        bias = (_dot(expt_ref[pl.ds(off, ch), :], sel_b) - 1.0) * BIG
        if causal:
            k_pos = off + lax.broadcasted_iota(jnp.int32, (ch, 1), 0)
            bias = bias + _tile_lanes(jnp.where(k_pos <= tq, 0.0, NEG), kv)
        bias = jnp.concatenate([bias[:, k * qb:(k + 1) * qb] for k in range(kv) for _ in range(g)], axis=1)
        return _dot(kt, qr_t) + bias, vst_ref[0, :, pl.ds(off, ch)]

    def update(sm, vt_t):
        _online_step_t(sm, vt_t, m_scr, l_scr, acc_scr, slice(None))

    c_diag = t0 // ch
    _chunk_pairs(c_diag, scores, update)
    update(*scores(c_diag, True))
    o_sel = acc_scr[...] / l_scr[...]

    w_start = pl.multiple_of(jnp.clip(t0 - WINDOW, 0, win_rows - slab), LANES)
    w_pos = w_start + lax.broadcasted_iota(jnp.int32, (slab, 1), 0)
    dist = tq - w_pos
    bias_w = jnp.where((dist >= 0) & (dist <= WINDOW), 0.0, NEG)
    kt = win_ref[0, pl.ds(w_start, slab), 0:LANES]
    p_w = _softmax_t(_dot(kt, qr_t) + _tile_lanes(bias_w, nh))
    o_win = _dot(wvt_ref[0, :, pl.ds(w_start, slab)], p_w.astype(MXU))

    gate_t = jax.nn.sigmoid(gw_ref[0]).T
    for h in range(nh):
        c = slice(h * qb, (h + 1) * qb)
        o = (gate_t[3 * h:3 * h + 1] * o_cmp[:, c] + gate_t[3 * h + 1:3 * h + 2] * o_sel[:, c]
             + gate_t[3 * h + 2:3 * h + 3] * o_win[:, c])
        o_ref[0, :, h * LANES:(h + 1) * LANES] = o.T.astype(o_ref.dtype)


def _nsa_t_call(qa, qar, gw, kc, vc, kvs, winb, ovl, *, qb, ch):
    ns, t, _ = qa.shape
    lpad = kvs.shape[1]
    assert ch % qb == 0 and qb % LANES == 0 and lpad == t
    slab = WINDOW + qb
    swap = lambda a: jnp.swapaxes(a, 1, 2)
    vct, vst, wvt = swap(vc), swap(kvs[:, :, LANES:]), swap(winb[:, :, LANES:])
    ovlt = ovl.T
    expt = jnp.asarray((np.arange(lpad)[:, None] // SEL_BLOCK) == np.arange(LANES)[None, :], MXU)
    tok = lambda w: pl.BlockSpec((1, qb, w), lambda b, i: (b, i, 0))
    seq = lambda a: pl.BlockSpec((1,) + a.shape[1:], lambda b, i: (b, 0, 0))
    full = lambda a: pl.BlockSpec(a.shape, lambda b, i: (0,) * a.ndim)
    cols = NSA_HEADS * qb
    return pl.pallas_call(
        functools.partial(_nsa_t_kernel, qb=qb, ch=ch, slab=slab),
        grid=(ns, t // qb),
        in_specs=[tok(W_QA), tok(W_QA), tok(W_GW), seq(kc), seq(vct), seq(kvs), seq(vst), seq(winb), seq(wvt),
                  full(ovlt), full(expt)],
        out_specs=tok(W_ONSA),
        out_shape=jax.ShapeDtypeStruct((ns, t, W_ONSA), MXU),
        scratch_shapes=[pltpu.VMEM((1, cols), F32), pltpu.VMEM((1, cols), F32), pltpu.VMEM((LANES, cols), F32)],
        compiler_params=_params(("parallel", "arbitrary")),
        name="nsa_mixer_t",
    )(qa, qar, gw, kc, vct, kvs, vst, winb, wvt, ovlt, expt)


def _dsa_t_kernel(qc_ref, qi_ref, gw_ref, kv_ref, kvt_ref, o_ref, key_scr, cut_scr, m_scr, l_scr, acc_scr,
                  *, qb, ch, k_top):
    t0 = pl.program_id(1) * qb
    tq = t0 + lax.broadcasted_iota(jnp.int32, (1, qb), 1)
    lpad = kv_ref.shape[1]
    n_ch = t0 // ch + 1
    gw_t = gw_ref[0].T
    w_idx = [gw_t[GW_WI + h:GW_WI + h + 1] for h in range(IDX_HEADS)]
    qi_t = [_transpose_tile(qi_ref, h) for h in range(IDX_HEADS)]
    slab = 64
    k_in_ch = lax.broadcasted_iota(jnp.int32, (ch, 1), 0)

    def score_body(c, carry):
        off = pl.multiple_of(c * ch, ch)
        kt = kv_ref[0, pl.ds(off, ch), LANES:2 * LANES]
        sc = w_idx[0] * jnp.maximum(_dot(kt, qi_t[0]), 0.0)
        for h in range(1, IDX_HEADS):
            sc = sc + w_idx[h] * jnp.maximum(_dot(kt, qi_t[h]), 0.0)
        sc = jnp.where(sc == 0.0, 0.0, sc)
        bits = lax.bitcast_convert_type(sc, jnp.int32)
        key = bits ^ (lax.shift_right_arithmetic(bits, 31) & 0x7FFFFFFF)
        key_scr[pl.ds(off, ch), :] = jnp.where(off + k_in_ch <= tq, key, INT_MIN)
        return carry

    lax.fori_loop(0, n_ch, score_body, 0)

    def count(pred):
        def body(c, acc):
            off = pl.multiple_of(c * ch, ch)
            for j in range(ch // slab):
                acc = acc + jnp.where(pred(key_scr[pl.ds(off + j * slab, slab), :], off + j * slab), 1.0, 0.0)
            return acc
        acc = lax.fori_loop(0, n_ch, body, jnp.zeros((slab, qb), F32))
        return jnp.sum(acc, axis=0, keepdims=True)

    few = tq + 1 <= k_top
    bits_per_check = 8

    def bit_cond(st):
        it, _, n_sel = st
        open_cols = jnp.where(few | (n_sel == k_top), 0.0, 1.0)
        return (it < 32) & (jnp.max(open_cols) > 0.0)

    def bit_body(st):
        it, prefix, n_sel = st
        for j in range(bits_per_check):
            cand = prefix | lax.shift_left(jnp.int32(1), 31 - (it + j))
            cand_key = cand ^ INT_MIN
            cnt = count(lambda k, off: k >= cand_key)
            take = (cnt >= k_top) & (n_sel != k_top)
            prefix, n_sel = jnp.where(take, cand, prefix), jnp.where(take, cnt, n_sel)
        return it + bits_per_check, prefix, n_sel

    _, prefix, n_sel = lax.while_loop(
        bit_cond, bit_body,
        (jnp.int32(0), jnp.zeros((1, qb), jnp.int32), jnp.full((1, qb), float(2 * lpad), F32)))
    thr = jnp.where(few, INT_MIN + 1, jnp.maximum(prefix ^ INT_MIN, INT_MIN + 1))

    overflow = (n_sel > k_top) & jnp.logical_not(few)
    cut_scr[...] = jnp.full((1, qb), lpad, jnp.int32)

    @pl.when(jnp.max(jnp.where(overflow, 1.0, 0.0)) > 0.0)
    def _():
        need = k_top - count(lambda k, off: k > thr)

        def cut_body(it, lo_hi):
            lo, hi = lo_hi
            mid = (lo + hi) // 2
            cnt = count(lambda k, off: (k == thr) & (off + k_in_ch[0:slab] <= mid))
            ok = cnt >= need
            return jnp.where(ok, lo, mid + 1), jnp.where(ok, mid, hi)
        lo, _ = lax.fori_loop(0, max(1, math.ceil(math.log2(lpad))), cut_body,
                              (jnp.zeros((1, qb), jnp.int32), jnp.full((1, qb), lpad - 1, jnp.int32)))
        cut_scr[...] = jnp.where(overflow, lo, lpad)

    cut = cut_scr[...]

    q_t = jnp.concatenate([_transpose_tile(qc_ref, h) for h in range(DSA_HEADS)], axis=1)
    _reset_state(m_scr, l_scr, acc_scr)

    def scores(c):
        off = pl.multiple_of(c * ch, ch)
        kt = kv_ref[0, pl.ds(off, ch), 0:LANES]
        key = key_scr[pl.ds(off, ch), :]
        bias = jnp.where((key > thr) | ((key == thr) & (off + k_in_ch <= cut)), 0.0, NEG)
        return _dot(kt, q_t) + _tile_lanes(bias, DSA_HEADS), kvt_ref[0, :, pl.ds(off, ch)]

    def update(sm, vt_t):
        _online_step_t(sm, vt_t, m_scr, l_scr, acc_scr, slice(None))

    _chunk_pairs(n_ch, scores, update)
    o_t = acc_scr[...] / l_scr[...]
    for h in range(DSA_HEADS):
        o_ref[0, :, h * LANES:(h + 1) * LANES] = o_t[:, h * qb:(h + 1) * qb].T.astype(o_ref.dtype)


def _dsa_t_call(qc, qi, gw, kv, kvt, *, qb, ch, k_top):
    ns, t, _ = qc.shape
    lpad = kv.shape[1]
    assert ch % qb == 0 and qb % LANES == 0 and lpad == t
    cols = DSA_HEADS * qb
    tok = lambda w: pl.BlockSpec((1, qb, w), lambda b, i: (b, i, 0))
    seq = lambda a: pl.BlockSpec((1,) + a.shape[1:], lambda b, i: (b, 0, 0))
    return pl.pallas_call(
        functools.partial(_dsa_t_kernel, qb=qb, ch=ch, k_top=k_top),
        grid=(ns, t // qb),
        in_specs=[tok(W_QC), tok(W_QI), tok(W_GW), seq(kv), seq(kvt)],
        out_specs=tok(W_ODSA),
        out_shape=jax.ShapeDtypeStruct((ns, t, W_ODSA), MXU),
        scratch_shapes=[pltpu.VMEM((lpad, qb), jnp.int32), pltpu.VMEM((1, qb), jnp.int32),
                        pltpu.VMEM((1, cols), F32), pltpu.VMEM((1, cols), F32), pltpu.VMEM((LANES, cols), F32)],
        compiler_params=_params(("parallel", "arbitrary")),
        name="dsa_mixer_t",
    )(qc, qi, gw, kv, kvt)


def _dsa_kernel(qc_ref, qi_ref, gw_ref, kv_ref, o_ref, key_scr, cut_scr, m_scr, l_scr, acc_scr,
                *, qb, q0, ch, k_top):
    t0 = q0 + pl.program_id(1) * qb
    tq = t0 + lax.broadcasted_iota(jnp.int32, (qb, 1), 0)
    lpad = kv_ref.shape[1]
    n_ch = t0 // ch + 1
    gw = gw_ref[0]
    w_idx = [gw[:, GW_WI + h:GW_WI + h + 1] for h in range(IDX_HEADS)]
    q_idx = [qi_ref[0, :, h * LANES:(h + 1) * LANES] for h in range(IDX_HEADS)]
    lane_ch = lax.broadcasted_iota(jnp.int32, (1, ch), 1)

    def score_body(c, carry):
        off = pl.multiple_of(c * ch, ch)
        kt = kv_ref[0, pl.ds(off, ch), LANES:2 * LANES]
        sc = w_idx[0] * jnp.maximum(_dot_t(q_idx[0], kt), 0.0)
        for h in range(1, IDX_HEADS):
            sc = sc + w_idx[h] * jnp.maximum(_dot_t(q_idx[h], kt), 0.0)
        sc = jnp.where(sc == 0.0, 0.0, sc)
        bits = lax.bitcast_convert_type(sc, jnp.int32)
        key = bits ^ (lax.shift_right_arithmetic(bits, 31) & 0x7FFFFFFF)
        key_scr[:, pl.ds(off, ch)] = jnp.where(off + lane_ch <= tq, key, INT_MIN)
        return carry

    lax.fori_loop(0, n_ch, score_body, 0)

    def count(pred):
        def body(c, acc):
            off = pl.multiple_of(c * ch, ch)
            hit = jnp.where(pred(key_scr[:, pl.ds(off, ch)], off), 1.0, 0.0)
            part = hit[:, 0:LANES]
            for j in range(1, ch // LANES):
                part = part + hit[:, j * LANES:(j + 1) * LANES]
            return acc + part
        acc = lax.fori_loop(0, n_ch, body, jnp.zeros((qb, LANES), F32))
        return jnp.sum(acc, axis=-1, keepdims=True)

    few = tq + 1 <= k_top

    bits_per_check = 8

    def bit_cond(st):
        it, _, n_sel = st
        open_rows = jnp.where(few | (n_sel == k_top), 0.0, 1.0)
        return (it < 32) & (jnp.max(open_rows) > 0.0)

    def bit_body(st):
        it, prefix, n_sel = st
        for j in range(bits_per_check):
            cand = prefix | lax.shift_left(jnp.int32(1), 31 - (it + j))
            cand_key = cand ^ INT_MIN
            cnt = count(lambda k, off: k >= cand_key)
            take = (cnt >= k_top) & (n_sel != k_top)
            prefix, n_sel = jnp.where(take, cand, prefix), jnp.where(take, cnt, n_sel)
        return it + bits_per_check, prefix, n_sel

    _, prefix, n_sel = lax.while_loop(
        bit_cond, bit_body,
        (jnp.int32(0), jnp.zeros((qb, 1), jnp.int32), jnp.full((qb, 1), float(2 * lpad), F32)))
    thr = jnp.where(few, INT_MIN + 1, jnp.maximum(prefix ^ INT_MIN, INT_MIN + 1))

    overflow = (n_sel > k_top) & jnp.logical_not(few)
    cut_scr[...] = jnp.full((qb, 1), lpad, jnp.int32)

    @pl.when(jnp.max(jnp.where(overflow, 1.0, 0.0)) > 0.0)
    def _():
        need = k_top - count(lambda k, off: k > thr)

        def cut_body(it, lo_hi):
            lo, hi = lo_hi
            mid = (lo + hi) // 2
            cnt = count(lambda k, off: (k == thr) & (off + lane_ch <= mid))
            ok = cnt >= need
            return jnp.where(ok, lo, mid + 1), jnp.where(ok, mid, hi)
        lo, _ = lax.fori_loop(0, max(1, math.ceil(math.log2(lpad))), cut_body,
                              (jnp.zeros((qb, 1), jnp.int32), jnp.full((qb, 1), lpad - 1, jnp.int32)))
        cut_scr[...] = jnp.where(overflow, lo, lpad)

    cut = cut_scr[...]

    q = _stack_heads(qc_ref, range(DSA_HEADS))
    _reset_state(m_scr, l_scr, acc_scr)

    def att_body(c, carry):
        off = pl.multiple_of(c * ch, ch)
        kt = kv_ref[0, pl.ds(off, ch), 0:LANES]
        key = key_scr[:, pl.ds(off, ch)]
        bias = jnp.where((key > thr) | ((key == thr) & (off + lane_ch <= cut)), 0.0, NEG)
        _online_step(_add_bias(_dot_t(q, kt), bias, qb), kt, m_scr, l_scr, acc_scr, slice(None))
        return carry

    lax.fori_loop(0, n_ch, att_body, 0)
    o = acc_scr[...] / l_scr[...]
    for h in range(DSA_HEADS):
        o_ref[0, :, h * LANES:(h + 1) * LANES] = o[h * qb:(h + 1) * qb].astype(o_ref.dtype)


def _dsa_call(qc, qi, gw, kv, *, qb, q0, ch, k_top):
    ns, t, _ = qc.shape
    lpad = kv.shape[1]
    assert ch % qb == 0 and (q0 % qb == 0 or ch == lpad)
    rows = DSA_HEADS * qb
    tok = lambda w: pl.BlockSpec((1, qb, w), lambda b, i: (b, i, 0))
    return pl.pallas_call(
        functools.partial(_dsa_kernel, qb=qb, q0=q0, ch=ch, k_top=k_top),
        grid=(ns, t // qb),
        in_specs=[tok(W_QC), tok(W_QI), tok(W_GW), pl.BlockSpec((1,) + kv.shape[1:], lambda b, i: (b, 0, 0))],
        out_specs=tok(W_ODSA),
        out_shape=jax.ShapeDtypeStruct((ns, t, W_ODSA), MXU),
        scratch_shapes=[pltpu.VMEM((qb, lpad), jnp.int32), pltpu.VMEM((qb, 1), jnp.int32),
                        pltpu.VMEM((rows, 1), F32), pltpu.VMEM((rows, 1), F32), pltpu.VMEM((rows, LANES), F32)],
        compiler_params=_params(("parallel", "arbitrary")),
        name="dsa_mixer",
    )(qc, qi, gw, kv)


def _post_attn_kernel(oa_ref, ob_ref, oc_ref, x_ref, g1_ref, sh_ref, sc_ref, wo_ref, lng_ref, lnb_ref,
                      wr_ref, br_ref, x1_ref, h2_ref, gate_ref):
    x = x_ref[...]
    s, r, d = x.shape
    att = None
    off = 0
    for o_ref in (oa_ref, ob_ref, oc_ref):
        w = o_ref.shape[2]
        part = _dot(o_ref[...].reshape(s * r, w), wo_ref[off:off + w, :])
        att = part if att is None else att + part
        off += w
    att = att.reshape(s, r, d)
    x1 = _ln(ALPHA * x + g1_ref[...] * att) * lng_ref[...] + lnb_ref[...]
    x1_ref[...] = x1
    h2 = _ln(x1) * (1.0 + sc_ref[...]) + sh_ref[...]
    h2b = h2.reshape(s * r, d).astype(MXU)
    h2_ref[...] = h2b.reshape(s, r, d)

    logit = _dot(h2b, wr_ref[...]) + br_ref[...]
    lane = lax.broadcasted_iota(jnp.int32, logit.shape, 1)
    is_g = (lane >= N_EXPERTS) & (lane < N_EXPERTS + N_GROUPS)
    lg = jnp.where(is_g, logit, NEG)
    mg = jnp.max(lg, axis=-1, keepdims=True)
    g_lane = jnp.min(jnp.where(lg == mg, lane, LANES), axis=-1, keepdims=True)
    g_den = jnp.sum(jnp.where(is_g, jnp.exp(lg - mg), 0.0), axis=-1, keepdims=True)
    g_w = 1.0 / g_den
    g_idx = g_lane - N_EXPERTS
    in_grp = (lane >= g_idx * EXPERTS_PER_GROUP) & (lane < (g_idx + 1) * EXPERTS_PER_GROUP)
    le = jnp.where(in_grp, logit, NEG)
    me = jnp.max(le, axis=-1, keepdims=True)
    ee = jnp.where(in_grp, jnp.exp(le - me), 0.0)
    pe = ee / jnp.sum(ee, axis=-1, keepdims=True)
    p1 = jnp.max(pe, axis=-1, keepdims=True)
    i1 = jnp.min(jnp.where((pe == p1) & in_grp, lane, LANES), axis=-1, keepdims=True)
    rest = jnp.where(in_grp & (lane != i1), pe, -1.0)
    p2 = jnp.max(rest, axis=-1, keepdims=True)
    i2 = jnp.min(jnp.where(rest == p2, lane, LANES), axis=-1, keepdims=True)
    tot = p1 + p2
    gate = jnp.where(lane == i1, g_w * p1 / tot, jnp.where(lane == i2, g_w * p2 / tot, 0.0))
    gate_ref[...] = gate.reshape(s, r, LANES)


def _post_attn_call(o_parts, x, g1, sh2, sc2, wo, lng, lnb, wr, br, s_blk, r_blk):
    ns, rt, _ = x.shape
    tok = lambda w: pl.BlockSpec((s_blk, r_blk, w), lambda i, j: (i, j, 0))
    mod = pl.BlockSpec((s_blk, 1, D_MODEL), lambda i, j: (i, 0, 0))
    full = lambda a: pl.BlockSpec(a.shape, lambda i, j: (0,) * a.ndim)
    return pl.pallas_call(
        _post_attn_kernel,
        grid=(ns // s_blk, rt // r_blk),
        in_specs=[tok(o.shape[2]) for o in o_parts] + [tok(D_MODEL), mod, mod, mod,
                                                       full(wo), full(lng), full(lnb), full(wr), full(br)],
        out_specs=[tok(D_MODEL), tok(D_MODEL), tok(LANES)],
        out_shape=[jax.ShapeDtypeStruct((ns, rt, D_MODEL), F32),
                   jax.ShapeDtypeStruct((ns, rt, D_MODEL), MXU),
                   jax.ShapeDtypeStruct((ns, rt, LANES), F32)],
        compiler_params=_params(("parallel", "parallel")),
        name="wout_ln_router",
    )(*o_parts, x, g1, sh2, sc2, wo, lng, lnb, wr, br)


def _moe_kernel(h_ref, gate_ref, x_ref, g2_ref, wi_ref, wo_ref, lng_ref, lnb_ref, out_ref, acc_ref):
    e = pl.program_id(2)
    s, r, d = x_ref.shape

    @pl.when(e == 0)
    def _():
        acc_ref[...] = jnp.zeros_like(acc_ref)

    h = h_ref[...].reshape(s * r, d)
    ab = _dot(h, wi_ref[0])
    a, b = ab[:, :D_EXPERT], ab[:, D_EXPERT:]
    gate = gate_ref[...].reshape(s * r, LANES)
    lane = lax.broadcasted_iota(jnp.int32, gate.shape, 1)
    gcol = jnp.sum(jnp.where(lane == e, gate, 0.0), axis=-1, keepdims=True)
    hid = (a * jax.nn.sigmoid(a) * b * gcol).astype(MXU)
    acc_ref[...] += _dot(hid, wo_ref[0])

    @pl.when(e == N_EXPERTS - 1)
    def _():
        f = acc_ref[...].reshape(s, r, d)
        out_ref[...] = _ln(ALPHA * x_ref[...] + g2_ref[...] * f) * lng_ref[...] + lnb_ref[...]


def _moe_call(h2, gate, x1, g2, wi, wo, lng, lnb, s_blk, r_blk):
    ns, rt, _ = x1.shape
    tok = lambda w: pl.BlockSpec((s_blk, r_blk, w), lambda i, j, e: (i, j, 0))
    mod = pl.BlockSpec((s_blk, 1, D_MODEL), lambda i, j, e: (i, 0, 0))
    vec = pl.BlockSpec((1, 1, D_MODEL), lambda i, j, e: (0, 0, 0))
    return pl.pallas_call(
        _moe_kernel,
        grid=(ns // s_blk, rt // r_blk, N_EXPERTS),
        in_specs=[tok(D_MODEL), tok(LANES), tok(D_MODEL), mod,
                  pl.BlockSpec((1, D_MODEL, 2 * D_EXPERT), lambda i, j, e: (e, 0, 0)),
                  pl.BlockSpec((1, D_EXPERT, D_MODEL), lambda i, j, e: (e, 0, 0)),
                  vec, vec],
        out_specs=tok(D_MODEL),
        out_shape=jax.ShapeDtypeStruct((ns, rt, D_MODEL), F32),
        scratch_shapes=[pltpu.VMEM((s_blk * r_blk, D_MODEL), F32)],
        compiler_params=_params(("parallel", "parallel", "arbitrary")),
        name="moe_ln",
    )(h2, gate, x1, g2, wi, wo, lng, lnb)


def _mixers(pr, cmp_w, past, cfg):
    (nsa, win, moba, dsa, gw, qa, qar, qb_, qc, qi, kc, vc, kvs, winb, mobab, dsab) = pr
    ns, t, _ = qa.shape
    if past is None:
        l_total, lpad, win_pos0 = t, t, 0
        win_out = win[:, -min(WINDOW, t):]
    else:
        cache_nsa, cache_moba, cache_dsa, state_win, table, l = past
        past_len = table.shape[1] * cache_nsa.shape[2]
        l_total = past_len + t
        lpad = -(-l_total // MOBA_BLOCK) * MOBA_BLOCK
        kc, vc, kvs, mobab, dsab = _assemble_call(table, cache_nsa, cache_moba, cache_dsa, l,
                                                 nsa, moba, dsa, lpad)
        w_buf = state_win.shape[2]
        rows_pad = -(-(w_buf + t) // LANES) * LANES
        winb, win_out = _win_call(state_win, l, win, rows_pad)
        win_pos0 = past_len - w_buf
    q0 = l_total - t
    chunk_rows = LANES * CMP_STRIDE
    nch = lpad // CMP_STRIDE
    ncp = -(-nch // (2 * LANES)) * 2 * LANES
    kcmp, vcmp = _cmp_call(kc.reshape(ns, nch, chunk_rows), vc.reshape(ns, nch, chunk_rows), *cmp_w, ncp)
    ovl, expand = _nsa_tables(l_total, ncp, lpad)
    qb = cfg['qb']
    k_top = min(DSA_TOPK, l_total // 4)
    if past is None and qb % LANES == 0:
        swap = lambda a: jnp.swapaxes(a, 1, 2)
        o_nsa = _nsa_t_call(qa, qar, gw, kcmp, vcmp, kvs, winb, ovl, qb=qb, ch=cfg['ch_nsa'])
        o_moba = _moba_t_call(qb_, mobab, swap(mobab[:, :, (MOBA_HEADS // 2) * LANES:]), qb=qb)
        o_dsa = _dsa_t_call(qc, qi, gw, dsab, swap(dsab[:, :, :LANES]), qb=qb, ch=cfg['ch_dsa'], k_top=k_top)
    else:
        o_nsa = _nsa_call(qa, qar, gw, kcmp, vcmp, kvs, winb, ovl, expand,
                          qb=qb, q0=q0, ch=cfg['ch_nsa'], win_pos0=win_pos0)
        o_moba = _moba_call(qb_, mobab, qb=qb, q0=q0, ch=cfg['ch_moba'])
        o_dsa = _dsa_call(qc, qi, gw, dsab, qb=qb, q0=q0, ch=cfg['ch_dsa'], k_top=k_top)
    return (o_nsa, o_moba, o_dsa), win_out


def _layer(x, mod, wts, tabs, cmp_w, past, cfg):
    sh1, sc1, g1, sh2, sc2, g2 = mod
    w_perm, wo, ln1g, ln1b, ln2g, ln2b, wr, br, wei, weo = wts
    pr = _proj_call(x, sh1, sc1, w_perm, tabs, *cfg['blk_a'])
    o_parts, win_out = _mixers(pr, cmp_w, past, cfg)
    x1, h2, gate = _post_attn_call(o_parts, x, g1, sh2, sc2, wo, ln1g, ln1b, wr, br, *cfg['blk_a'])
    y = _moe_call(h2, gate, x1, g2, wei, weo, ln2g, ln2b, *cfg['blk_m'])
    return y, pr[0], pr[2], pr[3], win_out


def _layer_weights(p, l):
    n_pad = LANES - N_EXPERTS - N_GROUPS
    wr = jnp.concatenate([p['w_router_expert'][l], p['w_router_group'][l],
                          jnp.zeros((D_MODEL, n_pad), F32)], axis=1).astype(MXU)
    br = jnp.concatenate([p['b_router_expert'][l], p['b_router_group'][l], jnp.zeros((n_pad,), F32)])[None, :]
    vec = lambda a: a[l].reshape(1, 1, D_MODEL)
    return (_permute_w_in(p['w_in'][l]), _pad_w_out(p['w_out'][l]), vec(p['ln1_g']), vec(p['ln1_b']),
            vec(p['ln2_g']), vec(p['ln2_b']), wr, br,
            p['w_expert_in'][l].astype(MXU), p['w_expert_out'][l].astype(MXU))


def kernel(x_prompt, x_sample, c_prompt, c_sample, cache_nsa, cache_moba, cache_dsa, state_win, page_table, w_ada, b_ada, w_in, cmp_pe, cmp_w1, cmp_w2, w_out, ln1_g, ln1_b, ln2_g, ln2_b, w_router_group, b_router_group, w_router_expert, b_router_expert, w_expert_in, w_expert_out):
    p = dict(w_in=w_in, w_out=w_out, ln1_g=ln1_g, ln1_b=ln1_b, ln2_g=ln2_g, ln2_b=ln2_b,
             w_router_group=w_router_group, b_router_group=b_router_group,
             w_router_expert=w_router_expert, b_router_expert=b_router_expert,
             w_expert_in=w_expert_in, w_expert_out=w_expert_out)
    nb_p, t_p, _ = x_prompt.shape
    nb_s, t_s, _ = x_sample.shape
    past_len = page_table.shape[1] * cache_nsa.shape[2]
    lpad_s = -(-(past_len + t_s) // MOBA_BLOCK) * MOBA_BLOCK

    n_c = nb_p + nb_s
    n_c_pad = -(-n_c // 8) * 8
    c_all = jnp.concatenate([c_prompt, c_sample, jnp.zeros((n_c_pad - n_c, D_MODEL), F32)], axis=0)
    mod_all = _ada_call(c_all, w_ada, b_ada)

    tabs_p = _rope_tables(jnp.arange(t_p))
    tabs_s = _rope_tables(past_len + jnp.arange(t_s))
    cfg_p = dict(blk_a=(1, 512), blk_m=(1, 1024), qb=128, ch_nsa=512, ch_moba=MOBA_BLOCK, ch_dsa=512)
    cfg_s = dict(blk_a=(32, t_s), blk_m=(nb_s, t_s), qb=t_s, ch_nsa=lpad_s, ch_moba=lpad_s, ch_dsa=lpad_s)

    y_p, y_s = x_prompt, x_sample
    st_p, st_s = [], []
    for l in range(DEPTH):
        mod_l = mod_all[l]
        split = lambda m: tuple(m[:, None, k * D_MODEL:(k + 1) * D_MODEL] for k in range(6))
        mod_p = split(mod_l[:nb_p])
        mod_s = split(mod_l[nb_p:n_c])
        wts = _layer_weights(p, l)
        cmp_w = _cmp_weights(cmp_pe[l], cmp_w1[l], cmp_w2[l])
        y_p, *sp = _layer(y_p, mod_p, wts, tabs_p, cmp_w, None, cfg_p)
        past = (cache_nsa, cache_moba, cache_dsa, state_win, page_table, l)
        y_s, *ss = _layer(y_s, mod_s, wts, tabs_s, cmp_w, past, cfg_s)
        st_p.append(sp)
        st_s.append(ss)

    def stack(st, k, tail):
        a = jnp.stack([s[k] for s in st])
        return a.reshape(a.shape[:3] + tail)

    nsa_t = (4, NSA_KV_HEADS, HEAD_DIM)
    moba_t = (2, MOBA_HEADS, HEAD_DIM)
    dsa_t = (3, HEAD_DIM)
    win_t = (2, NSA_KV_HEADS, HEAD_DIM)
    return (y_p, y_s, stack(st_p, 0, nsa_t), stack(st_s, 0, nsa_t), stack(st_p, 1, moba_t),
            stack(st_s, 1, moba_t), stack(st_p, 2, dsa_t), stack(st_s, 2, dsa_t),
            stack(st_p, 3, win_t), stack(st_s, 3, win_t))
```
